```python
import jax, jax.numpy as jnp
from jax import lax
import numpy as np

D_MODEL = 1024
BATCH = 8
SEQ = 4096
DEPTH = 2

GRID_W = 64
CTX_LEN = 256
HG_WIDTH = D_MODEL // 2
HG_HEADS = HG_WIDTH // 128
HG_DK = HG_WIDTH // HG_HEADS
HG_DV = HG_DK
ML_WIDTH = D_MODEL - HG_WIDTH
ML_HEADS = 4
ML_DK = ML_WIDTH // ML_HEADS // 2
ML_DV = ML_WIDTH // ML_HEADS
ML_CONV = 3
MIX_WIDTH = HG_WIDTH + ML_WIDTH
PROJ_DIM = 5 * HG_WIDTH + 2 * ML_HEADS * ML_DK + 2 * ML_WIDTH + 4 * ML_HEADS
CHUNK = 64
D_FF = 2816
N_EXPERTS = 8
TOP_K = 2
D_EXPERT = D_FF // TOP_K
DEEPNORM_ALPHA = (2 * DEPTH) ** 0.25
DEEPNORM_BETA = (8 * DEPTH) ** -0.25
LN_EPS = 1e-5
RMS_EPS = 1e-6

kernel_name = "hgrn2_mlstm_hybrid_dit_moe"


def _layer_norm(x, g, b):
    xf = x.astype(jnp.float32)
    mu = jnp.mean(xf, axis=-1, keepdims=True)
    var = jnp.mean(jnp.square(xf - mu), axis=-1, keepdims=True)
    y = (xf - mu) * lax.rsqrt(var + LN_EPS) * g.astype(jnp.float32) + b.astype(jnp.float32)
    return y.astype(x.dtype)


def _head_rms(x, g):
    return x * lax.rsqrt(jnp.mean(x * x, axis=-1, keepdims=True) + RMS_EPS) * g.astype(jnp.float32)


def _centred_dwconv(u, w, b):
    width = w.shape[0]
    half = width // 2
    n = u.shape[-2]
    up = jnp.pad(u, [(0, 0)] * (u.ndim - 2) + [(half, half), (0, 0)])
    out = b
    for t in range(width):
        out = out + up[..., t:t + n, :] * w[t]
    return out


def _to_chunks(t):
    bsz, n, h = t.shape[:3]
    t = t.reshape(bsz, n // CHUNK, CHUNK, h, *t.shape[3:])
    return jnp.moveaxis(t, (1, 3), (0, 2))


def _from_chunks(t):
    n, bsz, h, c = t.shape[:4]
    return jnp.moveaxis(t, (0, 2), (1, 3)).reshape(bsz, n * c, h, *t.shape[4:])


def _hgrn2_scan(q, v, log_f, k, s0):
    causal = jnp.tril(jnp.ones((CHUNK, CHUNK), dtype=bool))[:, :, None]

    def step(s, xs):
        qc, vc, fc, kc = xs
        g = jnp.cumsum(fc, axis=2)
        decay = jnp.exp(jnp.where(causal, g[:, :, :, None, :] - g[:, :, None, :, :], -jnp.inf))
        a = jnp.einsum("bhid,bhjd,bhijd->bhij", qc, kc, decay)
        o = jnp.einsum("bhij,bhjv->bhiv", a, vc) + jnp.einsum("bhid,bhdv->bhiv", qc * jnp.exp(g), s)
        g_end = g[:, :, -1, :]
        s_new = jnp.exp(g_end)[..., None] * s + jnp.einsum(
            "bhjd,bhjv->bhdv", kc * jnp.exp(g_end[:, :, None, :] - g), vc)
        return s_new, o

    xs = (_to_chunks(q), _to_chunks(v), _to_chunks(log_f), _to_chunks(k))
    s_fin, o = lax.scan(step, s0, xs)
    return _from_chunks(o), s_fin


def _mlstm_scan(q, k, v, log_i, log_f, state):
    causal = jnp.tril(jnp.ones((CHUNK, CHUNK), dtype=bool))

    def step(carry, xs):
        c_prev, n_prev, m_prev = carry
        qc, kc, vc, ic, fc = xs
        b = jnp.cumsum(fc, axis=-1)
        d = jnp.where(causal, b[..., :, None] - b[..., None, :] + ic[..., None, :], -jnp.inf)
        inter = b + m_prev[..., None]
        m = jnp.maximum(jnp.max(d, axis=-1), inter)
        s = jnp.einsum("bhid,bhjd->bhij", qc, kc) * jnp.exp(d - m[..., None])
        w_inter = jnp.exp(inter - m)[..., None]
        num = jnp.einsum("bhij,bhjv->bhiv", s, vc) + w_inter * jnp.einsum("bhid,bhdv->bhiv", qc, c_prev)
        den = jnp.sum(s, axis=-1) + w_inter[..., 0] * jnp.einsum("bhid,bhd->bhi", qc, n_prev)
        h = num / jnp.maximum(jnp.abs(den), jnp.exp(-m))[..., None]
        m_new = m[..., -1]
        w_k = jnp.exp(b[..., -1:] - b + ic - m_new[..., None])[..., None] * kc
        dec = jnp.exp(b[..., -1] + m_prev - m_new)
        c_new = dec[..., None, None] * c_prev + jnp.einsum("bhjd,bhjv->bhdv", w_k, vc)
        n_new = dec[..., None] * n_prev + jnp.sum(w_k, axis=2)
        return (c_new, n_new, m_new), h

    xs = (_to_chunks(q), _to_chunks(k), _to_chunks(v), _to_chunks(log_i), _to_chunks(log_f))
    state, h = lax.scan(step, state, xs)
    return _from_chunks(h), state


def _bidir(scan_fn, shared, dirs, init):
    out_f, st_f = scan_fn(*shared, *dirs[0], init[0])
    flip = lambda t: jnp.flip(t, axis=1)
    out_b, st_b = scan_fn(*[flip(t) for t in shared], *[flip(t) for t in dirs[1]], init[1])
    return out_f + flip(out_b), (st_f, st_b)


def _stream_features(z, conv_w, conv_b, lb, gate_b, rows):
    bsz, n, _ = z.shape
    f32 = jnp.float32
    sizes = (HG_WIDTH,) * 5 + (2 * ML_HEADS * ML_DK, ML_WIDTH, 4 * ML_HEADS, ML_WIDTH)
    cuts, acc = [], 0
    for s in sizes[:-1]:
        acc += s
        cuts.append(acc)
    hq, hf_fwd, hf_bwd, hi, hg, mqk, mv, mgates, mo = jnp.split(z, cuts, axis=-1)
    hg_dirs = []
    for zf, lbd in ((hf_fwd, lb[0]), (hf_bwd, lb[1])):
        zf = zf.astype(f32)
        lbd = lbd.astype(f32)
        log_f = jnp.logaddexp(jnp.log(lbd), jnp.log1p(-lbd) + jax.nn.log_sigmoid(zf))
        k = (1.0 - lbd) * jax.nn.sigmoid(-zf)
        hg_dirs.append((log_f.reshape(bsz, n, HG_HEADS, HG_DK), k.reshape(bsz, n, HG_HEADS, HG_DK)))
    if rows is None:
        qk = _centred_dwconv(mqk, conv_w, conv_b)
    else:
        qk = _centred_dwconv(mqk.reshape(bsz, rows, GRID_W, -1), conv_w, conv_b).reshape(bsz, n, -1)
    mq, mk = jnp.split(jax.nn.silu(qk), 2, axis=-1)
    gates = (mgates.astype(f32) + gate_b.reshape(-1).astype(f32)).reshape(bsz, n, 2, 2, ML_HEADS)
    ml_dirs = [(gates[:, :, d, 0], jax.nn.log_sigmoid(gates[:, :, d, 1])) for d in range(2)]
    return {
        "hg_q": jax.nn.silu(hq).astype(f32).reshape(bsz, n, HG_HEADS, HG_DK),
        "hg_v": hi.astype(f32).reshape(bsz, n, HG_HEADS, HG_DV),
        "hg_dirs": hg_dirs,
        "hg_gate": hg,
        "ml_q": mq.astype(f32).reshape(bsz, n, ML_HEADS, ML_DK) * (ML_DK ** -0.5),
        "ml_k": mk.astype(f32).reshape(bsz, n, ML_HEADS, ML_DK),
        "ml_v": mv.astype(f32).reshape(bsz, n, ML_HEADS, ML_DV),
        "ml_dirs": ml_dirs,
        "ml_gate": mo,
    }


def _mix_out(hg_o, ml_h, feats, hg_norm, ml_norm, w_out, dtype):
    bsz, n = hg_o.shape[:2]
    hg = _head_rms(hg_o, hg_norm) * jax.nn.silu(feats["hg_gate"].astype(jnp.float32)).reshape(bsz, n, HG_HEADS, HG_DV)
    ml = _head_rms(ml_h, ml_norm) * jax.nn.sigmoid(feats["ml_gate"].astype(jnp.float32)).reshape(bsz, n, ML_HEADS, ML_DV)
    y = jnp.concatenate([hg.reshape(bsz, n, HG_WIDTH), ml.reshape(bsz, n, ML_WIDTH)], axis=-1)
    return y.astype(dtype) @ w_out


def _token_mix(hx, hc, rows, w_in, conv_w, conv_b, lb, gate_b, hg_norm, ml_norm, w_out, with_ctx_out):
    bsz = hx.shape[0]
    f32 = jnp.float32
    fx = _stream_features(hx @ w_in, conv_w, conv_b, lb, gate_b, rows)
    fc = _stream_features(hc @ w_in, conv_w, conv_b, lb, gate_b, None)
    hg0 = jnp.zeros((bsz, HG_HEADS, HG_DK, HG_DV), f32)
    ml0 = (jnp.zeros((bsz, ML_HEADS, ML_DK, ML_DV), f32), jnp.zeros((bsz, ML_HEADS, ML_DK), f32),
           jnp.zeros((bsz, ML_HEADS), f32))
    hg_c, hg_state = _bidir(_hgrn2_scan, (fc["hg_q"], fc["hg_v"]), fc["hg_dirs"], (hg0, hg0))
    ml_c, ml_state = _bidir(_mlstm_scan, (fc["ml_q"], fc["ml_k"], fc["ml_v"]), fc["ml_dirs"], (ml0, ml0))
    hg_x, _ = _bidir(_hgrn2_scan, (fx["hg_q"], fx["hg_v"]), fx["hg_dirs"], hg_state)
    ml_x, _ = _bidir(_mlstm_scan, (fx["ml_q"], fx["ml_k"], fx["ml_v"]), fx["ml_dirs"], ml_state)
    out_x = _mix_out(hg_x, ml_x, fx, hg_norm, ml_norm, w_out, hx.dtype)
    out_c = _mix_out(hg_c, ml_c, fc, hg_norm, ml_norm, w_out, hc.dtype) if with_ctx_out else None
    return out_x, out_c


def _swiglu(h, w_gate_up, w_down):
    gate, up = jnp.split(h @ w_gate_up, 2, axis=-1)
    return (jax.nn.silu(gate) * up) @ w_down


def _moe_swiglu(h, router_w, router_b, w_gate_up, w_down):
    logits = (h @ router_w).astype(jnp.float32) + router_b.astype(jnp.float32)
    top_val, top_idx = lax.top_k(logits, TOP_K)
    top_p = jax.nn.softmax(top_val, axis=-1)
    combine = jnp.einsum("blk,blke->ble", top_p, jax.nn.one_hot(top_idx, N_EXPERTS, dtype=jnp.float32)).astype(h.dtype)
    out = jnp.zeros_like(h)
    for e in range(N_EXPERTS):
        out = out + combine[..., e:e + 1] * _swiglu(h, w_gate_up[e], w_down[e])
    return out


def _channel_mix(h, layer, ffn_w_gate_up, ffn_w_down, router_w, router_b, moe_w_gate_up, moe_w_down):
    j = layer // 2
    if layer % 2 == 0:
        return _swiglu(h, ffn_w_gate_up[j], ffn_w_down[j])
    return _moe_swiglu(h, router_w[j], router_b[j], moe_w_gate_up[j], moe_w_down[j])


def setup_inputs(seed: int = 0) -> dict:
    key = jax.random.key(seed)
    ks = jax.random.split(key, 24)
    nrm = jax.random.normal
    f32 = jnp.float32
    n_dense = (DEPTH + 1) // 2
    n_moe = DEPTH // 2
    d_in = D_MODEL ** -0.5
    ig_b = 0.1 * nrm(ks[10], (DEPTH, 2, 1, ML_HEADS), f32)
    fg_b = jnp.linspace(3.0, 6.0, ML_HEADS, dtype=f32) + 0.1 * nrm(ks[11], (DEPTH, 2, 1, ML_HEADS), f32)
    return {
        "x": nrm(ks[0], (BATCH, SEQ, D_MODEL), f32),
        "c": nrm(ks[1], (BATCH, D_MODEL), f32),
        "ctx": nrm(ks[2], (BATCH, CTX_LEN, D_MODEL), f32),
        "c_ctx": nrm(ks[3], (D_MODEL,), f32),
        "w_ada": 0.5 * d_in * nrm(ks[4], (DEPTH, D_MODEL, 6 * D_MODEL), f32),
        "b_ada": 0.02 * nrm(ks[5], (DEPTH, 6 * D_MODEL), f32),
        "w_in": d_in * nrm(ks[6], (DEPTH, D_MODEL, PROJ_DIM), f32),
        "ml_conv_w": (ML_CONV ** -0.5) * nrm(ks[7], (DEPTH, ML_CONV, 2 * ML_HEADS * ML_DK), f32),
        "ml_conv_b": 0.02 * nrm(ks[8], (DEPTH, 2 * ML_HEADS * ML_DK), f32),
        "hg_lower_bound": 0.1 * nrm(ks[9], (DEPTH, 2, HG_WIDTH), f32),
        "ml_gate_bias": jnp.concatenate([ig_b, fg_b], axis=2),
        "hg_norm": 1.0 + 0.05 * nrm(ks[12], (DEPTH, HG_HEADS, HG_DV), f32),
        "ml_norm": 1.0 + 0.05 * nrm(ks[13], (DEPTH, ML_HEADS, ML_DV), f32),
        "w_out": DEEPNORM_BETA * (MIX_WIDTH ** -0.5) * nrm(ks[14], (DEPTH, MIX_WIDTH, D_MODEL), f32),
        "ln_g": 1.0 + 0.05 * nrm(ks[15], (DEPTH, 2, D_MODEL), f32),
        "ln_b": 0.02 * nrm(ks[16], (DEPTH, 2, D_MODEL), f32),
        "ffn_w_gate_up": d_in * nrm(ks[17], (n_dense, D_MODEL, 2 * D_FF), f32),
        "ffn_w_down": DEEPNORM_BETA * (D_FF ** -0.5) * nrm(ks[18], (n_dense, D_FF, D_MODEL), f32),
        "router_w": d_in * nrm(ks[19], (n_moe, D_MODEL, N_EXPERTS), f32),
        "router_b": 0.01 * nrm(ks[20], (n_moe, N_EXPERTS), f32),
        "moe_w_gate_up": d_in * nrm(ks[21], (n_moe, N_EXPERTS, D_MODEL, 2 * D_EXPERT), f32),
        "moe_w_down": DEEPNORM_BETA * (D_EXPERT ** -0.5) * nrm(ks[22], (n_moe, N_EXPERTS, D_EXPERT, D_MODEL), f32),
    }


def reference(x, c, ctx, c_ctx, w_ada, b_ada, w_in, ml_conv_w, ml_conv_b, hg_lower_bound, ml_gate_bias,
              hg_norm, ml_norm, w_out, ln_g, ln_b, ffn_w_gate_up, ffn_w_down, router_w, router_b,
              moe_w_gate_up, moe_w_down):
    bsz, seq, d = x.shape
    rows = seq // GRID_W
    lb_all = jnp.cumsum(jax.nn.softmax(hg_lower_bound.astype(jnp.float32), axis=0), axis=0)
    lb_all = lb_all - lb_all[0]
    cond_x = jax.nn.silu(c)
    cond_c = jax.nn.silu(c_ctx)
    for l in range(DEPTH):
        last = l == DEPTH - 1
        mod_x = (cond_x @ w_ada[l] + b_ada[l]).reshape(bsz, 6, 1, d)
        mod_c = (cond_c @ w_ada[l] + b_ada[l]).reshape(6, 1, d)
        hx = x * (1.0 + mod_x[:, 1]) + mod_x[:, 0]
        hc = ctx * (1.0 + mod_c[1]) + mod_c[0]
        mix_x, mix_c = _token_mix(hx, hc, rows, w_in[l], ml_conv_w[l], ml_conv_b[l], lb_all[l], ml_gate_bias[l],
                                  hg_norm[l], ml_norm[l], w_out[l], not last)
        x = _layer_norm(DEEPNORM_ALPHA * x + mod_x[:, 2] * mix_x, ln_g[l, 0], ln_b[l, 0])
        if not last:
            ctx = _layer_norm(DEEPNORM_ALPHA * ctx + mod_c[2] * mix_c, ln_g[l, 0], ln_b[l, 0])
        hx = x * (1.0 + mod_x[:, 4]) + mod_x[:, 3]
        f_x = _channel_mix(hx, l, ffn_w_gate_up, ffn_w_down, router_w, router_b, moe_w_gate_up, moe_w_down)
        x = _layer_norm(DEEPNORM_ALPHA * x + mod_x[:, 5] * f_x, ln_g[l, 1], ln_b[l, 1])
        if not last:
            hc = ctx * (1.0 + mod_c[4]) + mod_c[3]
            f_c = _channel_mix(hc, l, ffn_w_gate_up, ffn_w_down, router_w, router_b, moe_w_gate_up, moe_w_down)
            ctx = _layer_norm(DEEPNORM_ALPHA * ctx + mod_c[5] * f_c, ln_g[l, 1], ln_b[l, 1])
    return x
```

```python
import functools

import numpy as np
import jax
import jax.numpy as jnp
from jax import lax
from jax.experimental import pallas as pl
from jax.experimental.pallas import tpu as pltpu

F32 = jnp.float32
BF16 = jnp.bfloat16

D_MODEL = 1024
CTX_LEN = 256
GRID_W = 64
HG_WIDTH = 512
HEADS = 4
HEAD_DIM = 128
ML_DK = 64
CHUNK = 64
D_FF = 2816
N_EXPERTS = 8
D_EXPERT = 1408
DEPTH = 2
DEEPNORM_ALPHA = (2 * DEPTH) ** 0.25
LN_EPS = 1e-5
RMS_EPS = 1e-6

TOKEN_BLOCK = 256
MOE_TOKEN_BLOCK = 512
PROJ_COLS = 8 * 512 + 128
N_LEVELS = 6
VMEM_LIMIT = 56 * 1024 * 1024

NT_DIMS = (((1,), (1,)), ((), ()))
TN_DIMS = (((0,), (0,)), ((), ()))


def _dot(a, b):
    return jnp.dot(a, b, preferred_element_type=F32)


def _dot_nt(a, b):
    return lax.dot_general(a, b, NT_DIMS, preferred_element_type=F32)


def _dot_tn(a, b):
    return lax.dot_general(a, b, TN_DIMS, preferred_element_type=F32)


def _sigmoid(z):
    return 1.0 / (1.0 + jnp.exp(-z))


def _log_sigmoid(z):
    return jnp.minimum(z, 0.0) - jnp.log(1.0 + jnp.exp(-jnp.abs(z)))


def _split3(x):
    hi = x.astype(BF16)
    r = x - hi.astype(F32)
    mid = r.astype(BF16)
    lo = (r - mid.astype(F32)).astype(BF16)
    return hi, mid, lo


def _layer_norm(r, g, b):
    mu = jnp.mean(r, axis=-1, keepdims=True)
    d = r - mu
    var = jnp.mean(d * d, axis=-1, keepdims=True)
    return d * lax.rsqrt(var + LN_EPS) * g + b


def _mod_kernel(c_ref, w_ref, b_ref, o_ref):
    c = c_ref[...]
    s = c * _sigmoid(c)
    o_ref[0] = jnp.dot(s, w_ref[0], preferred_element_type=F32,
                       precision=lax.Precision.HIGHEST) + b_ref[0]


def _modulation(cond, w_ada, b_ada):
    depth, d, n = w_ada.shape
    rows = cond.shape[0]
    nb = 1536
    return pl.pallas_call(
        _mod_kernel,
        out_shape=jax.ShapeDtypeStruct((depth, rows, n), F32),
        grid=(depth, n // nb),
        in_specs=[
            pl.BlockSpec((rows, d), lambda l, j: (0, 0)),
            pl.BlockSpec((1, d, nb), lambda l, j: (l, 0, j)),
            pl.BlockSpec((1, 1, nb), lambda l, j: (l, 0, j)),
        ],
        out_specs=pl.BlockSpec((1, rows, nb), lambda l, j: (l, 0, j)),
        compiler_params=pltpu.CompilerParams(
            dimension_semantics=("arbitrary", "arbitrary"), vmem_limit_bytes=VMEM_LIMIT),
        name="adaln_mod",
    )(cond, w_ada, b_ada.reshape(depth, 1, n))


def _inproj_kernel(n_lat_blocks, x_ref, mod_ref, w_ref, pr_ref, gb_ref,
                   hq_o, lff_o, lfb_o, hv_o, hg_o, mq_o, mk_o, mv_o, mo_o, mg_o):
    j = pl.program_id(1)
    x = x_ref[0]
    hx = (x * (1.0 + mod_ref[0, 0, 1:2, :]) + mod_ref[0, 0, 0:1, :]).astype(BF16)

    def proj(group):
        return _dot(hx, w_ref[:, group * 512:(group + 1) * 512])

    z = proj(0)
    hq_o[0] = (z * _sigmoid(z)).astype(BF16)
    for r, out in ((0, lff_o), (1, lfb_o)):
        z = proj(1 + r)
        a = pr_ref[6 + r:7 + r, :] + _log_sigmoid(z)
        llb = pr_ref[4 + r:5 + r, :]
        out[0] = jnp.maximum(llb, a) + jnp.log(1.0 + jnp.exp(-jnp.abs(llb - a)))
    hv_o[0] = proj(3).astype(BF16)
    z = proj(4)
    hg_o[0] = (z * _sigmoid(z)).astype(BF16)

    u = proj(5)
    rows = u.shape[0]
    seg_mask = jnp.where(j >= n_lat_blocks, CTX_LEN - 1, GRID_W - 1)
    pos = lax.broadcasted_iota(jnp.int32, (rows, 1), 0) & seg_mask
    u_prev = jnp.where(pos == 0, 0.0, pltpu.roll(u, 1, axis=0))
    u_next = jnp.where(pos == seg_mask, 0.0, pltpu.roll(u, rows - 1, axis=0))
    qk = pr_ref[3:4, :] + u_prev * pr_ref[0:1, :] + u * pr_ref[1:2, :] + u_next * pr_ref[2:3, :]
    qk = qk * _sigmoid(qk)
    mq_o[0] = (qk[:, :256] * (ML_DK ** -0.5)).astype(BF16)
    mk_o[0] = qk[:, 256:].astype(BF16)
    mv_o[0] = proj(6).astype(BF16)
    mo_o[0] = _sigmoid(proj(7)).astype(BF16)

    g = _dot(hx, w_ref[:, 4096:4224]) + gb_ref[...]
    col = lax.broadcasted_iota(jnp.int32, g.shape, 1)
    is_forget = jnp.logical_and((col & 4) != 0, col < 16)
    mg_o[0] = jnp.where(is_forget, _log_sigmoid(g), g)


def _inproj(xt, modall, w_bf, pr, gb, n_lat_blocks):
    b, t, d = xt.shape
    nblk = t // TOKEN_BLOCK
    tok = lambda width: pl.BlockSpec((1, TOKEN_BLOCK, width), lambda i, j: (i, j, 0))
    shp = lambda width, dt: jax.ShapeDtypeStruct((b, t, width), dt)
    return pl.pallas_call(
        functools.partial(_inproj_kernel, n_lat_blocks),
        out_shape=(shp(512, BF16), shp(512, F32), shp(512, F32), shp(512, BF16), shp(512, BF16),
                   shp(256, BF16), shp(256, BF16), shp(512, BF16), shp(512, BF16), shp(128, F32)),
        grid=(b, nblk),
        in_specs=[
            tok(d),
            pl.BlockSpec((1, 1, 6, d), lambda i, j: (i, jnp.where(j >= n_lat_blocks, 1, 0), 0, 0)),
            pl.BlockSpec((d, PROJ_COLS), lambda i, j: (0, 0), pipeline_mode=pl.Buffered(1)),
            pl.BlockSpec((8, 512), lambda i, j: (0, 0)),
            pl.BlockSpec((1, 128), lambda i, j: (0, 0)),
        ],
        out_specs=(tok(512), tok(512), tok(512), tok(512), tok(512),
                   tok(256), tok(256), tok(512), tok(512), tok(128)),
        compiler_params=pltpu.CompilerParams(
            dimension_semantics=("arbitrary", "arbitrary"), vmem_limit_bytes=VMEM_LIMIT),
        name="inproj_features",
    )(xt, modall, w_bf, pr, gb)


def _hgrn2_constants():
    c = CHUNK
    msum = np.zeros((N_LEVELS + 2, c, c), np.float32)
    qsel = np.zeros((N_LEVELS, c), np.float32)
    amask = np.zeros((N_LEVELS + 1, c, c), np.float32)
    for l in range(N_LEVELS):
        s = c >> (l + 1)
        for i in range(c):
            m = (i // (2 * s)) * 2 * s + s
            if i >= m:
                qsel[l, i] = 1.0
                msum[l, i, m:i + 1] = 1.0
                amask[l, i, m - s:m] = 1.0
            else:
                msum[l, i, i + 1:m] = 1.0
    amask[N_LEVELS] = np.eye(c)
    msum[N_LEVELS] = np.tril(np.ones((c, c)))
    msum[N_LEVELS + 1] = np.triu(np.ones((c, c)), 1)
    flip = lambda a: a[..., ::-1, ::-1]
    msum = np.stack([msum, flip(msum)]).reshape(2, (N_LEVELS + 2) * c, c)
    amask = np.stack([amask, flip(amask)]).reshape(2, (N_LEVELS + 1) * c, c)
    qsel = np.stack([qsel, qsel[:, ::-1]])
    qsel = np.broadcast_to(qsel[..., None], (2, N_LEVELS, c, HEAD_DIM)).reshape(2, N_LEVELS * c, HEAD_DIM)
    return (jnp.asarray(msum, BF16), jnp.asarray(np.ascontiguousarray(qsel), F32), jnp.asarray(amask, F32))


def _hgrn2_kernel(qf_ref, qb_ref, vf_ref, vb_ref, lff_ref, lfb_ref, msum_ref, qsel_ref, amask_ref,
                  of_ref, ob_ref, st_ref):
    c = CHUNK

    @pl.when(pl.program_id(1) == 0)
    def _():
        st_ref[...] = jnp.zeros_like(st_ref)

    dirs = ((qf_ref, vf_ref, lff_ref, of_ref, c - 1), (qb_ref, vb_ref, lfb_ref, ob_ref, 0))
    for d, (q_ref, v_ref, lf_ref, o_ref, end_row) in enumerate(dirs):
        lf = lf_ref[0]
        pieces = jnp.concatenate(_split3(lf), axis=1)
        sums = _dot(msum_ref[d], pieces)
        w = HG_WIDTH
        decay = jnp.exp(sums[:, :w] + sums[:, w:2 * w] + sums[:, 2 * w:])
        key = 1.0 - jnp.exp(lf)
        g_row = N_LEVELS * c
        for h in range(HEADS):
            sl = slice(h * HEAD_DIM, (h + 1) * HEAD_DIM)
            q = q_ref[0, :, sl].astype(F32)
            k = key[:, sl]
            v = v_ref[0, :, sl]
            a = jnp.zeros((c, c), F32)
            for l in range(N_LEVELS):
                rows = slice(l * c, (l + 1) * c)
                t = (jnp.where(qsel_ref[d, rows, :] > 0.5, q, k) * decay[rows, sl]).astype(BF16)
                a = a + _dot_nt(t, t) * amask_ref[d, rows, :]
            a = a + _dot_nt(q.astype(BF16), k.astype(BF16)) * amask_ref[d, g_row:g_row + c, :]
            state = st_ref[d, h]
            q_in = (q * decay[g_row:g_row + c, sl]).astype(BF16)
            o = _dot(a.astype(BF16), v) + _dot_nt(q_in, state.astype(BF16))
            o_ref[0, :, sl] = o.astype(BF16)
            k_out = (k * decay[g_row + c:g_row + 2 * c, sl]).astype(BF16)
            total = decay[g_row + end_row:g_row + end_row + 1, sl]
            st_ref[d, h] = state * total + _dot_tn(v, k_out)


def _scan_chunk_maps(n_lat, n_ctx):
    n = n_lat + n_ctx

    def fwd(i, c):
        return (i, jnp.where(c < n_ctx, n_lat + c, c - n_ctx), 0)

    def bwd(i, c):
        return (i, jnp.where(c < n_ctx, n - 1 - c, n - 1 - c), 0)

    return fwd, bwd


def _hgrn2_scan(hq, hv, lff, lfb, consts):
    b, t, w = hq.shape
    n_ctx = CTX_LEN // CHUNK
    n_lat = t // CHUNK - n_ctx
    fwd, bwd = _scan_chunk_maps(n_lat, n_ctx)
    blk = lambda m: pl.BlockSpec((1, CHUNK, w), m)
    const = lambda a: pl.BlockSpec(a.shape, lambda i, c: (0, 0, 0))
    msum, qsel, amask = consts
    out = jax.ShapeDtypeStruct((b, t, w), BF16)
    return pl.pallas_call(
        _hgrn2_kernel,
        out_shape=(out, out),
        grid=(b, t // CHUNK),
        in_specs=[blk(fwd), blk(bwd), blk(fwd), blk(bwd), blk(fwd), blk(bwd),
                  const(msum), const(qsel), const(amask)],
        out_specs=(blk(fwd), blk(bwd)),
        scratch_shapes=[pltpu.VMEM((2, HEADS, HEAD_DIM, HEAD_DIM), F32)],
        compiler_params=pltpu.CompilerParams(
            dimension_semantics=("arbitrary", "arbitrary"), vmem_limit_bytes=VMEM_LIMIT),
        name="hgrn2_scan",
    )(hq, hq, hv, hv, lff, lfb, msum, qsel, amask)


def _mlstm_constants():
    c = CHUNK
    tri = np.tril(np.ones((c, c), np.float32))
    cum = np.stack([tri, tri.T])
    return jnp.asarray(cum, BF16), jnp.asarray(cum, F32)


def _mlstm_kernel(qf_ref, qb_ref, kf_ref, kb_ref, vf_ref, vb_ref, gf_ref, gb_ref, cum_ref, mask_ref,
                  hf_ref, hb_ref, cn_ref, m_ref):
    c = CHUNK

    @pl.when(pl.program_id(1) == 0)
    def _():
        cn_ref[...] = jnp.zeros_like(cn_ref)
        m_ref[...] = jnp.zeros_like(m_ref)

    ones = jnp.ones((c, HEAD_DIM), BF16)
    dirs = ((qf_ref, kf_ref, vf_ref, gf_ref, hf_ref, c - 1), (qb_ref, kb_ref, vb_ref, gb_ref, hb_ref, 0))
    for d, (q_ref, k_ref, v_ref, g_ref, h_ref, end_row) in enumerate(dirs):
        gates = g_ref[0]
        pieces = jnp.concatenate(_split3(gates), axis=1)
        sums = _dot(cum_ref[d], pieces)
        cums = sums[:, :128] + sums[:, 128:256] + sums[:, 256:]
        u = gates - pltpu.roll(cums, 128 - HEADS, axis=1)
        u_t = u.T
        causal = mask_ref[d] > 0.5
        for h in range(HEADS):
            cb = 8 * d + HEADS + h
            cu = 8 * d + h
            b_col = cums[:, cb:cb + 1]
            u_col = u[:, cu:cu + 1]
            u_row = u_t[cu:cu + 1, :]
            m_prev = m_ref[d, h][0:1, 0:1]
            dmat = jnp.where(causal, b_col + u_row, -1e30)
            inter = b_col + m_prev
            m_row = jnp.maximum(jnp.max(dmat, axis=1, keepdims=True), inter)
            q = q_ref[0, :, h * ML_DK:(h + 1) * ML_DK]
            k = k_ref[0, :, h * ML_DK:(h + 1) * ML_DK]
            v1 = jnp.concatenate([v_ref[0, :, h * HEAD_DIM:(h + 1) * HEAD_DIM], ones], axis=1)
            s = _dot_nt(q, k) * jnp.exp(dmat - m_row)
            w_inter = jnp.exp(inter - m_row)
            cn = cn_ref[d, h]
            res = _dot(s.astype(BF16), v1) + _dot((q.astype(F32) * w_inter).astype(BF16), cn.astype(BF16))
            num = res[:, :HEAD_DIM]
            den = res[:, HEAD_DIM:]
            h_ref[0, :, h * HEAD_DIM:(h + 1) * HEAD_DIM] = (
                num / jnp.maximum(jnp.abs(den), jnp.exp(-m_row))).astype(BF16)
            m_new = m_row[end_row:end_row + 1, :]
            b_end = b_col[end_row:end_row + 1, :]
            k_w = (k.astype(F32) * jnp.exp(u_col + b_end - m_new)).astype(BF16)
            carry = jnp.exp(b_end + m_prev - m_new)
            cn_ref[d, h] = carry * cn + _dot_tn(k_w, v1)
            m_ref[d, h] = jnp.broadcast_to(m_new, (8, 128))


def _mlstm_scan(mq, mk, mv, mg, consts):
    b, t, _ = mq.shape
    n_ctx = CTX_LEN // CHUNK
    n_lat = t // CHUNK - n_ctx
    fwd, bwd = _scan_chunk_maps(n_lat, n_ctx)
    blk = lambda width, m: pl.BlockSpec((1, CHUNK, width), m)
    const = lambda a: pl.BlockSpec(a.shape, lambda i, c: (0, 0, 0))
    cum, mask = consts
    out = jax.ShapeDtypeStruct((b, t, HEADS * HEAD_DIM), BF16)
    return pl.pallas_call(
        _mlstm_kernel,
        out_shape=(out, out),
        grid=(b, t // CHUNK),
        in_specs=[blk(256, fwd), blk(256, bwd), blk(256, fwd), blk(256, bwd),
                  blk(512, fwd), blk(512, bwd), blk(128, fwd), blk(128, bwd),
                  const(cum), const(mask)],
        out_specs=(blk(512, fwd), blk(512, bwd)),
        scratch_shapes=[pltpu.VMEM((2, HEADS, ML_DK, 2 * HEAD_DIM), F32),
                        pltpu.VMEM((2, HEADS, 8, 128), F32)],
        compiler_params=pltpu.CompilerParams(
            dimension_semantics=("arbitrary", "arbitrary"), vmem_limit_bytes=VMEM_LIMIT),
        name="mlstm_scan",
    )(mq, mq, mk, mk, mv, mv, mg, mg, cum, mask)


def _mixout_kernel(x_ref, hf_ref, hb_ref, mf_ref, mb_ref, hg_ref, mo_ref, nrm_ref, w_ref, mod_ref, ln_ref,
                   o_ref):
    x = x_ref[0]
    parts = []
    streams = ((hf_ref, hb_ref, hg_ref, 0), (mf_ref, mb_ref, mo_ref, 1))
    for f_ref, b_ref, gate_ref, row in streams:
        o = f_ref[0].astype(F32) + b_ref[0].astype(F32)
        gate = gate_ref[0].astype(F32)
        for h in range(HEADS):
            sl = slice(h * HEAD_DIM, (h + 1) * HEAD_DIM)
            oh = o[:, sl]
            ms = jnp.mean(oh * oh, axis=-1, keepdims=True)
            parts.append((oh * lax.rsqrt(ms + RMS_EPS) * nrm_ref[row:row + 1, sl] * gate[:, sl]).astype(BF16))
    y = jnp.concatenate(parts, axis=1)
    mix = _dot(y, w_ref[...])
    r = DEEPNORM_ALPHA * x + mod_ref[0, 0, 2:3, :] * mix
    o_ref[0] = _layer_norm(r, ln_ref[0:1, :], ln_ref[1:2, :])


def _mixout(xt, hg_f, hg_b, ml_f, ml_b, hg_gate, ml_gate, norms, w_bf, modall, ln, n_lat_blocks, n_blocks):
    b, t, d = xt.shape
    tok = lambda width: pl.BlockSpec((1, TOKEN_BLOCK, width), lambda i, j: (i, j, 0))
    return pl.pallas_call(
        _mixout_kernel,
        out_shape=jax.ShapeDtypeStruct((b, n_blocks * TOKEN_BLOCK, d), F32),
        grid=(b, n_blocks),
        in_specs=[
            tok(d), tok(512), tok(512), tok(512), tok(512), tok(512), tok(512),
            pl.BlockSpec((2, 512), lambda i, j: (0, 0)),
            pl.BlockSpec((d, d), lambda i, j: (0, 0), pipeline_mode=pl.Buffered(1)),
            pl.BlockSpec((1, 1, 6, d), lambda i, j: (i, jnp.where(j >= n_lat_blocks, 1, 0), 0, 0)),
            pl.BlockSpec((2, d), lambda i, j: (0, 0)),
        ],
        out_specs=tok(d),
        compiler_params=pltpu.CompilerParams(
            dimension_semantics=("arbitrary", "arbitrary"), vmem_limit_bytes=VMEM_LIMIT),
        name="mix_out_ln",
    )(xt, hg_f, hg_b, ml_f, ml_b, hg_gate, ml_gate, norms, w_bf, modall, ln)


FF_CHUNK = 256


def _ffn_kernel(x_ref, mod_ref, wgu_ref, wd_ref, ln_ref, o_ref, h_ref):
    x = x_ref[0]
    hx = (x * (1.0 + mod_ref[0, 0, 4:5, :]) + mod_ref[0, 0, 3:4, :]).astype(BF16)
    for c0 in range(0, D_FF, FF_CHUNK):
        g = _dot(hx, wgu_ref[:, c0:c0 + FF_CHUNK])
        u = _dot(hx, wgu_ref[:, D_FF + c0:D_FF + c0 + FF_CHUNK])
        h_ref[:, c0:c0 + FF_CHUNK] = (g * _sigmoid(g) * u).astype(BF16)
    f = _dot(h_ref[...], wd_ref[...])
    r = DEEPNORM_ALPHA * x + mod_ref[0, 0, 5:6, :] * f
    o_ref[0] = _layer_norm(r, ln_ref[0:1, :], ln_ref[1:2, :])


def _ffn(xt, modall, wgu_bf, wd_bf, ln, n_lat_blocks):
    b, t, d = xt.shape
    tok = pl.BlockSpec((1, TOKEN_BLOCK, d), lambda i, j: (i, j, 0))
    return pl.pallas_call(
        _ffn_kernel,
        out_shape=jax.ShapeDtypeStruct((b, t, d), F32),
        grid=(b, t // TOKEN_BLOCK),
        in_specs=[
            tok,
            pl.BlockSpec((1, 1, 6, d), lambda i, j: (i, jnp.where(j >= n_lat_blocks, 1, 0), 0, 0)),
            pl.BlockSpec((d, 2 * D_FF), lambda i, j: (0, 0), pipeline_mode=pl.Buffered(1)),
            pl.BlockSpec((D_FF, d), lambda i, j: (0, 0), pipeline_mode=pl.Buffered(1)),
            pl.BlockSpec((2, d), lambda i, j: (0, 0)),
        ],
        out_specs=tok,
        scratch_shapes=[pltpu.VMEM((TOKEN_BLOCK, D_FF), BF16)],
        compiler_params=pltpu.CompilerParams(
            dimension_semantics=("arbitrary", "arbitrary"), vmem_limit_bytes=VMEM_LIMIT),
        name="ffn_dense_ln",
    )(xt, modall, wgu_bf, wd_bf, ln)


EXPERT_CHUNKS = ((0, 256), (256, 256), (512, 256), (768, 256), (1024, 256), (1280, 128))


def _moe_kernel(x_ref, mod_ref, rw_ref, rb_ref, wgu_ref, wd_ref, ln_ref, o_ref,
                hx_ref, cw_ref, h_ref, acc_ref):
    e = pl.program_id(2)

    @pl.when(e == 0)
    def _():
        x = x_ref[0]
        hx = x * (1.0 + mod_ref[0, 0, 4:5, :]) + mod_ref[0, 0, 3:4, :]
        hx_ref[...] = hx.astype(BF16)
        logits = jnp.dot(hx, rw_ref[...], preferred_element_type=F32,
                         precision=lax.Precision.HIGHEST) + rb_ref[...]
        lane = lax.broadcasted_iota(jnp.int32, logits.shape, 1).astype(F32)
        logits = jnp.where(lane < N_EXPERTS, logits, -jnp.inf)
        top1 = jnp.max(logits, axis=1, keepdims=True)
        idx1 = jnp.min(jnp.where(logits == top1, lane, 128.0), axis=1, keepdims=True)
        rest = jnp.where(lane == idx1, -jnp.inf, logits)
        top2 = jnp.max(rest, axis=1, keepdims=True)
        idx2 = jnp.min(jnp.where(rest == top2, lane, 128.0), axis=1, keepdims=True)
        p1 = 1.0 / (1.0 + jnp.exp(top2 - top1))
        p2 = 1.0 - p1
        for ex in range(N_EXPERTS):
            w = jnp.where(idx1 == ex, p1, 0.0) + jnp.where(idx2 == ex, p2, 0.0)
            cw_ref[ex] = jnp.broadcast_to(w, (w.shape[0], 128))
        acc_ref[...] = jnp.zeros_like(acc_ref)

    hx = hx_ref[...]
    for c0, cw in EXPERT_CHUNKS:
        g = _dot(hx, wgu_ref[0, :, c0:c0 + cw])
        u = _dot(hx, wgu_ref[0, :, D_EXPERT + c0:D_EXPERT + c0 + cw])
        h_ref[:, c0:c0 + cw] = (g * _sigmoid(g) * u).astype(BF16)
    f = _dot(h_ref[...], wd_ref[0])
    weight = cw_ref[e]
    for n0 in range(0, D_MODEL, 128):
        acc_ref[:, n0:n0 + 128] += weight * f[:, n0:n0 + 128]

    @pl.when(e == N_EXPERTS - 1)
    def _():
        r = DEEPNORM_ALPHA * x_ref[0] + mod_ref[0, 0, 5:6, :] * acc_ref[...]
        o_ref[0] = _layer_norm(r, ln_ref[0:1, :], ln_ref[1:2, :])


def _moe(xt, modall, rw, rb, wgu_bf, wd_bf, ln, seq):
    b, _, d = xt.shape
    tm = MOE_TOKEN_BLOCK
    tok = pl.BlockSpec((1, tm, d), lambda i, j, e: (i, j, 0))
    return pl.pallas_call(
        _moe_kernel,
        out_shape=jax.ShapeDtypeStruct((b, seq, d), F32),
        grid=(b, seq // tm, N_EXPERTS),
        in_specs=[
            tok,
            pl.BlockSpec((1, 1, 6, d), lambda i, j, e: (i, 0, 0, 0)),
            pl.BlockSpec((d, 128), lambda i, j, e: (0, 0)),
            pl.BlockSpec((1, 128), lambda i, j, e: (0, 0)),
            pl.BlockSpec((1, d, 2 * D_EXPERT), lambda i, j, e: (e, 0, 0)),
            pl.BlockSpec((1, D_EXPERT, d), lambda i, j, e: (e, 0, 0)),
            pl.BlockSpec((2, d), lambda i, j, e: (0, 0)),
        ],
        out_specs=tok,
        scratch_shapes=[pltpu.VMEM((tm, d), BF16), pltpu.VMEM((N_EXPERTS, tm, 128), F32),
                        pltpu.VMEM((tm, D_EXPERT), BF16), pltpu.VMEM((tm, d), F32)],
        compiler_params=pltpu.CompilerParams(
            dimension_semantics=("arbitrary", "arbitrary", "arbitrary"), vmem_limit_bytes=VMEM_LIMIT),
        name="moe_experts_ln",
    )(xt, modall, rw, rb, wgu_bf, wd_bf, ln)


def _permute_w_in(w):
    gates = jnp.pad(w[:, 3584:3600], ((0, 0), (0, 112)))
    return jnp.concatenate([w[:, :3584], w[:, 3600:4112], gates], axis=1).astype(BF16)


def kernel(x, c, ctx, c_ctx, w_ada, b_ada, w_in, ml_conv_w, ml_conv_b, hg_lower_bound, ml_gate_bias,
           hg_norm, ml_norm, w_out, ln_g, ln_b, ffn_w_gate_up, ffn_w_down, router_w, router_b,
           moe_w_gate_up, moe_w_down):
    bsz, seq, d = x.shape
    depth = w_ada.shape[0]
    assert depth == DEPTH and d == D_MODEL and ctx.shape[1] == CTX_LEN and seq % MOE_TOKEN_BLOCK == 0
    n_lat_blocks = seq // TOKEN_BLOCK
    n_blocks = n_lat_blocks + CTX_LEN // TOKEN_BLOCK

    lb = jnp.cumsum(jax.nn.softmax(hg_lower_bound.astype(F32), axis=0), axis=0)
    lb = lb - lb[0]
    log_lb = jnp.log(lb)
    log_1m_lb = jnp.log1p(-lb)

    cond = jnp.concatenate([c, c_ctx[None, :], jnp.zeros((16 - bsz - 1, d), F32)], axis=0)
    mod = _modulation(cond, w_ada, b_ada)
    mod = mod.reshape(depth, 16, 6, d)

    hg_consts = _hgrn2_constants()
    ml_consts = _mlstm_constants()

    xt = jnp.concatenate([x, ctx], axis=1)
    out = None
    for l in range(depth):
        last = l == depth - 1
        mod_c = jnp.broadcast_to(mod[l, bsz][None], (bsz, 6, d))
        modall = jnp.stack([mod[l, :bsz], mod_c], axis=1)
        pr = jnp.concatenate([ml_conv_w[l], ml_conv_b[l][None], log_lb[l], log_1m_lb[l]], axis=0)
        gb = jnp.pad(ml_gate_bias[l].reshape(1, -1), ((0, 0), (0, 112)))
        feats = _inproj(xt, modall, _permute_w_in(w_in[l]), pr, gb, n_lat_blocks)
        hq, lff, lfb, hv, hg_gate, mq, mk, mv, ml_gate, mg = feats
        hg_f, hg_b = _hgrn2_scan(hq, hv, lff, lfb, hg_consts)
        ml_f, ml_b = _mlstm_scan(mq, mk, mv, mg, ml_consts)
        norms = jnp.stack([hg_norm[l].reshape(-1), ml_norm[l].reshape(-1)])
        xt = _mixout(xt, hg_f, hg_b, ml_f, ml_b, hg_gate, ml_gate, norms, w_out[l].astype(BF16), modall,
                     jnp.stack([ln_g[l, 0], ln_b[l, 0]]), n_lat_blocks,
                     n_lat_blocks if last else n_blocks)
        ln1 = jnp.stack([ln_g[l, 1], ln_b[l, 1]])
        jj = l // 2
        if l % 2 == 0:
            xt = _ffn(xt, modall, ffn_w_gate_up[jj].astype(BF16), ffn_w_down[jj].astype(BF16), ln1,
                      n_lat_blocks)
        else:
            rw = jnp.pad(router_w[jj], ((0, 0), (0, 128 - N_EXPERTS)))
            rb = jnp.pad(router_b[jj].reshape(1, -1), ((0, 0), (0, 128 - N_EXPERTS)))
            out = _moe(xt, modall, rw, rb, moe_w_gate_up[jj].astype(BF16), moe_w_down[jj].astype(BF16),
                       ln1, seq)
    return out
```

```python
import functools
import math

import numpy as np
import jax
import jax.numpy as jnp
from jax import lax
from jax.experimental import pallas as pl
from jax.experimental.pallas import tpu as pltpu

F32 = jnp.float32
BF16 = jnp.bfloat16

D_MODEL = 1024
CTX_LEN = 256
GRID_W = 64
HG_WIDTH = 512
HEADS = 4
HEAD_DIM = 128
ML_DK = 64
CHUNK = 64
D_FF = 2816
N_EXPERTS = 8
D_EXPERT = 1408
DEPTH = 2
DEEPNORM_ALPHA = (2 * DEPTH) ** 0.25
LN_EPS = 1e-5
RMS_EPS = 1e-6

TOKEN_BLOCK = 256
MOE_TOKEN_BLOCK = 512
SCAN_ROWS = 4
ML_CHUNK = 128
PROJ_COLS = 9 * 512 + 128
N_LEVELS = 6
VMEM_LIMIT = 56 * 1024 * 1024

NT_DIMS = (((1,), (1,)), ((), ()))
TN_DIMS = (((0,), (0,)), ((), ()))


def _dot(a, b):
    return jnp.dot(a, b, preferred_element_type=F32)


def _dot_nt(a, b):
    return lax.dot_general(a, b, NT_DIMS, preferred_element_type=F32)


def _dot_tn(a, b):
    return lax.dot_general(a, b, TN_DIMS, preferred_element_type=F32)


def _sigmoid(z):
    return 1.0 / (1.0 + jnp.exp(-z))


def _log_sigmoid(z):
    return jnp.minimum(z, 0.0) - jnp.log(1.0 + jnp.exp(-jnp.abs(z)))


def _split3(x):
    hi = x.astype(BF16)
    r = x - hi.astype(F32)
    mid = r.astype(BF16)
    lo = (r - mid.astype(F32)).astype(BF16)
    return hi, mid, lo


def _layer_norm(r, g, b):
    mu = jnp.mean(r, axis=-1, keepdims=True)
    d = r - mu
    var = jnp.mean(d * d, axis=-1, keepdims=True)
    return d * lax.rsqrt(var + LN_EPS) * g + b


def _mod_kernel(c_ref, w_ref, b_ref, o_ref):
    c = c_ref[...]
    s = c * _sigmoid(c)
    o_ref[0] = jnp.dot(s, w_ref[0], preferred_element_type=F32,
                       precision=lax.Precision.HIGHEST) + b_ref[0]


def _modulation(cond, w_ada, b_ada):
    depth, d, n = w_ada.shape
    rows = cond.shape[0]
    nb = 1536
    return pl.pallas_call(
        _mod_kernel,
        out_shape=jax.ShapeDtypeStruct((depth, rows, n), F32),
        grid=(depth, n // nb),
        in_specs=[
            pl.BlockSpec((rows, d), lambda l, j: (0, 0)),
            pl.BlockSpec((1, d, nb), lambda l, j: (l, 0, j)),
            pl.BlockSpec((1, 1, nb), lambda l, j: (l, 0, j)),
        ],
        out_specs=pl.BlockSpec((1, rows, nb), lambda l, j: (l, 0, j)),
        compiler_params=pltpu.CompilerParams(
            dimension_semantics=("arbitrary", "arbitrary"), vmem_limit_bytes=VMEM_LIMIT),
        name="adaln_mod",
    )(cond, w_ada, b_ada.reshape(depth, 1, n))


def _inproj_kernel(n_lat_blocks, x_ref, mod_ref, w_ref, lb_ref, conv_ref, gb_ref,
                   hq_o, lff_o, lfb_o, hv_o, hg_o, mq_o, mk_o, mv_o, mo_o, mg_o):
    j = pl.program_id(1)
    x = x_ref[0]
    hx = (x * (1.0 + mod_ref[0, 0, 1:2, :]) + mod_ref[0, 0, 0:1, :]).astype(BF16)

    def proj(group):
        return _dot(hx, w_ref[:, group * 512:(group + 1) * 512])

    z = proj(0)
    hq_o[0] = (z * _sigmoid(z)).astype(BF16)
    for r, out in ((0, lff_o), (1, lfb_o)):
        z = proj(1 + r)
        a = lb_ref[2 + r:3 + r, :] + _log_sigmoid(z)
        llb = lb_ref[r:r + 1, :]
        out[0] = jnp.maximum(llb, a) + jnp.log(1.0 + jnp.exp(-jnp.abs(llb - a)))
    hv_o[0] = proj(3).astype(BF16)
    z = proj(4)
    hg_o[0] = (z * _sigmoid(z)).astype(BF16)

    rows = x.shape[0]
    seg_mask = jnp.where(j >= n_lat_blocks, CTX_LEN - 1, GRID_W - 1)
    pos = lax.broadcasted_iota(jnp.int32, (rows, 1), 0) & seg_mask
    for group, out, scale in ((5, mq_o, ML_DK ** -0.5), (6, mk_o, 1.0)):
        u = proj(group)
        taps = conv_ref[:, (group - 5) * 512:(group - 4) * 512]
        u_prev = jnp.where(pos == 0, 0.0, pltpu.roll(u, 1, axis=0))
        u_next = jnp.where(pos == seg_mask, 0.0, pltpu.roll(u, rows - 1, axis=0))
        qk = taps[3:4, :] + u_prev * taps[0:1, :] + u * taps[1:2, :] + u_next * taps[2:3, :]
        out[0] = (qk * _sigmoid(qk) * scale).astype(BF16)
    mv_o[0] = proj(7).astype(BF16)
    mo_o[0] = _sigmoid(proj(8)).astype(BF16)

    g = _dot(hx, w_ref[:, 9 * 512:PROJ_COLS]) + gb_ref[...]
    col = lax.broadcasted_iota(jnp.int32, g.shape, 1)
    is_forget = jnp.logical_and((col & 4) != 0, col < 16)
    mg_o[0] = jnp.where(is_forget, _log_sigmoid(g), g)


def _inproj(xt, modall, w_bf, lbp, convp, gb, n_lat_blocks):
    b, t, d = xt.shape
    nblk = t // TOKEN_BLOCK
    tok = lambda width: pl.BlockSpec((1, TOKEN_BLOCK, width), lambda i, j: (i, j, 0))
    shp = lambda width, dt: jax.ShapeDtypeStruct((b, t, width), dt)
    return pl.pallas_call(
        functools.partial(_inproj_kernel, n_lat_blocks),
        out_shape=(shp(512, BF16), shp(512, F32), shp(512, F32), shp(512, BF16), shp(512, BF16),
                   shp(512, BF16), shp(512, BF16), shp(512, BF16), shp(512, BF16), shp(128, F32)),
        grid=(b, nblk),
        in_specs=[
            tok(d),
            pl.BlockSpec((1, 1, 6, d), lambda i, j: (i, jnp.where(j >= n_lat_blocks, 1, 0), 0, 0)),
            pl.BlockSpec((d, PROJ_COLS), lambda i, j: (0, 0), pipeline_mode=pl.Buffered(1)),
            pl.BlockSpec((4, 512), lambda i, j: (0, 0)),
            pl.BlockSpec((4, 1024), lambda i, j: (0, 0)),
            pl.BlockSpec((1, 128), lambda i, j: (0, 0)),
        ],
        out_specs=(tok(512), tok(512), tok(512), tok(512), tok(512),
                   tok(512), tok(512), tok(512), tok(512), tok(128)),
        compiler_params=pltpu.CompilerParams(
            dimension_semantics=("arbitrary", "arbitrary"), vmem_limit_bytes=VMEM_LIMIT),
        name="inproj_features",
    )(xt, modall, w_bf, lbp, convp, gb)


def _hgrn2_constants():
    c = CHUNK
    amask = np.zeros((N_LEVELS + 1, c, c), np.float32)
    side = np.zeros((3, c, 4 * HEAD_DIM), np.float32)
    for l in range(N_LEVELS):
        s = c >> (l + 1)
        for i in range(c):
            m = (i // (2 * s)) * 2 * s + s
            if i >= m:
                amask[l, i, m - s:m] = 1.0
            if l >= 3:
                side[l - 3, i, :] = 1.0 if i >= m else -1.0
    amask[N_LEVELS] = np.eye(c)
    tri = np.tril(np.ones((c, c), np.float32))
    cum = np.stack([tri, tri.T])
    amask = np.stack([amask, amask[..., ::-1, ::-1]]).reshape(2, (N_LEVELS + 1) * c, c)
    side = np.stack([side, side[:, ::-1]]).reshape(2, 3 * c, 4 * HEAD_DIM)
    return (jnp.asarray(cum, BF16), jnp.asarray(np.ascontiguousarray(side), F32),
            jnp.asarray(np.ascontiguousarray(amask), F32))


def _hgrn2_level_exponents(g_ref, lf, side_ref, d):
    c = CHUNK
    g = g_ref[...]
    width = g.shape[1]

    def split_row(r, n):
        return jnp.broadcast_to(g_ref[pl.ds(r, 1), :], (n, width))

    out = []
    for l in range(3):
        s = c >> (l + 1)
        parts = []
        for r0 in range(0, c, 2 * s):
            lo, hi = g[r0:r0 + s], g[r0 + s:r0 + 2 * s]
            if d == 0:
                gm = split_row(r0 + s - 1, s)
                parts += [gm - lo, hi - gm]
            else:
                gm = split_row(r0 + s, s)
                parts += [lo - gm, gm - hi]
        out.append(jnp.concatenate(parts, axis=0))
    sub = lax.broadcasted_iota(jnp.int32, (8, width), 0)
    for l, s in ((3, 4), (4, 2)):
        tiles = []
        for r0 in range(0, c, 8):
            if s == 4:
                gm = split_row(r0 + (3 if d == 0 else 4), 8)
            else:
                a, b = (1, 5) if d == 0 else (2, 6)
                gm = jnp.where(sub < 4, split_row(r0 + a, 8), split_row(r0 + b, 8))
            tiles.append(gm)
        gm = jnp.concatenate(tiles, axis=0)
        out.append((g - gm) * side_ref[d, (l - 3) * c:(l - 2) * c, :])
    out.append(lf * jnp.maximum(side_ref[d, 2 * c:3 * c, :], 0.0))
    return out


def _hgrn2_kernel(qf_ref, qb_ref, vf_ref, vb_ref, lff_ref, lfb_ref, cum_ref, side_ref, amask_ref,
                  of_ref, ob_ref, st_ref, g_ref):
    c = CHUNK

    @pl.when(pl.program_id(1) == 0)
    def _():
        st_ref[...] = jnp.zeros_like(st_ref)

    dirs = ((qf_ref, vf_ref, lff_ref, of_ref, c - 1), (qb_ref, vb_ref, lfb_ref, ob_ref, 0))
    streams = [(bi, d) + dirs[d] for bi in range(qf_ref.shape[0]) for d in range(2)]
    w = HG_WIDTH
    heads = [slice(h * HEAD_DIM, (h + 1) * HEAD_DIM) for h in range(HEADS)]

    for bi, d, _, _, lf_ref, _, _ in streams:
        sums = _dot(cum_ref[d], jnp.concatenate(_split3(lf_ref[bi]), axis=1))
        g_ref[bi, d] = sums[:, :w] + sums[:, w:2 * w] + sums[:, 2 * w:]

    feats = []
    for bi, d, q_ref, _, lf_ref, _, end_row in streams:
        lf = lf_ref[bi]
        gd_ref = g_ref.at[bi, d]
        g = gd_ref[...]
        g_end = jnp.broadcast_to(gd_ref[pl.ds(end_row, 1), :], (c, w))
        q = q_ref[bi].astype(F32)
        k = 1.0 - jnp.exp(lf)
        exps = _hgrn2_level_exponents(gd_ref, lf, side_ref, d)
        ts = []
        for l in range(N_LEVELS):
            s = c >> (l + 1)
            if l < 3:
                parts = []
                for r0 in range(0, c, 2 * s):
                    first, second = (k, q) if d == 0 else (q, k)
                    parts += [first[r0:r0 + s], second[r0 + s:r0 + 2 * s]]
                qk = jnp.concatenate(parts, axis=0)
            else:
                qk = jnp.where(side_ref[d, (l - 3) * c:(l - 2) * c, :] > 0.0, q, k)
            ts.append((qk * jnp.exp(exps[l])).astype(BF16))
        feats.append(dict(
            ts=ts, q_bf=q.astype(BF16), k_bf=k.astype(BF16),
            q_in=(q * jnp.exp(g)).astype(BF16),
            k_out=(k * jnp.exp(g_end - g)).astype(BF16),
            total=jnp.exp(gd_ref[pl.ds(end_row, 1), :])))

    diag_row = N_LEVELS * c
    prods = []
    for (bi, d, *_), f in zip(streams, feats):
        for sl in heads:
            p = [_dot_nt(t[:, sl], t[:, sl]) for t in f["ts"]]
            p.append(_dot_nt(f["q_bf"][:, sl], f["k_bf"][:, sl]))
            prods.append(p)

    intra = []
    for n, (bi, d, *_) in enumerate(streams):
        for h in range(HEADS):
            p = prods[n * HEADS + h]
            a = p[N_LEVELS] * amask_ref[d, diag_row:diag_row + c, :]
            for l in range(N_LEVELS):
                a = a + p[l] * amask_ref[d, l * c:(l + 1) * c, :]
            intra.append(a.astype(BF16))

    for n, ((bi, d, _, v_ref, _, o_ref, _), f) in enumerate(zip(streams, feats)):
        for h, sl in enumerate(heads):
            v = v_ref[bi, :, sl]
            state = st_ref[bi, d, h]
            o = _dot(intra[n * HEADS + h], v) + _dot_nt(f["q_in"][:, sl], state.astype(BF16))
            o_ref[bi, :, sl] = o.astype(BF16)
            st_ref[bi, d, h] = state * f["total"][:, sl] + _dot_tn(v, f["k_out"][:, sl])


def _scan_chunk_maps(n_lat, n_ctx):
    n = n_lat + n_ctx

    def fwd(i, c):
        return (i, jnp.where(c < n_ctx, n_lat + c, c - n_ctx), 0)

    def bwd(i, c):
        return (i, jnp.where(c < n_ctx, n - 1 - c, n - 1 - c), 0)

    return fwd, bwd


def _hgrn2_scan(hq, hv, lff, lfb, consts):
    b, t, w = hq.shape
    n_ctx = CTX_LEN // CHUNK
    n_lat = t // CHUNK - n_ctx
    fwd, bwd = _scan_chunk_maps(n_lat, n_ctx)
    nb = math.gcd(SCAN_ROWS, b)
    blk = lambda m: pl.BlockSpec((nb, CHUNK, w), m)
    const = lambda a: pl.BlockSpec(a.shape, lambda i, c: (0, 0, 0))
    cum, side, amask = consts
    out = jax.ShapeDtypeStruct((b, t, w), BF16)
    return pl.pallas_call(
        _hgrn2_kernel,
        out_shape=(out, out),
        grid=(b // nb, t // CHUNK),
        in_specs=[blk(fwd), blk(bwd), blk(fwd), blk(bwd), blk(fwd), blk(bwd),
                  const(cum), const(side), const(amask)],
        out_specs=(blk(fwd), blk(bwd)),
        scratch_shapes=[pltpu.VMEM((nb, 2, HEADS, HEAD_DIM, HEAD_DIM), F32),
                        pltpu.VMEM((nb, 2, CHUNK, w), F32)],
        compiler_params=pltpu.CompilerParams(
            dimension_semantics=("arbitrary", "arbitrary"), vmem_limit_bytes=VMEM_LIMIT),
        name="hgrn2_scan",
    )(hq, hq, hv, hv, lff, lfb, cum, side, amask)


def _mlstm_constants():
    c = ML_CHUNK
    tri = np.tril(np.ones((c, c), np.float32))
    cum = np.stack([tri, tri.T])
    negmask = np.where(cum > 0.5, 0.0, -1e30).astype(np.float32)
    sel = np.zeros((2, 3, 128, HEADS, HEAD_DIM), np.float32)
    for d in range(2):
        for h in range(HEADS):
            sel[d, :, 8 * d + h, h, :] = 1.0
    sel = sel.reshape(2, 3 * 128, HEADS * HEAD_DIM)
    return jnp.asarray(cum, BF16), jnp.asarray(negmask, F32), jnp.asarray(sel, BF16)


def _running_max(x, reverse):
    n, lanes = x.shape
    row = lax.broadcasted_iota(jnp.int32, (n, lanes), 0)
    shift = 1
    while shift < n:
        if shift % 8 == 0:
            pad = jnp.full((shift, lanes), -jnp.inf, x.dtype)
            shifted = (jnp.concatenate([x[shift:], pad], axis=0) if reverse
                       else jnp.concatenate([pad, x[:n - shift]], axis=0))
        elif reverse:
            shifted = jnp.where(row >= n - shift, -jnp.inf, pltpu.roll(x, n - shift, axis=0))
        else:
            shifted = jnp.where(row < shift, -jnp.inf, pltpu.roll(x, shift, axis=0))
        x = jnp.maximum(x, shifted)
        shift *= 2
    return x


def _mlstm_kernel(qf_ref, qb_ref, kf_ref, kb_ref, vf_ref, vb_ref, gf_ref, gb_ref,
                  cum_ref, neg_ref, sel_ref, hf_ref, hb_ref, cn_ref, m_ref):
    c = ML_CHUNK

    @pl.when(pl.program_id(1) == 0)
    def _():
        cn_ref[...] = jnp.zeros_like(cn_ref)
        m_ref[...] = jnp.zeros_like(m_ref)

    dirs = ((qf_ref, kf_ref, vf_ref, gf_ref, hf_ref, c - 1), (qb_ref, kb_ref, vb_ref, gb_ref, hb_ref, 0))
    streams = [(bi, d) + dirs[d] for bi in range(qf_ref.shape[0]) for d in range(2)]
    heads = [slice(h * HEAD_DIM, (h + 1) * HEAD_DIM) for h in range(HEADS)]
    lane = lax.broadcasted_iota(jnp.int32, (c, 128), 1)
    cat3 = lambda a: jnp.concatenate(_split3(a), axis=1)

    cums = []
    for bi, d, _, _, _, g_ref, _, _ in streams:
        sums = _dot(cum_ref[d], cat3(g_ref[bi]))
        cums.append(sums[:, :128] + sums[:, 128:256] + sums[:, 256:])

    small = []
    for (bi, d, _, _, _, g_ref, _, end_row), cs in zip(streams, cums):
        valid = jnp.logical_and(lane >= 8 * d, lane < 8 * d + HEADS)
        b_sh = jnp.where(valid, pltpu.roll(cs, 128 - HEADS, axis=1), 0.0)
        u = jnp.where(valid, g_ref[bi], 0.0) - b_sh
        mu = m_ref[bi, d][0:1, :]
        m_run = jnp.maximum(_running_max(u, d == 1), mu)
        m_end = m_run[end_row:end_row + 1, :]
        w_inter = jnp.exp(mu - m_run)
        m_ref[bi, d] = jnp.broadcast_to(b_sh[end_row:end_row + 1, :] + m_end, (8, 128))
        small.append(dict(
            m_run=m_run, w_inter=w_inter.astype(BF16), w_k=jnp.exp(u - m_end).astype(BF16),
            floor=jnp.exp(-(b_sh + m_run)).astype(BF16),
            carry=jnp.broadcast_to(w_inter[end_row:end_row + 1, :], (8, 128)), u_t=u.T))

    wide = []
    for (bi, d, *_), sm in zip(streams, small):
        sel3 = sel_ref[d]
        sel1 = sel_ref[d, 0:128, :]
        wide.append(dict(
            m_run=_dot(cat3(sm["m_run"]), sel3), w_inter=_dot(sm["w_inter"], sel1),
            w_k=_dot(sm["w_k"], sel1), floor=_dot(sm["floor"], sel1),
            carry=_dot(cat3(sm["carry"]), sel3)))

    qk = []
    for bi, d, q_ref, k_ref, *_ in streams:
        qk.append([_dot_nt(q_ref[bi, :, hb], k_ref[bi, :, hb]) for hb in heads])

    ones = jnp.ones((c, HEAD_DIM), BF16)
    zeros = jnp.zeros((HEAD_DIM - ML_DK, 2 * HEAD_DIM), BF16)
    operands = []
    for n, (bi, d, q_ref, _, v_ref, _, _, _) in enumerate(streams):
        for h, hb in enumerate(heads):
            u_row = jnp.broadcast_to(small[n]["u_t"][8 * d + h:8 * d + h + 1, :], (c, c))
            w = jnp.exp(u_row - wide[n]["m_run"][:, hb] + neg_ref[d])
            s = (qk[n][h] * w).astype(BF16)
            q_w = (q_ref[bi, :, hb].astype(F32) * wide[n]["w_inter"][:, hb]).astype(BF16)
            v1 = jnp.concatenate([v_ref[bi, :, hb], ones], axis=1)
            cn = cn_ref[bi, d, h]
            rhs = jnp.concatenate([v1, cn.astype(BF16), zeros], axis=0)
            operands.append((jnp.concatenate([s, q_w], axis=1), rhs, v1, cn))

    for n, (bi, d, _, k_ref, _, _, h_ref, _) in enumerate(streams):
        for h, hb in enumerate(heads):
            lhs, rhs, v1, cn = operands[n * HEADS + h]
            res = _dot(lhs, rhs)
            k_w = (k_ref[bi, :, hb].astype(F32) * wide[n]["w_k"][:, hb]).astype(BF16)
            upd = _dot_tn(k_w, v1)[:ML_DK]
            h_ref[bi, :, hb] = (res[:, :HEAD_DIM] / jnp.maximum(
                jnp.abs(res[:, HEAD_DIM:]), wide[n]["floor"][:, hb])).astype(BF16)
            carry = wide[n]["carry"][0:1, hb]
            cn_ref[bi, d, h] = cn * jnp.concatenate([carry, carry], axis=1) + upd


def _mlstm_scan(mq, mk, mv, mg, consts):
    b, t, w = mq.shape
    n_ctx = CTX_LEN // ML_CHUNK
    n_lat = t // ML_CHUNK - n_ctx
    fwd, bwd = _scan_chunk_maps(n_lat, n_ctx)
    nb = math.gcd(SCAN_ROWS, b)
    blk = lambda width, m: pl.BlockSpec((nb, ML_CHUNK, width), m)
    const = lambda a: pl.BlockSpec(a.shape, lambda i, c: (0, 0, 0))
    cum, negmask, sel = consts
    out = jax.ShapeDtypeStruct((b, t, w), BF16)
    return pl.pallas_call(
        _mlstm_kernel,
        out_shape=(out, out),
        grid=(b // nb, t // ML_CHUNK),
        in_specs=[blk(w, fwd), blk(w, bwd), blk(w, fwd), blk(w, bwd),
                  blk(w, fwd), blk(w, bwd), blk(128, fwd), blk(128, bwd),
                  const(cum), const(negmask), const(sel)],
        out_specs=(blk(w, fwd), blk(w, bwd)),
        scratch_shapes=[pltpu.VMEM((nb, 2, HEADS, ML_DK, 2 * HEAD_DIM), F32),
                        pltpu.VMEM((nb, 2, 8, 128), F32)],
        compiler_params=pltpu.CompilerParams(
            dimension_semantics=("arbitrary", "arbitrary"), vmem_limit_bytes=VMEM_LIMIT),
        name="mlstm_scan",
    )(mq, mq, mk, mk, mv, mv, mg, mg, cum, negmask, sel)


def _mixout_kernel(x_ref, hf_ref, hb_ref, mf_ref, mb_ref, hg_ref, mo_ref, nrm_ref, w_ref, mod_ref, ln_ref,
                   o_ref):
    x = x_ref[0]
    parts = []
    streams = ((hf_ref, hb_ref, hg_ref, 0), (mf_ref, mb_ref, mo_ref, 1))
    for f_ref, b_ref, gate_ref, row in streams:
        o = f_ref[0].astype(F32) + b_ref[0].astype(F32)
        gate = gate_ref[0].astype(F32)
        for h in range(HEADS):
            sl = slice(h * HEAD_DIM, (h + 1) * HEAD_DIM)
            oh = o[:, sl]
            ms = jnp.mean(oh * oh, axis=-1, keepdims=True)
            parts.append((oh * lax.rsqrt(ms + RMS_EPS) * nrm_ref[row:row + 1, sl] * gate[:, sl]).astype(BF16))
    y = jnp.concatenate(parts, axis=1)
    mix = _dot(y, w_ref[...])
    r = DEEPNORM_ALPHA * x + mod_ref[0, 0, 2:3, :] * mix
    o_ref[0] = _layer_norm(r, ln_ref[0:1, :], ln_ref[1:2, :])


def _mixout(xt, hg_f, hg_b, ml_f, ml_b, hg_gate, ml_gate, norms, w_bf, modall, ln, n_lat_blocks, n_blocks):
    b, t, d = xt.shape
    tok = lambda width: pl.BlockSpec((1, TOKEN_BLOCK, width), lambda i, j: (i, j, 0))
    return pl.pallas_call(
        _mixout_kernel,
        out_shape=jax.ShapeDtypeStruct((b, n_blocks * TOKEN_BLOCK, d), F32),
        grid=(b, n_blocks),
        in_specs=[
            tok(d), tok(512), tok(512), tok(512), tok(512), tok(512), tok(512),
            pl.BlockSpec((2, 512), lambda i, j: (0, 0)),
            pl.BlockSpec((d, d), lambda i, j: (0, 0), pipeline_mode=pl.Buffered(1)),
            pl.BlockSpec((1, 1, 6, d), lambda i, j: (i, jnp.where(j >= n_lat_blocks, 1, 0), 0, 0)),
            pl.BlockSpec((2, d), lambda i, j: (0, 0)),
        ],
        out_specs=tok(d),
        compiler_params=pltpu.CompilerParams(
            dimension_semantics=("arbitrary", "arbitrary"), vmem_limit_bytes=VMEM_LIMIT),
        name="mix_out_ln",
    )(xt, hg_f, hg_b, ml_f, ml_b, hg_gate, ml_gate, norms, w_bf, modall, ln)


FF_CHUNK = 256


def _ffn_kernel(x_ref, mod_ref, wgu_ref, wd_ref, ln_ref, o_ref, h_ref):
    x = x_ref[0]
    hx = (x * (1.0 + mod_ref[0, 0, 4:5, :]) + mod_ref[0, 0, 3:4, :]).astype(BF16)
    for c0 in range(0, D_FF, FF_CHUNK):
        g = _dot(hx, wgu_ref[:, c0:c0 + FF_CHUNK])
        u = _dot(hx, wgu_ref[:, D_FF + c0:D_FF + c0 + FF_CHUNK])
        h_ref[:, c0:c0 + FF_CHUNK] = (g * _sigmoid(g) * u).astype(BF16)
    f = _dot(h_ref[...], wd_ref[...])
    r = DEEPNORM_ALPHA * x + mod_ref[0, 0, 5:6, :] * f
    o_ref[0] = _layer_norm(r, ln_ref[0:1, :], ln_ref[1:2, :])


def _ffn(xt, modall, wgu_bf, wd_bf, ln, n_lat_blocks):
    b, t, d = xt.shape
    tok = pl.BlockSpec((1, TOKEN_BLOCK, d), lambda i, j: (i, j, 0))
    return pl.pallas_call(
        _ffn_kernel,
        out_shape=jax.ShapeDtypeStruct((b, t, d), F32),
        grid=(b, t // TOKEN_BLOCK),
        in_specs=[
            tok,
            pl.BlockSpec((1, 1, 6, d), lambda i, j: (i, jnp.where(j >= n_lat_blocks, 1, 0), 0, 0)),
            pl.BlockSpec((d, 2 * D_FF), lambda i, j: (0, 0), pipeline_mode=pl.Buffered(1)),
            pl.BlockSpec((D_FF, d), lambda i, j: (0, 0), pipeline_mode=pl.Buffered(1)),
            pl.BlockSpec((2, d), lambda i, j: (0, 0)),
        ],
        out_specs=tok,
        scratch_shapes=[pltpu.VMEM((TOKEN_BLOCK, D_FF), BF16)],
        compiler_params=pltpu.CompilerParams(
            dimension_semantics=("arbitrary", "arbitrary"), vmem_limit_bytes=VMEM_LIMIT),
        name="ffn_dense_ln",
    )(xt, modall, wgu_bf, wd_bf, ln)


EXPERT_CHUNKS = ((0, 256), (256, 256), (512, 256), (768, 256), (1024, 256), (1280, 128))


def _moe_kernel(x_ref, mod_ref, rw_ref, rb_ref, wgu_ref, wd_ref, ln_ref, o_ref,
                hx_ref, cw_ref, h_ref, acc_ref):
    e = pl.program_id(2)

    @pl.when(e == 0)
    def _():
        x = x_ref[0]
        hx = x * (1.0 + mod_ref[0, 0, 4:5, :]) + mod_ref[0, 0, 3:4, :]
        hx_ref[...] = hx.astype(BF16)
        logits = jnp.dot(hx, rw_ref[...], preferred_element_type=F32,
                         precision=lax.Precision.HIGHEST) + rb_ref[...]
        lane = lax.broadcasted_iota(jnp.int32, logits.shape, 1).astype(F32)
        logits = jnp.where(lane < N_EXPERTS, logits, -jnp.inf)
        top1 = jnp.max(logits, axis=1, keepdims=True)
        idx1 = jnp.min(jnp.where(logits == top1, lane, 128.0), axis=1, keepdims=True)
        rest = jnp.where(lane == idx1, -jnp.inf, logits)
        top2 = jnp.max(rest, axis=1, keepdims=True)
        idx2 = jnp.min(jnp.where(rest == top2, lane, 128.0), axis=1, keepdims=True)
        p1 = 1.0 / (1.0 + jnp.exp(top2 - top1))
        p2 = 1.0 - p1
        for ex in range(N_EXPERTS):
            w = jnp.where(idx1 == ex, p1, 0.0) + jnp.where(idx2 == ex, p2, 0.0)
            cw_ref[ex] = jnp.broadcast_to(w, (w.shape[0], 128))
        acc_ref[...] = jnp.zeros_like(acc_ref)

    hx = hx_ref[...]
    for c0, cw in EXPERT_CHUNKS:
        g = _dot(hx, wgu_ref[0, :, c0:c0 + cw])
        u = _dot(hx, wgu_ref[0, :, D_EXPERT + c0:D_EXPERT + c0 + cw])
        h_ref[:, c0:c0 + cw] = (g * _sigmoid(g) * u).astype(BF16)
    f = _dot(h_ref[...], wd_ref[0])
    weight = cw_ref[e]
    for n0 in range(0, D_MODEL, 128):
        acc_ref[:, n0:n0 + 128] += weight * f[:, n0:n0 + 128]

    @pl.when(e == N_EXPERTS - 1)
    def _():
        r = DEEPNORM_ALPHA * x_ref[0] + mod_ref[0, 0, 5:6, :] * acc_ref[...]
        o_ref[0] = _layer_norm(r, ln_ref[0:1, :], ln_ref[1:2, :])


def _moe(xt, modall, rw, rb, wgu_bf, wd_bf, ln, seq):
    b, _, d = xt.shape
    tm = MOE_TOKEN_BLOCK
    tok = pl.BlockSpec((1, tm, d), lambda i, j, e: (i, j, 0))
    return pl.pallas_call(
        _moe_kernel,
        out_shape=jax.ShapeDtypeStruct((b, seq, d), F32),
        grid=(b, seq // tm, N_EXPERTS),
        in_specs=[
            tok,
            pl.BlockSpec((1, 1, 6, d), lambda i, j, e: (i, 0, 0, 0)),
            pl.BlockSpec((d, 128), lambda i, j, e: (0, 0)),
            pl.BlockSpec((1, 128), lambda i, j, e: (0, 0)),
            pl.BlockSpec((1, d, 2 * D_EXPERT), lambda i, j, e: (e, 0, 0)),
            pl.BlockSpec((1, D_EXPERT, d), lambda i, j, e: (e, 0, 0)),
            pl.BlockSpec((2, d), lambda i, j, e: (0, 0)),
        ],
        out_specs=tok,
        scratch_shapes=[pltpu.VMEM((tm, d), BF16), pltpu.VMEM((N_EXPERTS, tm, 128), F32),
                        pltpu.VMEM((tm, D_EXPERT), BF16), pltpu.VMEM((tm, d), F32)],
        compiler_params=pltpu.CompilerParams(
            dimension_semantics=("arbitrary", "arbitrary", "arbitrary"), vmem_limit_bytes=VMEM_LIMIT),
        name="moe_experts_ln",
    )(xt, modall, rw, rb, wgu_bf, wd_bf, ln)


def _pad_heads(a):
    lead = a.shape[:-1]
    a = a.reshape(*lead, HEADS, ML_DK)
    a = jnp.pad(a, [(0, 0)] * len(lead) + [(0, 0), (0, HEAD_DIM - ML_DK)])
    return a.reshape(*lead, HEADS * HEAD_DIM)


def _permute_w_in(w):
    gates = jnp.pad(w[:, 3584:3600], ((0, 0), (0, 112)))
    cols = [w[:, :2560], _pad_heads(w[:, 2560:2816]), _pad_heads(w[:, 2816:3072]),
            w[:, 3072:3584], w[:, 3600:4112], gates]
    return jnp.concatenate(cols, axis=1).astype(BF16)


def kernel(x, c, ctx, c_ctx, w_ada, b_ada, w_in, ml_conv_w, ml_conv_b, hg_lower_bound, ml_gate_bias,
           hg_norm, ml_norm, w_out, ln_g, ln_b, ffn_w_gate_up, ffn_w_down, router_w, router_b,
           moe_w_gate_up, moe_w_down):
    bsz, seq, d = x.shape
    depth = w_ada.shape[0]
    assert depth == DEPTH and d == D_MODEL and ctx.shape[1] == CTX_LEN and seq % MOE_TOKEN_BLOCK == 0
    n_lat_blocks = seq // TOKEN_BLOCK
    n_blocks = n_lat_blocks + CTX_LEN // TOKEN_BLOCK

    lb = jnp.cumsum(jax.nn.softmax(hg_lower_bound.astype(F32), axis=0), axis=0)
    lb = lb - lb[0]
    log_lb = jnp.log(lb)
    log_1m_lb = jnp.log1p(-lb)

    cond = jnp.concatenate([c, c_ctx[None, :], jnp.zeros((16 - bsz - 1, d), F32)], axis=0)
    mod = _modulation(cond, w_ada, b_ada)
    mod = mod.reshape(depth, 16, 6, d)

    hg_consts = _hgrn2_constants()
    ml_consts = _mlstm_constants()

    xt = jnp.concatenate([x, ctx], axis=1)
    out = None
    for l in range(depth):
        last = l == depth - 1
        mod_c = jnp.broadcast_to(mod[l, bsz][None], (bsz, 6, d))
        modall = jnp.stack([mod[l, :bsz], mod_c], axis=1)
        lbp = jnp.concatenate([log_lb[l], log_1m_lb[l]], axis=0)
        taps = jnp.concatenate([ml_conv_w[l], ml_conv_b[l][None]], axis=0)
        convp = jnp.concatenate([_pad_heads(taps[:, :256]), _pad_heads(taps[:, 256:])], axis=1)
        gb = jnp.pad(ml_gate_bias[l].reshape(1, -1), ((0, 0), (0, 112)))
        feats = _inproj(xt, modall, _permute_w_in(w_in[l]), lbp, convp, gb, n_lat_blocks)
        hq, lff, lfb, hv, hg_gate, mq, mk, mv, ml_gate, mg = feats
        hg_f, hg_b = _hgrn2_scan(hq, hv, lff, lfb, hg_consts)
        ml_f, ml_b = _mlstm_scan(mq, mk, mv, mg, ml_consts)
        norms = jnp.stack([hg_norm[l].reshape(-1), ml_norm[l].reshape(-1)])
        xt = _mixout(xt, hg_f, hg_b, ml_f, ml_b, hg_gate, ml_gate, norms, w_out[l].astype(BF16), modall,
                     jnp.stack([ln_g[l, 0], ln_b[l, 0]]), n_lat_blocks,
                     n_lat_blocks if last else n_blocks)
        ln1 = jnp.stack([ln_g[l, 1], ln_b[l, 1]])
        jj = l // 2
        if l % 2 == 0:
            xt = _ffn(xt, modall, ffn_w_gate_up[jj].astype(BF16), ffn_w_down[jj].astype(BF16), ln1,
                      n_lat_blocks)
        else:
            rw = jnp.pad(router_w[jj], ((0, 0), (0, 128 - N_EXPERTS)))
            rb = jnp.pad(router_b[jj].reshape(1, -1), ((0, 0), (0, 128 - N_EXPERTS)))
            out = _moe(xt, modall, rw, rb, moe_w_gate_up[jj].astype(BF16), moe_w_down[jj].astype(BF16),
                       ln1, seq)
    return out
```

```python
import functools
import math

import numpy as np
import jax
import jax.numpy as jnp
from jax import lax
from jax.experimental import pallas as pl
from jax.experimental.pallas import tpu as pltpu
from jax.experimental.pallas import tpu_sc as plsc

F32 = jnp.float32
BF16 = jnp.bfloat16

D_MODEL = 1024
CTX_LEN = 256
GRID_W = 64
HG_WIDTH = 512
HEADS = 4
HEAD_DIM = 128
ML_DK = 64
CHUNK = 64
D_FF = 2816
N_EXPERTS = 8
D_EXPERT = 1408
DEPTH = 2
DEEPNORM_ALPHA = (2 * DEPTH) ** 0.25
LN_EPS = 1e-5
RMS_EPS = 1e-6

TOKEN_BLOCK = 256
EXPERT_TILE = 256
SCAN_ROWS = 4
ML_CHUNK = 128
PROJ_COLS = 9 * 512 + 128
N_LEVELS = 6
VMEM_LIMIT = 56 * 1024 * 1024

NT_DIMS = (((1,), (1,)), ((), ()))
TN_DIMS = (((0,), (0,)), ((), ()))


def _dot(a, b):
    return jnp.dot(a, b, preferred_element_type=F32)


def _dot_nt(a, b):
    return lax.dot_general(a, b, NT_DIMS, preferred_element_type=F32)


def _dot_tn(a, b):
    return lax.dot_general(a, b, TN_DIMS, preferred_element_type=F32)


def _sigmoid(z):
    return 1.0 / (1.0 + jnp.exp(-z))


def _log_sigmoid(z):
    return jnp.minimum(z, 0.0) - jnp.log(1.0 + jnp.exp(-jnp.abs(z)))


def _split3(x):
    hi = x.astype(BF16)
    r = x - hi.astype(F32)
    mid = r.astype(BF16)
    lo = (r - mid.astype(F32)).astype(BF16)
    return hi, mid, lo


def _layer_norm(r, g, b):
    mu = jnp.mean(r, axis=-1, keepdims=True)
    d = r - mu
    var = jnp.mean(d * d, axis=-1, keepdims=True)
    return d * lax.rsqrt(var + LN_EPS) * g + b


def _mod_kernel(c_ref, w_ref, b_ref, o_ref):
    c = c_ref[...]
    s = c * _sigmoid(c)
    o_ref[0] = jnp.dot(s, w_ref[0], preferred_element_type=F32,
                       precision=lax.Precision.HIGHEST) + b_ref[0]


def _modulation(cond, w_ada, b_ada):
    depth, d, n = w_ada.shape
    rows = cond.shape[0]
    nb = 1536
    return pl.pallas_call(
        _mod_kernel,
        out_shape=jax.ShapeDtypeStruct((depth, rows, n), F32),
        grid=(depth, n // nb),
        in_specs=[
            pl.BlockSpec((rows, d), lambda l, j: (0, 0)),
            pl.BlockSpec((1, d, nb), lambda l, j: (l, 0, j)),
            pl.BlockSpec((1, 1, nb), lambda l, j: (l, 0, j)),
        ],
        out_specs=pl.BlockSpec((1, rows, nb), lambda l, j: (l, 0, j)),
        compiler_params=pltpu.CompilerParams(
            dimension_semantics=("arbitrary", "arbitrary"), vmem_limit_bytes=VMEM_LIMIT),
        name="adaln_mod",
    )(cond, w_ada, b_ada.reshape(depth, 1, n))


def _inproj_kernel(n_lat_blocks, x_ref, mod_ref, w_ref, lb_ref, conv_ref, gb_ref,
                   hq_o, lff_o, lfb_o, hv_o, hg_o, mq_o, mk_o, mv_o, mo_o, mg_o):
    j = pl.program_id(1)
    x = x_ref[0]
    hx = (x * (1.0 + mod_ref[0, 0, 1:2, :]) + mod_ref[0, 0, 0:1, :]).astype(BF16)

    def proj(group):
        return _dot(hx, w_ref[:, group * 512:(group + 1) * 512])

    z = proj(0)
    hq_o[0] = (z * _sigmoid(z)).astype(BF16)
    for r, out in ((0, lff_o), (1, lfb_o)):
        z = proj(1 + r)
        a = lb_ref[2 + r:3 + r, :] + _log_sigmoid(z)
        llb = lb_ref[r:r + 1, :]
        out[0] = jnp.maximum(llb, a) + jnp.log(1.0 + jnp.exp(-jnp.abs(llb - a)))
    hv_o[0] = proj(3).astype(BF16)
    z = proj(4)
    hg_o[0] = (z * _sigmoid(z)).astype(BF16)

    rows = x.shape[0]
    seg_mask = jnp.where(j >= n_lat_blocks, CTX_LEN - 1, GRID_W - 1)
    pos = lax.broadcasted_iota(jnp.int32, (rows, 1), 0) & seg_mask
    for group, out, scale in ((5, mq_o, ML_DK ** -0.5), (6, mk_o, 1.0)):
        u = proj(group)
        taps = conv_ref[:, (group - 5) * 512:(group - 4) * 512]
        u_prev = jnp.where(pos == 0, 0.0, pltpu.roll(u, 1, axis=0))
        u_next = jnp.where(pos == seg_mask, 0.0, pltpu.roll(u, rows - 1, axis=0))
        qk = taps[3:4, :] + u_prev * taps[0:1, :] + u * taps[1:2, :] + u_next * taps[2:3, :]
        out[0] = (qk * _sigmoid(qk) * scale).astype(BF16)
    mv_o[0] = proj(7).astype(BF16)
    mo_o[0] = _sigmoid(proj(8)).astype(BF16)

    g = _dot(hx, w_ref[:, 9 * 512:PROJ_COLS]) + gb_ref[...]
    col = lax.broadcasted_iota(jnp.int32, g.shape, 1)
    is_forget = jnp.logical_and((col & 4) != 0, col < 16)
    mg_o[0] = jnp.where(is_forget, _log_sigmoid(g), g)


def _inproj(xt, modall, w_bf, lbp, convp, gb, n_lat_blocks):
    b, t, d = xt.shape
    nblk = t // TOKEN_BLOCK
    tok = lambda width: pl.BlockSpec((1, TOKEN_BLOCK, width), lambda i, j: (i, j, 0))
    shp = lambda width, dt: jax.ShapeDtypeStruct((b, t, width), dt)
    return pl.pallas_call(
        functools.partial(_inproj_kernel, n_lat_blocks),
        out_shape=(shp(512, BF16), shp(512, F32), shp(512, F32), shp(512, BF16), shp(512, BF16),
                   shp(512, BF16), shp(512, BF16), shp(512, BF16), shp(512, BF16), shp(128, F32)),
        grid=(b, nblk),
        in_specs=[
            tok(d),
            pl.BlockSpec((1, 1, 6, d), lambda i, j: (i, jnp.where(j >= n_lat_blocks, 1, 0), 0, 0)),
            pl.BlockSpec((d, PROJ_COLS), lambda i, j: (0, 0), pipeline_mode=pl.Buffered(1)),
            pl.BlockSpec((4, 512), lambda i, j: (0, 0)),
            pl.BlockSpec((4, 1024), lambda i, j: (0, 0)),
            pl.BlockSpec((1, 128), lambda i, j: (0, 0)),
        ],
        out_specs=(tok(512), tok(512), tok(512), tok(512), tok(512),
                   tok(512), tok(512), tok(512), tok(512), tok(128)),
        compiler_params=pltpu.CompilerParams(
            dimension_semantics=("arbitrary", "arbitrary"), vmem_limit_bytes=VMEM_LIMIT),
        name="inproj_features",
    )(xt, modall, w_bf, lbp, convp, gb)


def _hgrn2_constants():
    c = CHUNK
    amask = np.zeros((N_LEVELS + 1, c, c), np.float32)
    side = np.zeros((3, c, 4 * HEAD_DIM), np.float32)
    for l in range(N_LEVELS):
        s = c >> (l + 1)
        for i in range(c):
            m = (i // (2 * s)) * 2 * s + s
            if i >= m:
                amask[l, i, m - s:m] = 1.0
            if l >= 3:
                side[l - 3, i, :] = 1.0 if i >= m else -1.0
    amask[N_LEVELS] = np.eye(c)
    tri = np.tril(np.ones((c, c), np.float32))
    cum = np.stack([tri, tri.T])
    amask = np.stack([amask, amask[..., ::-1, ::-1]]).reshape(2, (N_LEVELS + 1) * c, c)
    side = np.stack([side, side[:, ::-1]]).reshape(2, 3 * c, 4 * HEAD_DIM)
    return (jnp.asarray(cum, BF16), jnp.asarray(np.ascontiguousarray(side), F32),
            jnp.asarray(np.ascontiguousarray(amask), F32))


def _hgrn2_level_exponents(g_ref, lf, side_ref, d):
    c = CHUNK
    g = g_ref[...]
    width = g.shape[1]

    def split_row(r, n):
        return jnp.broadcast_to(g_ref[pl.ds(r, 1), :], (n, width))

    out = []
    for l in range(3):
        s = c >> (l + 1)
        parts = []
        for r0 in range(0, c, 2 * s):
            lo, hi = g[r0:r0 + s], g[r0 + s:r0 + 2 * s]
            if d == 0:
                gm = split_row(r0 + s - 1, s)
                parts += [gm - lo, hi - gm]
            else:
                gm = split_row(r0 + s, s)
                parts += [lo - gm, gm - hi]
        out.append(jnp.concatenate(parts, axis=0))
    sub = lax.broadcasted_iota(jnp.int32, (8, width), 0)
    for l, s in ((3, 4), (4, 2)):
        tiles = []
        for r0 in range(0, c, 8):
            if s == 4:
                gm = split_row(r0 + (3 if d == 0 else 4), 8)
            else:
                a, b = (1, 5) if d == 0 else (2, 6)
                gm = jnp.where(sub < 4, split_row(r0 + a, 8), split_row(r0 + b, 8))
            tiles.append(gm)
        gm = jnp.concatenate(tiles, axis=0)
        out.append((g - gm) * side_ref[d, (l - 3) * c:(l - 2) * c, :])
    out.append(lf * jnp.maximum(side_ref[d, 2 * c:3 * c, :], 0.0))
    return out


def _hgrn2_kernel(qf_ref, qb_ref, vf_ref, vb_ref, lff_ref, lfb_ref, cum_ref, side_ref, amask_ref,
                  of_ref, ob_ref, st_ref, g_ref):
    c = CHUNK

    @pl.when(pl.program_id(1) == 0)
    def _():
        st_ref[...] = jnp.zeros_like(st_ref)

    dirs = ((qf_ref, vf_ref, lff_ref, of_ref, c - 1), (qb_ref, vb_ref, lfb_ref, ob_ref, 0))
    streams = [(bi, d) + dirs[d] for bi in range(qf_ref.shape[0]) for d in range(2)]
    w = HG_WIDTH
    heads = [slice(h * HEAD_DIM, (h + 1) * HEAD_DIM) for h in range(HEADS)]

    for bi, d, _, _, lf_ref, _, _ in streams:
        sums = _dot(cum_ref[d], jnp.concatenate(_split3(lf_ref[bi]), axis=1))
        g_ref[bi, d] = sums[:, :w] + sums[:, w:2 * w] + sums[:, 2 * w:]

    feats = []
    for bi, d, q_ref, _, lf_ref, _, end_row in streams:
        lf = lf_ref[bi]
        gd_ref = g_ref.at[bi, d]
        g = gd_ref[...]
        g_end = jnp.broadcast_to(gd_ref[pl.ds(end_row, 1), :], (c, w))
        q = q_ref[bi].astype(F32)
        k = 1.0 - jnp.exp(lf)
        exps = _hgrn2_level_exponents(gd_ref, lf, side_ref, d)
        ts = []
        for l in range(N_LEVELS):
            s = c >> (l + 1)
            if l < 3:
                parts = []
                for r0 in range(0, c, 2 * s):
                    first, second = (k, q) if d == 0 else (q, k)
                    parts += [first[r0:r0 + s], second[r0 + s:r0 + 2 * s]]
                qk = jnp.concatenate(parts, axis=0)
            else:
                qk = jnp.where(side_ref[d, (l - 3) * c:(l - 2) * c, :] > 0.0, q, k)
            ts.append((qk * jnp.exp(exps[l])).astype(BF16))
        feats.append(dict(
            ts=ts, q_bf=q.astype(BF16), k_bf=k.astype(BF16),
            q_in=(q * jnp.exp(g)).astype(BF16),
            k_out=(k * jnp.exp(g_end - g)).astype(BF16),
            total=jnp.exp(gd_ref[pl.ds(end_row, 1), :])))

    diag_row = N_LEVELS * c
    prods = []
    for (bi, d, *_), f in zip(streams, feats):
        for sl in heads:
            p = [_dot_nt(t[:, sl], t[:, sl]) for t in f["ts"]]
            p.append(_dot_nt(f["q_bf"][:, sl], f["k_bf"][:, sl]))
            prods.append(p)

    intra = []
    for n, (bi, d, *_) in enumerate(streams):
        for h in range(HEADS):
            p = prods[n * HEADS + h]
            a = p[N_LEVELS] * amask_ref[d, diag_row:diag_row + c, :]
            for l in range(N_LEVELS):
                a = a + p[l] * amask_ref[d, l * c:(l + 1) * c, :]
            intra.append(a.astype(BF16))

    for n, ((bi, d, _, v_ref, _, o_ref, _), f) in enumerate(zip(streams, feats)):
        for h, sl in enumerate(heads):
            v = v_ref[bi, :, sl]
            state = st_ref[bi, d, h]
            o = _dot(intra[n * HEADS + h], v) + _dot_nt(f["q_in"][:, sl], state.astype(BF16))
            o_ref[bi, :, sl] = o.astype(BF16)
            st_ref[bi, d, h] = state * f["total"][:, sl] + _dot_tn(v, f["k_out"][:, sl])


def _scan_chunk_maps(n_lat, n_ctx):
    n = n_lat + n_ctx

    def fwd(i, c):
        return (i, jnp.where(c < n_ctx, n_lat + c, c - n_ctx), 0)

    def bwd(i, c):
        return (i, jnp.where(c < n_ctx, n - 1 - c, n - 1 - c), 0)

    return fwd, bwd


def _hgrn2_scan(hq, hv, lff, lfb, consts):
    b, t, w = hq.shape
    n_ctx = CTX_LEN // CHUNK
    n_lat = t // CHUNK - n_ctx
    fwd, bwd = _scan_chunk_maps(n_lat, n_ctx)
    nb = math.gcd(SCAN_ROWS, b)
    blk = lambda m: pl.BlockSpec((nb, CHUNK, w), m)
    const = lambda a: pl.BlockSpec(a.shape, lambda i, c: (0, 0, 0))
    cum, side, amask = consts
    out = jax.ShapeDtypeStruct((b, t, w), BF16)
    return pl.pallas_call(
        _hgrn2_kernel,
        out_shape=(out, out),
        grid=(b // nb, t // CHUNK),
        in_specs=[blk(fwd), blk(bwd), blk(fwd), blk(bwd), blk(fwd), blk(bwd),
                  const(cum), const(side), const(amask)],
        out_specs=(blk(fwd), blk(bwd)),
        scratch_shapes=[pltpu.VMEM((nb, 2, HEADS, HEAD_DIM, HEAD_DIM), F32),
                        pltpu.VMEM((nb, 2, CHUNK, w), F32)],
        compiler_params=pltpu.CompilerParams(
            dimension_semantics=("arbitrary", "arbitrary"), vmem_limit_bytes=VMEM_LIMIT),
        name="hgrn2_scan",
    )(hq, hq, hv, hv, lff, lfb, cum, side, amask)


def _mlstm_constants():
    c = ML_CHUNK
    tri = np.tril(np.ones((c, c), np.float32))
    cum = np.stack([tri, tri.T])
    negmask = np.where(cum > 0.5, 0.0, -1e30).astype(np.float32)
    sel = np.zeros((2, 3, 128, HEADS, HEAD_DIM), np.float32)
    for d in range(2):
        for h in range(HEADS):
            sel[d, :, 8 * d + h, h, :] = 1.0
    sel = sel.reshape(2, 3 * 128, HEADS * HEAD_DIM)
    return jnp.asarray(cum, BF16), jnp.asarray(negmask, F32), jnp.asarray(sel, BF16)


def _running_max(x, reverse):
    n, lanes = x.shape
    row = lax.broadcasted_iota(jnp.int32, (n, lanes), 0)
    shift = 1
    while shift < n:
        if shift % 8 == 0:
            pad = jnp.full((shift, lanes), -jnp.inf, x.dtype)
            shifted = (jnp.concatenate([x[shift:], pad], axis=0) if reverse
                       else jnp.concatenate([pad, x[:n - shift]], axis=0))
        elif reverse:
            shifted = jnp.where(row >= n - shift, -jnp.inf, pltpu.roll(x, n - shift, axis=0))
        else:
            shifted = jnp.where(row < shift, -jnp.inf, pltpu.roll(x, shift, axis=0))
        x = jnp.maximum(x, shifted)
        shift *= 2
    return x


def _mlstm_kernel(qf_ref, qb_ref, kf_ref, kb_ref, vf_ref, vb_ref, gf_ref, gb_ref,
                  cum_ref, neg_ref, sel_ref, hf_ref, hb_ref, cn_ref, m_ref):
    c = ML_CHUNK

    @pl.when(pl.program_id(1) == 0)
    def _():
        cn_ref[...] = jnp.zeros_like(cn_ref)
        m_ref[...] = jnp.zeros_like(m_ref)

    dirs = ((qf_ref, kf_ref, vf_ref, gf_ref, hf_ref, c - 1), (qb_ref, kb_ref, vb_ref, gb_ref, hb_ref, 0))
    streams = [(bi, d) + dirs[d] for bi in range(qf_ref.shape[0]) for d in range(2)]
    heads = [slice(h * HEAD_DIM, (h + 1) * HEAD_DIM) for h in range(HEADS)]
    lane = lax.broadcasted_iota(jnp.int32, (c, 128), 1)
    cat3 = lambda a: jnp.concatenate(_split3(a), axis=1)

    cums = []
    for bi, d, _, _, _, g_ref, _, _ in streams:
        sums = _dot(cum_ref[d], cat3(g_ref[bi]))
        cums.append(sums[:, :128] + sums[:, 128:256] + sums[:, 256:])

    small = []
    for (bi, d, _, _, _, g_ref, _, end_row), cs in zip(streams, cums):
        valid = jnp.logical_and(lane >= 8 * d, lane < 8 * d + HEADS)
        b_sh = jnp.where(valid, pltpu.roll(cs, 128 - HEADS, axis=1), 0.0)
        u = jnp.where(valid, g_ref[bi], 0.0) - b_sh
        mu = m_ref[bi, d][0:1, :]
        m_run = jnp.maximum(_running_max(u, d == 1), mu)
        m_end = m_run[end_row:end_row + 1, :]
        w_inter = jnp.exp(mu - m_run)
        m_ref[bi, d] = jnp.broadcast_to(b_sh[end_row:end_row + 1, :] + m_end, (8, 128))
        small.append(dict(
            m_run=m_run, w_inter=w_inter.astype(BF16), w_k=jnp.exp(u - m_end).astype(BF16),
            floor=jnp.exp(-(b_sh + m_run)).astype(BF16),
            carry=jnp.broadcast_to(w_inter[end_row:end_row + 1, :], (8, 128)), u_t=u.T))

    wide = []
    for (bi, d, *_), sm in zip(streams, small):
        sel3 = sel_ref[d]
        sel1 = sel_ref[d, 0:128, :]
        wide.append(dict(
            m_run=_dot(cat3(sm["m_run"]), sel3), w_inter=_dot(sm["w_inter"], sel1),
            w_k=_dot(sm["w_k"], sel1), floor=_dot(sm["floor"], sel1),
            carry=_dot(cat3(sm["carry"]), sel3)))

    qk = []
    for bi, d, q_ref, k_ref, *_ in streams:
        qk.append([_dot_nt(q_ref[bi, :, hb], k_ref[bi, :, hb]) for hb in heads])

    ones = jnp.ones((c, HEAD_DIM), BF16)
    zeros = jnp.zeros((HEAD_DIM - ML_DK, 2 * HEAD_DIM), BF16)
    operands = []
    for n, (bi, d, q_ref, _, v_ref, _, _, _) in enumerate(streams):
        for h, hb in enumerate(heads):
            u_row = jnp.broadcast_to(small[n]["u_t"][8 * d + h:8 * d + h + 1, :], (c, c))
            w = jnp.exp(u_row - wide[n]["m_run"][:, hb] + neg_ref[d])
            s = (qk[n][h] * w).astype(BF16)
            q_w = (q_ref[bi, :, hb].astype(F32) * wide[n]["w_inter"][:, hb]).astype(BF16)
            v1 = jnp.concatenate([v_ref[bi, :, hb], ones], axis=1)
            cn = cn_ref[bi, d, h]
            rhs = jnp.concatenate([v1, cn.astype(BF16), zeros], axis=0)
            operands.append((jnp.concatenate([s, q_w], axis=1), rhs, v1, cn))

    for n, (bi, d, _, k_ref, _, _, h_ref, _) in enumerate(streams):
        for h, hb in enumerate(heads):
            lhs, rhs, v1, cn = operands[n * HEADS + h]
            res = _dot(lhs, rhs)
            k_w = (k_ref[bi, :, hb].astype(F32) * wide[n]["w_k"][:, hb]).astype(BF16)
            upd = _dot_tn(k_w, v1)[:ML_DK]
            h_ref[bi, :, hb] = (res[:, :HEAD_DIM] / jnp.maximum(
                jnp.abs(res[:, HEAD_DIM:]), wide[n]["floor"][:, hb])).astype(BF16)
            carry = wide[n]["carry"][0:1, hb]
            cn_ref[bi, d, h] = cn * jnp.concatenate([carry, carry], axis=1) + upd


def _mlstm_scan(mq, mk, mv, mg, consts):
    b, t, w = mq.shape
    n_ctx = CTX_LEN // ML_CHUNK
    n_lat = t // ML_CHUNK - n_ctx
    fwd, bwd = _scan_chunk_maps(n_lat, n_ctx)
    nb = math.gcd(SCAN_ROWS, b)
    blk = lambda width, m: pl.BlockSpec((nb, ML_CHUNK, width), m)
    const = lambda a: pl.BlockSpec(a.shape, lambda i, c: (0, 0, 0))
    cum, negmask, sel = consts
    out = jax.ShapeDtypeStruct((b, t, w), BF16)
    return pl.pallas_call(
        _mlstm_kernel,
        out_shape=(out, out),
        grid=(b // nb, t // ML_CHUNK),
        in_specs=[blk(w, fwd), blk(w, bwd), blk(w, fwd), blk(w, bwd),
                  blk(w, fwd), blk(w, bwd), blk(128, fwd), blk(128, bwd),
                  const(cum), const(negmask), const(sel)],
        out_specs=(blk(w, fwd), blk(w, bwd)),
        scratch_shapes=[pltpu.VMEM((nb, 2, HEADS, ML_DK, 2 * HEAD_DIM), F32),
                        pltpu.VMEM((nb, 2, 8, 128), F32)],
        compiler_params=pltpu.CompilerParams(
            dimension_semantics=("arbitrary", "arbitrary"), vmem_limit_bytes=VMEM_LIMIT),
        name="mlstm_scan",
    )(mq, mq, mk, mk, mv, mv, mg, mg, cum, negmask, sel)


def _mixout_kernel(x_ref, hf_ref, hb_ref, mf_ref, mb_ref, hg_ref, mo_ref, nrm_ref, w_ref, mod_ref, ln_ref,
                   o_ref):
    x = x_ref[0]
    parts = []
    streams = ((hf_ref, hb_ref, hg_ref, 0), (mf_ref, mb_ref, mo_ref, 1))
    for f_ref, b_ref, gate_ref, row in streams:
        o = f_ref[0].astype(F32) + b_ref[0].astype(F32)
        gate = gate_ref[0].astype(F32)
        for h in range(HEADS):
            sl = slice(h * HEAD_DIM, (h + 1) * HEAD_DIM)
            oh = o[:, sl]
            ms = jnp.mean(oh * oh, axis=-1, keepdims=True)
            parts.append((oh * lax.rsqrt(ms + RMS_EPS) * nrm_ref[row:row + 1, sl] * gate[:, sl]).astype(BF16))
    y = jnp.concatenate(parts, axis=1)
    mix = _dot(y, w_ref[...])
    r = DEEPNORM_ALPHA * x + mod_ref[0, 0, 2:3, :] * mix
    o_ref[0] = _layer_norm(r, ln_ref[0:1, :], ln_ref[1:2, :])


def _mixout(xt, hg_f, hg_b, ml_f, ml_b, hg_gate, ml_gate, norms, w_bf, modall, ln, n_lat_blocks, n_blocks):
    b, t, d = xt.shape
    tok = lambda width: pl.BlockSpec((1, TOKEN_BLOCK, width), lambda i, j: (i, j, 0))
    return pl.pallas_call(
        _mixout_kernel,
        out_shape=jax.ShapeDtypeStruct((b, n_blocks * TOKEN_BLOCK, d), F32),
        grid=(b, n_blocks),
        in_specs=[
            tok(d), tok(512), tok(512), tok(512), tok(512), tok(512), tok(512),
            pl.BlockSpec((2, 512), lambda i, j: (0, 0)),
            pl.BlockSpec((d, d), lambda i, j: (0, 0), pipeline_mode=pl.Buffered(1)),
            pl.BlockSpec((1, 1, 6, d), lambda i, j: (i, jnp.where(j >= n_lat_blocks, 1, 0), 0, 0)),
            pl.BlockSpec((2, d), lambda i, j: (0, 0)),
        ],
        out_specs=tok(d),
        compiler_params=pltpu.CompilerParams(
            dimension_semantics=("arbitrary", "arbitrary"), vmem_limit_bytes=VMEM_LIMIT),
        name="mix_out_ln",
    )(xt, hg_f, hg_b, ml_f, ml_b, hg_gate, ml_gate, norms, w_bf, modall, ln)


FF_CHUNK = 256


def _ffn_kernel(x_ref, mod_ref, wgu_ref, wd_ref, ln_ref, o_ref, h_ref):
    x = x_ref[0]
    hx = (x * (1.0 + mod_ref[0, 0, 4:5, :]) + mod_ref[0, 0, 3:4, :]).astype(BF16)
    for c0 in range(0, D_FF, FF_CHUNK):
        g = _dot(hx, wgu_ref[:, c0:c0 + FF_CHUNK])
        u = _dot(hx, wgu_ref[:, D_FF + c0:D_FF + c0 + FF_CHUNK])
        h_ref[:, c0:c0 + FF_CHUNK] = (g * _sigmoid(g) * u).astype(BF16)
    f = _dot(h_ref[...], wd_ref[...])
    r = DEEPNORM_ALPHA * x + mod_ref[0, 0, 5:6, :] * f
    o_ref[0] = _layer_norm(r, ln_ref[0:1, :], ln_ref[1:2, :])


def _ffn(xt, modall, wgu_bf, wd_bf, ln, n_lat_blocks):
    b, t, d = xt.shape
    tok = pl.BlockSpec((1, TOKEN_BLOCK, d), lambda i, j: (i, j, 0))
    return pl.pallas_call(
        _ffn_kernel,
        out_shape=jax.ShapeDtypeStruct((b, t, d), F32),
        grid=(b, t // TOKEN_BLOCK),
        in_specs=[
            tok,
            pl.BlockSpec((1, 1, 6, d), lambda i, j: (i, jnp.where(j >= n_lat_blocks, 1, 0), 0, 0)),
            pl.BlockSpec((d, 2 * D_FF), lambda i, j: (0, 0), pipeline_mode=pl.Buffered(1)),
            pl.BlockSpec((D_FF, d), lambda i, j: (0, 0), pipeline_mode=pl.Buffered(1)),
            pl.BlockSpec((2, d), lambda i, j: (0, 0)),
        ],
        out_specs=tok,
        scratch_shapes=[pltpu.VMEM((TOKEN_BLOCK, D_FF), BF16)],
        compiler_params=pltpu.CompilerParams(
            dimension_semantics=("arbitrary", "arbitrary"), vmem_limit_bytes=VMEM_LIMIT),
        name="ffn_dense_ln",
    )(xt, modall, wgu_bf, wd_bf, ln)


EXPERT_CHUNKS = ((0, 256), (256, 256), (512, 256), (768, 256), (1024, 256), (1280, 128))
ROUTE_LANES = dict(e1=0, e2=1, rank1=2, rank2=3, p1=4, p2=5)
PACK_W = 256
SC_WINDOW = 128


def _pack_pairs(a, b):
    ua = lax.bitcast_convert_type(a.astype(BF16).astype(F32), jnp.uint32)
    ub = lax.bitcast_convert_type(b.astype(BF16).astype(F32), jnp.uint32)
    return (ua >> 16) | ub


def _unpack_pairs(w):
    a = lax.bitcast_convert_type(w << 16, F32)
    b = lax.bitcast_convert_type(w & jnp.uint32(0xFFFF0000), F32)
    return a, b


def _route_kernel(x_ref, mod_ref, rw_ref, rb_ref, tri_ref, hxa_o, hxb_o, route_o, cnt_o, carry_ref):
    @pl.when(jnp.logical_and(pl.program_id(0) == 0, pl.program_id(1) == 0))
    def _():
        carry_ref[...] = jnp.zeros_like(carry_ref)

    x = x_ref[0]
    hx = x * (1.0 + mod_ref[0, 0, 4:5, :]) + mod_ref[0, 0, 3:4, :]
    hxa_o[...] = _pack_pairs(hx[:, 0:256], hx[:, 256:512])
    hxb_o[...] = _pack_pairs(hx[:, 512:768], hx[:, 768:1024])
    logits = jnp.dot(hx, rw_ref[...], preferred_element_type=F32,
                     precision=lax.Precision.HIGHEST) + rb_ref[...]
    lane = lax.broadcasted_iota(jnp.int32, logits.shape, 1).astype(F32)
    logits = jnp.where(lane < N_EXPERTS, logits, -jnp.inf)
    top1 = jnp.max(logits, axis=1, keepdims=True)
    idx1 = jnp.min(jnp.where(logits == top1, lane, 128.0), axis=1, keepdims=True)
    rest = jnp.where(lane == idx1, -jnp.inf, logits)
    top2 = jnp.max(rest, axis=1, keepdims=True)
    idx2 = jnp.min(jnp.where(rest == top2, lane, 128.0), axis=1, keepdims=True)
    p1 = 1.0 / (1.0 + jnp.exp(top2 - top1))
    p2 = 1.0 - p1
    oh1 = jnp.where(lane == idx1, 1.0, 0.0)
    oh2 = jnp.where(lane == idx2, 1.0, 0.0)
    both = oh1 + oh2
    before = _dot(tri_ref[...], both.astype(BF16)) + carry_ref[0:1, :]
    rank1 = jnp.sum(before * oh1, axis=1, keepdims=True)
    rank2 = jnp.sum(before * oh2, axis=1, keepdims=True)
    total = carry_ref[0:1, :] + jnp.sum(both, axis=0, keepdims=True)
    carry_ref[...] = jnp.broadcast_to(total, carry_ref.shape)
    cnt_o[...] = jnp.broadcast_to(total, cnt_o.shape)
    route = jnp.zeros_like(logits)
    for name, val in (("e1", idx1), ("e2", idx2), ("rank1", rank1), ("rank2", rank2), ("p1", p1), ("p2", p2)):
        route = jnp.where(lane == float(ROUTE_LANES[name]), val, route)
    route_o[...] = route


def _moe_route(xt, modall, rw, rb, seq):
    b, _, d = xt.shape
    tm = TOKEN_BLOCK
    nj = seq // tm
    tri = jnp.asarray(np.tril(np.ones((tm, tm), np.float32), -1), BF16)
    rows = lambda width: pl.BlockSpec((tm, width), lambda i, j: (i * nj + j, 0))
    n = b * seq
    return pl.pallas_call(
        _route_kernel,
        out_shape=(jax.ShapeDtypeStruct((n, PACK_W), jnp.uint32), jax.ShapeDtypeStruct((n, PACK_W), jnp.uint32),
                   jax.ShapeDtypeStruct((n, 128), F32), jax.ShapeDtypeStruct((8, 128), F32)),
        grid=(b, nj),
        in_specs=[
            pl.BlockSpec((1, tm, d), lambda i, j: (i, j, 0)),
            pl.BlockSpec((1, 1, 6, d), lambda i, j: (i, 0, 0, 0)),
            pl.BlockSpec((d, 128), lambda i, j: (0, 0)),
            pl.BlockSpec((1, 128), lambda i, j: (0, 0)),
            pl.BlockSpec((tm, tm), lambda i, j: (0, 0)),
        ],
        out_specs=(rows(PACK_W), rows(PACK_W), rows(128), pl.BlockSpec((8, 128), lambda i, j: (0, 0))),
        scratch_shapes=[pltpu.VMEM((8, 128), F32)],
        compiler_params=pltpu.CompilerParams(
            dimension_semantics=("arbitrary", "arbitrary"), vmem_limit_bytes=VMEM_LIMIT),
        name="moe_route",
    )(xt, modall, rw, rb, tri)


def _sc_mesh():
    return plsc.VectorSubcoreMesh(core_axis_name="core", subcore_axis_name="subcore")


def _sc_scatter_rows(rows, idx, n_out):
    n_src, width = rows.shape
    m = idx.shape[0]
    src_blocks = n_src // SC_WINDOW

    @functools.partial(pl.kernel, out_type=jax.ShapeDtypeStruct((n_out, width), rows.dtype), mesh=_sc_mesh())
    def scatter(x_hbm, i_hbm, o_hbm):
        def body(x_vmem, i_vmem):
            pltpu.sync_copy(x_vmem, o_hbm.at[i_vmem.at[0]])

        pltpu.emit_pipeline(
            body, grid=(m // SC_WINDOW,),
            in_specs=[pl.BlockSpec((SC_WINDOW, width), lambda i: (i % src_blocks, 0)),
                      pl.BlockSpec((1, SC_WINDOW), lambda i: (0, i))],
            out_specs=[],
            core_axis_name=("core", "subcore"), dimension_semantics=(pltpu.PARALLEL,),
        )(x_hbm, i_hbm)

    return scatter(rows, idx.reshape(1, m))


def _sc_gather_rows(table, idx):
    m = idx.shape[0]
    width = table.shape[1]

    @functools.partial(pl.kernel, out_type=jax.ShapeDtypeStruct((m, width), table.dtype), mesh=_sc_mesh())
    def gather(t_hbm, i_hbm, o_hbm):
        def body(i_vmem, o_vmem):
            pltpu.sync_copy(t_hbm.at[i_vmem.at[0]], o_vmem)

        pltpu.emit_pipeline(
            body, grid=(m // SC_WINDOW,),
            in_specs=[pl.BlockSpec((1, SC_WINDOW), lambda i: (0, i))],
            out_specs=[pl.BlockSpec((SC_WINDOW, width), lambda i: (i, 0))],
            core_axis_name=("core", "subcore"), dimension_semantics=(pltpu.PARALLEL,),
        )(i_hbm, o_hbm)

    return gather(table, idx.reshape(1, m))


def _experts_kernel(te_ref, used_ref, xa_ref, xb_ref, wgu_ref, wd_ref, ya_o, yb_o, h_ref):
    @pl.when(pl.program_id(0) < used_ref[0])
    def _():
        pieces = [p.astype(BF16) for w in (xa_ref[...], xb_ref[...]) for p in _unpack_pairs(w)]

        def proj(col, width):
            return sum(_dot(p, wgu_ref[0, q * PACK_W:(q + 1) * PACK_W, col:col + width])
                       for q, p in enumerate(pieces))

        for c0, cw in EXPERT_CHUNKS:
            g = proj(c0, cw)
            u = proj(D_EXPERT + c0, cw)
            h_ref[:, c0:c0 + cw] = (g * _sigmoid(g) * u).astype(BF16)
        f = _dot(h_ref[...], wd_ref[0])
        ya_o[...] = _pack_pairs(f[:, 0:256], f[:, 256:512])
        yb_o[...] = _pack_pairs(f[:, 512:768], f[:, 768:1024])


def _moe_experts(tile_expert, n_used, xa, xb, wgu_bf, wd_bf):
    r = xa.shape[0]
    tm = EXPERT_TILE
    rows = pl.BlockSpec((tm, PACK_W), lambda i, te, nu: (i, 0))
    out = jax.ShapeDtypeStruct((r, PACK_W), jnp.uint32)
    return pl.pallas_call(
        _experts_kernel,
        out_shape=(out, out),
        grid_spec=pltpu.PrefetchScalarGridSpec(
            num_scalar_prefetch=2,
            grid=(r // tm,),
            in_specs=[rows, rows,
                      pl.BlockSpec((1, D_MODEL, 2 * D_EXPERT), lambda i, te, nu: (te[i], 0, 0)),
                      pl.BlockSpec((1, D_EXPERT, D_MODEL), lambda i, te, nu: (te[i], 0, 0))],
            out_specs=(rows, rows),
            scratch_shapes=[pltpu.VMEM((tm, D_EXPERT), BF16)]),
        compiler_params=pltpu.CompilerParams(
            dimension_semantics=("arbitrary",), vmem_limit_bytes=VMEM_LIMIT),
        name="moe_experts",
    )(tile_expert, n_used, xa, xb, wgu_bf, wd_bf)


def _combine_kernel(x_ref, mod_ref, route_ref, y1a_ref, y1b_ref, y2a_ref, y2b_ref, ln_ref, o_ref):
    route = route_ref[...]
    f = None
    for ya_ref, yb_ref, name in ((y1a_ref, y1b_ref, "p1"), (y2a_ref, y2b_ref, "p2")):
        lane = ROUTE_LANES[name]
        y = jnp.concatenate(_unpack_pairs(ya_ref[...]) + _unpack_pairs(yb_ref[...]), axis=1)
        term = route[:, lane:lane + 1] * y
        f = term if f is None else f + term
    r = DEEPNORM_ALPHA * x_ref[0] + mod_ref[0, 0, 5:6, :] * f
    o_ref[0] = _layer_norm(r, ln_ref[0:1, :], ln_ref[1:2, :])


def _moe_combine(xt, modall, route, yga, ygb, ln, seq):
    b, _, d = xt.shape
    tm = TOKEN_BLOCK
    nj = seq // tm
    nblk = b * nj
    first = lambda width: pl.BlockSpec((tm, width), lambda i, j: (i * nj + j, 0))
    second = lambda width: pl.BlockSpec((tm, width), lambda i, j: (nblk + i * nj + j, 0))
    return pl.pallas_call(
        _combine_kernel,
        out_shape=jax.ShapeDtypeStruct((b, seq, d), F32),
        grid=(b, nj),
        in_specs=[
            pl.BlockSpec((1, tm, d), lambda i, j: (i, j, 0)),
            pl.BlockSpec((1, 1, 6, d), lambda i, j: (i, 0, 0, 0)),
            first(128), first(PACK_W), first(PACK_W), second(PACK_W), second(PACK_W),
            pl.BlockSpec((2, d), lambda i, j: (0, 0)),
        ],
        out_specs=pl.BlockSpec((1, tm, d), lambda i, j: (i, j, 0)),
        compiler_params=pltpu.CompilerParams(
            dimension_semantics=("arbitrary", "arbitrary"), vmem_limit_bytes=VMEM_LIMIT),
        name="moe_combine_ln",
    )(xt, modall, route, yga, ygb, yga, ygb, ln)


def _moe(xt, modall, rw, rb, wgu_bf, wd_bf, ln, seq):
    b = xt.shape[0]
    n = b * seq
    tm = EXPERT_TILE
    hxa, hxb, route, counts = _moe_route(xt, modall, rw, rb, seq)
    counts = counts[0, :N_EXPERTS].astype(jnp.int32)
    sizes = (counts + tm - 1) // tm * tm
    ends = jnp.cumsum(sizes)
    starts = ends - sizes
    col = lambda name: route[:, ROUTE_LANES[name]].astype(jnp.int32)
    pos = jnp.concatenate([jnp.take(starts, col("e1")) + col("rank1"),
                           jnp.take(starts, col("e2")) + col("rank2")])
    n_rows = 2 * n + N_EXPERTS * tm
    tile_start = jnp.arange(n_rows // tm, dtype=jnp.int32) * tm
    tile_expert = jnp.minimum(jnp.sum(tile_start[:, None] >= ends[None, :], axis=1), N_EXPERTS - 1)
    n_used = (ends[-1:] // tm).astype(jnp.int32)
    xa = _sc_scatter_rows(hxa, pos, n_rows)
    xb = _sc_scatter_rows(hxb, pos, n_rows)
    ya, yb = _moe_experts(tile_expert.astype(jnp.int32), n_used, xa, xb, wgu_bf, wd_bf)
    return _moe_combine(xt, modall, route, _sc_gather_rows(ya, pos), _sc_gather_rows(yb, pos), ln, seq)


def _pad_heads(a):
    lead = a.shape[:-1]
    a = a.reshape(*lead, HEADS, ML_DK)
    a = jnp.pad(a, [(0, 0)] * len(lead) + [(0, 0), (0, HEAD_DIM - ML_DK)])
    return a.reshape(*lead, HEADS * HEAD_DIM)


def _permute_w_in(w):
    gates = jnp.pad(w[:, 3584:3600], ((0, 0), (0, 112)))
    cols = [w[:, :2560], _pad_heads(w[:, 2560:2816]), _pad_heads(w[:, 2816:3072]),
            w[:, 3072:3584], w[:, 3600:4112], gates]
    return jnp.concatenate(cols, axis=1).astype(BF16)


def kernel(x, c, ctx, c_ctx, w_ada, b_ada, w_in, ml_conv_w, ml_conv_b, hg_lower_bound, ml_gate_bias,
           hg_norm, ml_norm, w_out, ln_g, ln_b, ffn_w_gate_up, ffn_w_down, router_w, router_b,
           moe_w_gate_up, moe_w_down):
    bsz, seq, d = x.shape
    depth = w_ada.shape[0]
    assert depth == DEPTH and d == D_MODEL and ctx.shape[1] == CTX_LEN and seq % TOKEN_BLOCK == 0
    n_lat_blocks = seq // TOKEN_BLOCK
    n_blocks = n_lat_blocks + CTX_LEN // TOKEN_BLOCK

    lb = jnp.cumsum(jax.nn.softmax(hg_lower_bound.astype(F32), axis=0), axis=0)
    lb = lb - lb[0]
    log_lb = jnp.log(lb)
    log_1m_lb = jnp.log1p(-lb)

    cond = jnp.concatenate([c, c_ctx[None, :], jnp.zeros((16 - bsz - 1, d), F32)], axis=0)
    mod = _modulation(cond, w_ada, b_ada)
    mod = mod.reshape(depth, 16, 6, d)

    hg_consts = _hgrn2_constants()
    ml_consts = _mlstm_constants()

    xt = jnp.concatenate([x, ctx], axis=1)
    out = None
    for l in range(depth):
        last = l == depth - 1
        mod_c = jnp.broadcast_to(mod[l, bsz][None], (bsz, 6, d))
        modall = jnp.stack([mod[l, :bsz], mod_c], axis=1)
        lbp = jnp.concatenate([log_lb[l], log_1m_lb[l]], axis=0)
        taps = jnp.concatenate([ml_conv_w[l], ml_conv_b[l][None]], axis=0)
        convp = jnp.concatenate([_pad_heads(taps[:, :256]), _pad_heads(taps[:, 256:])], axis=1)
        gb = jnp.pad(ml_gate_bias[l].reshape(1, -1), ((0, 0), (0, 112)))
        feats = _inproj(xt, modall, _permute_w_in(w_in[l]), lbp, convp, gb, n_lat_blocks)
        hq, lff, lfb, hv, hg_gate, mq, mk, mv, ml_gate, mg = feats
        hg_f, hg_b = _hgrn2_scan(hq, hv, lff, lfb, hg_consts)
        ml_f, ml_b = _mlstm_scan(mq, mk, mv, mg, ml_consts)
        norms = jnp.stack([hg_norm[l].reshape(-1), ml_norm[l].reshape(-1)])
        xt = _mixout(xt, hg_f, hg_b, ml_f, ml_b, hg_gate, ml_gate, norms, w_out[l].astype(BF16), modall,
                     jnp.stack([ln_g[l, 0], ln_b[l, 0]]), n_lat_blocks,
                     n_lat_blocks if last else n_blocks)
        ln1 = jnp.stack([ln_g[l, 1], ln_b[l, 1]])
        jj = l // 2
        if l % 2 == 0:
            xt = _ffn(xt, modall, ffn_w_gate_up[jj].astype(BF16), ffn_w_down[jj].astype(BF16), ln1,
                      n_lat_blocks)
        else:
            rw = jnp.pad(router_w[jj], ((0, 0), (0, 128 - N_EXPERTS)))
            rb = jnp.pad(router_b[jj].reshape(1, -1), ((0, 0), (0, 128 - N_EXPERTS)))
            out = _moe(xt, modall, rw, rb, moe_w_gate_up[jj].astype(BF16), moe_w_down[jj].astype(BF16),
                       ln1, seq)
    return out
```

```python
import functools
import math

import numpy as np
import jax
import jax.numpy as jnp
from jax import lax
from jax.experimental import pallas as pl
from jax.experimental.pallas import tpu as pltpu
from jax.experimental.pallas import tpu_sc as plsc

F32 = jnp.float32
BF16 = jnp.bfloat16

D_MODEL = 1024
CTX_LEN = 256
GRID_W = 64
HG_WIDTH = 512
HEADS = 4
HEAD_DIM = 128
ML_DK = 64
CHUNK = 64
D_FF = 2816
N_EXPERTS = 8
D_EXPERT = 1408
DEPTH = 2
DEEPNORM_ALPHA = (2 * DEPTH) ** 0.25
LOG2_E = 1.4426950408889634
LN_EPS = 1e-5
RMS_EPS = 1e-6

TOKEN_BLOCK = 256
EXPERT_TILE = 256
SCAN_ROWS = 4
ML_CHUNK = 128
PROJ_COLS = 9 * 512 + 128
N_LEVELS = 6
VMEM_LIMIT = 56 * 1024 * 1024

NT_DIMS = (((1,), (1,)), ((), ()))
TN_DIMS = (((0,), (0,)), ((), ()))


def _dot(a, b):
    return jnp.dot(a, b, preferred_element_type=F32)


def _dot_nt(a, b):
    return lax.dot_general(a, b, NT_DIMS, preferred_element_type=F32)


def _dot_tn(a, b):
    return lax.dot_general(a, b, TN_DIMS, preferred_element_type=F32)


def _sigmoid(z):
    return 0.5 * jnp.tanh(0.5 * z) + 0.5


def _silu(z):
    h = 0.5 * z
    return h + h * jnp.tanh(h)


def _log_sigmoid(z):
    return jnp.minimum(z, 0.0) - jnp.log(1.0 + jnp.exp(-jnp.abs(z)))


def _split3(x):
    hi = x.astype(BF16)
    r = x - hi.astype(F32)
    mid = r.astype(BF16)
    lo = (r - mid.astype(F32)).astype(BF16)
    return hi, mid, lo


def _layer_norm(r, g, b):
    mu = jnp.mean(r, axis=-1, keepdims=True)
    d = r - mu
    var = jnp.mean(d * d, axis=-1, keepdims=True)
    return d * lax.rsqrt(var + LN_EPS) * g + b


def _mod_kernel(c_ref, w_ref, b_ref, o_ref):
    c = c_ref[...]
    s = _silu(c)
    o_ref[0] = jnp.dot(s, w_ref[0], preferred_element_type=F32,
                       precision=lax.Precision.HIGHEST) + b_ref[0]


def _modulation(cond, w_ada, b_ada):
    depth, d, n = w_ada.shape
    rows = cond.shape[0]
    nb = 1536
    return pl.pallas_call(
        _mod_kernel,
        out_shape=jax.ShapeDtypeStruct((depth, rows, n), F32),
        grid=(depth, n // nb),
        in_specs=[
            pl.BlockSpec((rows, d), lambda l, j: (0, 0)),
            pl.BlockSpec((1, d, nb), lambda l, j: (l, 0, j)),
            pl.BlockSpec((1, 1, nb), lambda l, j: (l, 0, j)),
        ],
        out_specs=pl.BlockSpec((1, rows, nb), lambda l, j: (l, 0, j)),
        compiler_params=pltpu.CompilerParams(
            dimension_semantics=("arbitrary", "arbitrary"), vmem_limit_bytes=VMEM_LIMIT),
        name="adaln_mod",
    )(cond, w_ada, b_ada.reshape(depth, 1, n))


def _stream_specs(lat, ctx, ctx_block, n_lat_blocks):
    d = lat.shape[-1]
    return [pl.BlockSpec((1, TOKEN_BLOCK, d), lambda i, j: (i, jnp.minimum(j, n_lat_blocks - 1), 0)),
            pl.BlockSpec((1, TOKEN_BLOCK, d), lambda i, j: (i, ctx_block, 0))]


def _inproj_kernel(n_lat_blocks, x_ref, c_ref, mod_ref, w_ref, lb_ref, conv_ref, gb_ref,
                   hq_o, lff_o, lfb_o, hv_o, hg_o, mq_o, mk_o, mv_o, mo_o, mg_o):
    j = pl.program_id(1)
    x = jnp.where(j >= n_lat_blocks, c_ref[0], x_ref[0])
    hx = (x * (1.0 + mod_ref[0, 0, 1:2, :]) + mod_ref[0, 0, 0:1, :]).astype(BF16)

    rows = x.shape[0]
    seg_mask = jnp.where(j >= n_lat_blocks, CTX_LEN - 1, GRID_W - 1)
    pos = lax.broadcasted_iota(jnp.int32, (rows, 1), 0) & seg_mask

    def silu_to(out):
        def finish(z):
            out[0] = _silu(z).astype(BF16)
        return finish

    def log_forget_to(out, r):
        def finish(z):
            lb = lb_ref[r:r + 1, :]
            t = jnp.exp(-jnp.abs(z))
            num = jnp.where(z >= 0.0, 1.0 + lb * t, lb + t)
            out[0] = jnp.maximum(jnp.log2(num), jnp.minimum(z, 0.0) * LOG2_E) - jnp.log2(1.0 + t)
        return finish

    def cast_to(out):
        def finish(z):
            out[0] = z.astype(BF16)
        return finish

    def conv_silu_to(out, part, scale):
        def finish(u):
            taps = conv_ref[:, part * 512:(part + 1) * 512]
            u_prev = jnp.where(pos == 0, 0.0, pltpu.roll(u, 1, axis=0))
            u_next = jnp.where(pos == seg_mask, 0.0, pltpu.roll(u, rows - 1, axis=0))
            qk = taps[3:4, :] + u_prev * taps[0:1, :] + u * taps[1:2, :] + u_next * taps[2:3, :]
            out[0] = (_silu(qk) * scale).astype(BF16)
        return finish

    def sigmoid_to(out):
        def finish(z):
            out[0] = _sigmoid(z).astype(BF16)
        return finish

    def gates_to(out):
        def finish(z):
            g = z + gb_ref[...]
            col = lax.broadcasted_iota(jnp.int32, g.shape, 1)
            is_forget = jnp.logical_and((col & 4) != 0, col < 16)
            out[0] = jnp.where(is_forget, _log_sigmoid(g), g)
        return finish

    finishers = [silu_to(hq_o), log_forget_to(lff_o, 0), log_forget_to(lfb_o, 1), cast_to(hv_o),
                 silu_to(hg_o), conv_silu_to(mq_o, 0, ML_DK ** -0.5), conv_silu_to(mk_o, 1, 1.0),
                 cast_to(mv_o), sigmoid_to(mo_o), gates_to(mg_o)]
    bounds = [(g * 512, (g + 1) * 512) for g in range(9)] + [(9 * 512, PROJ_COLS)]
    z = _dot(hx, w_ref[:, bounds[0][0]:bounds[0][1]])
    for n, finish in enumerate(finishers):
        z_next = None
        if n + 1 < len(bounds):
            z_next = _dot(hx, w_ref[:, bounds[n + 1][0]:bounds[n + 1][1]])
        finish(z)
        z = z_next


def _inproj(lat, ctx, ctx_block, modall, w_bf, lbp, convp, gb, n_lat_blocks):
    b, _, d = lat.shape
    nblk = n_lat_blocks + CTX_LEN // TOKEN_BLOCK
    t = nblk * TOKEN_BLOCK
    tok = lambda width: pl.BlockSpec((1, TOKEN_BLOCK, width), lambda i, j: (i, j, 0))
    shp = lambda width, dt: jax.ShapeDtypeStruct((b, t, width), dt)
    return pl.pallas_call(
        functools.partial(_inproj_kernel, n_lat_blocks),
        out_shape=(shp(512, BF16), shp(512, F32), shp(512, F32), shp(512, BF16), shp(512, BF16),
                   shp(512, BF16), shp(512, BF16), shp(512, BF16), shp(512, BF16), shp(128, F32)),
        grid=(b, nblk),
        in_specs=_stream_specs(lat, ctx, ctx_block, n_lat_blocks) + [
            pl.BlockSpec((1, 1, 6, d), lambda i, j: (i, jnp.where(j >= n_lat_blocks, 1, 0), 0, 0)),
            pl.BlockSpec((d, PROJ_COLS), lambda i, j: (0, 0), pipeline_mode=pl.Buffered(1)),
            pl.BlockSpec((2, 512), lambda i, j: (0, 0)),
            pl.BlockSpec((4, 1024), lambda i, j: (0, 0)),
            pl.BlockSpec((1, 128), lambda i, j: (0, 0)),
        ],
        out_specs=(tok(512), tok(512), tok(512), tok(512), tok(512),
                   tok(512), tok(512), tok(512), tok(512), tok(128)),
        compiler_params=pltpu.CompilerParams(
            dimension_semantics=("arbitrary", "arbitrary"), vmem_limit_bytes=VMEM_LIMIT),
        name="inproj_features",
    )(lat, ctx, modall, w_bf, lbp, convp, gb)


def _hgrn2_constants():
    c = CHUNK
    amask = np.zeros((N_LEVELS + 1, c, c), np.float32)
    side = np.zeros((3, c, 4 * HEAD_DIM), np.float32)
    for l in range(N_LEVELS):
        s = c >> (l + 1)
        for i in range(c):
            m = (i // (2 * s)) * 2 * s + s
            if i >= m:
                amask[l, i, m - s:m] = 1.0
            if l >= 3:
                side[l - 3, i, :] = 1.0 if i >= m else -1.0
    amask[N_LEVELS] = np.eye(c)
    tri = np.tril(np.ones((c, c), np.float32))
    cum = np.stack([tri, tri.T])
    amask = np.stack([amask, amask[..., ::-1, ::-1]]).reshape(2, (N_LEVELS + 1) * c, c)
    side = np.stack([side, side[:, ::-1]]).reshape(2, 3 * c, 4 * HEAD_DIM)
    return (jnp.asarray(cum, BF16), jnp.asarray(np.ascontiguousarray(side), F32),
            jnp.asarray(np.ascontiguousarray(amask), F32))


def _hgrn2_level_exponents(g_ref, lf, side_ref, d):
    c = CHUNK
    g = g_ref[...]
    width = g.shape[1]

    def split_row(r, n):
        return jnp.broadcast_to(g_ref[pl.ds(r, 1), :], (n, width))

    out = []
    for l in range(3):
        s = c >> (l + 1)
        parts = []
        for r0 in range(0, c, 2 * s):
            lo, hi = g[r0:r0 + s], g[r0 + s:r0 + 2 * s]
            if d == 0:
                gm = split_row(r0 + s - 1, s)
                parts += [gm - lo, hi - gm]
            else:
                gm = split_row(r0 + s, s)
                parts += [lo - gm, gm - hi]
        out.append(jnp.concatenate(parts, axis=0))
    sub = lax.broadcasted_iota(jnp.int32, (8, width), 0)
    for l, s in ((3, 4), (4, 2)):
        tiles = []
        for r0 in range(0, c, 8):
            if s == 4:
                gm = split_row(r0 + (3 if d == 0 else 4), 8)
            else:
                a, b = (1, 5) if d == 0 else (2, 6)
                gm = jnp.where(sub < 4, split_row(r0 + a, 8), split_row(r0 + b, 8))
            tiles.append(gm)
        gm = jnp.concatenate(tiles, axis=0)
        out.append((g - gm) * side_ref[d, (l - 3) * c:(l - 2) * c, :])
    out.append(lf * jnp.maximum(side_ref[d, 2 * c:3 * c, :], 0.0))
    return out


def _hgrn2_kernel(qf_ref, qb_ref, vf_ref, vb_ref, lff_ref, lfb_ref, cum_ref, side_ref, amask_ref,
                  of_ref, ob_ref, st_ref, g_ref):
    c = CHUNK

    @pl.when(pl.program_id(1) == 0)
    def _():
        st_ref[...] = jnp.zeros_like(st_ref)

    dirs = ((qf_ref, vf_ref, lff_ref, of_ref, c - 1), (qb_ref, vb_ref, lfb_ref, ob_ref, 0))
    streams = [(bi, d) + dirs[d] for bi in range(qf_ref.shape[0]) for d in range(2)]
    w = HG_WIDTH
    heads = [slice(h * HEAD_DIM, (h + 1) * HEAD_DIM) for h in range(HEADS)]

    for bi, d, _, _, lf_ref, _, _ in streams:
        sums = _dot(cum_ref[d], jnp.concatenate(_split3(lf_ref[bi]), axis=1))
        g_ref[bi, d] = sums[:, :w] + sums[:, w:2 * w] + sums[:, 2 * w:]

    feats = []
    for bi, d, q_ref, _, lf_ref, _, end_row in streams:
        lf = lf_ref[bi]
        gd_ref = g_ref.at[bi, d]
        g = gd_ref[...]
        g_end = jnp.broadcast_to(gd_ref[pl.ds(end_row, 1), :], (c, w))
        q = q_ref[bi].astype(F32)
        k = 1.0 - jnp.exp2(lf)
        exps = _hgrn2_level_exponents(gd_ref, lf, side_ref, d)
        ts = []
        for l in range(N_LEVELS):
            s = c >> (l + 1)
            if l < 3:
                parts = []
                for r0 in range(0, c, 2 * s):
                    first, second = (k, q) if d == 0 else (q, k)
                    parts += [first[r0:r0 + s], second[r0 + s:r0 + 2 * s]]
                qk = jnp.concatenate(parts, axis=0)
            else:
                qk = jnp.where(side_ref[d, (l - 3) * c:(l - 2) * c, :] > 0.0, q, k)
            ts.append((qk * jnp.exp2(exps[l])).astype(BF16))
        feats.append(dict(
            ts=ts, q_bf=q.astype(BF16), k_bf=k.astype(BF16),
            q_in=(q * jnp.exp2(g)).astype(BF16),
            k_out=(k * jnp.exp2(g_end - g)).astype(BF16),
            total=jnp.exp2(gd_ref[pl.ds(end_row, 1), :])))

    diag_row = N_LEVELS * c
    prods = []
    for (bi, d, *_), f in zip(streams, feats):
        for sl in heads:
            p = [_dot_nt(t[:, sl], t[:, sl]) for t in f["ts"]]
            p.append(_dot_nt(f["q_bf"][:, sl], f["k_bf"][:, sl]))
            prods.append(p)

    intra = []
    for n, (bi, d, *_) in enumerate(streams):
        for h in range(HEADS):
            p = prods[n * HEADS + h]
            a = p[N_LEVELS] * amask_ref[d, diag_row:diag_row + c, :]
            for l in range(N_LEVELS):
                a = a + p[l] * amask_ref[d, l * c:(l + 1) * c, :]
            intra.append(a.astype(BF16))

    for n, ((bi, d, _, v_ref, _, o_ref, _), f) in enumerate(zip(streams, feats)):
        for h, sl in enumerate(heads):
            v = v_ref[bi, :, sl]
            state = st_ref[bi, d, h]
            o = _dot(intra[n * HEADS + h], v) + _dot_nt(f["q_in"][:, sl], state.astype(BF16))
            o_ref[bi, :, sl] = o.astype(BF16)
            st_ref[bi, d, h] = state * f["total"][:, sl] + _dot_tn(v, f["k_out"][:, sl])


def _scan_chunk_maps(n_lat, n_ctx):
    n = n_lat + n_ctx

    def fwd(i, c):
        return (i, jnp.where(c < n_ctx, n_lat + c, c - n_ctx), 0)

    def bwd(i, c):
        return (i, jnp.where(c < n_ctx, n - 1 - c, n - 1 - c), 0)

    return fwd, bwd


def _hgrn2_scan(hq, hv, lff, lfb, consts):
    b, t, w = hq.shape
    n_ctx = CTX_LEN // CHUNK
    n_lat = t // CHUNK - n_ctx
    fwd, bwd = _scan_chunk_maps(n_lat, n_ctx)
    nb = math.gcd(SCAN_ROWS, b)
    blk = lambda m: pl.BlockSpec((nb, CHUNK, w), m)
    const = lambda a: pl.BlockSpec(a.shape, lambda i, c: (0, 0, 0))
    cum, side, amask = consts
    out = jax.ShapeDtypeStruct((b, t, w), BF16)
    return pl.pallas_call(
        _hgrn2_kernel,
        out_shape=(out, out),
        grid=(b // nb, t // CHUNK),
        in_specs=[blk(fwd), blk(bwd), blk(fwd), blk(bwd), blk(fwd), blk(bwd),
                  const(cum), const(side), const(amask)],
        out_specs=(blk(fwd), blk(bwd)),
        scratch_shapes=[pltpu.VMEM((nb, 2, HEADS, HEAD_DIM, HEAD_DIM), F32),
                        pltpu.VMEM((nb, 2, CHUNK, w), F32)],
        compiler_params=pltpu.CompilerParams(
            dimension_semantics=("arbitrary", "arbitrary"), vmem_limit_bytes=VMEM_LIMIT),
        name="hgrn2_scan",
    )(hq, hq, hv, hv, lff, lfb, cum, side, amask)


def _mlstm_constants():
    c = ML_CHUNK
    tri = np.tril(np.ones((c, c), np.float32))
    cum = np.stack([tri, tri.T])
    negmask = np.where(cum > 0.5, 0.0, -1e30).astype(np.float32)
    sel = np.zeros((2, 3, 128, HEADS, HEAD_DIM), np.float32)
    for d in range(2):
        for h in range(HEADS):
            sel[d, :, 8 * d + h, h, :] = 1.0
    sel = sel.reshape(2, 3 * 128, HEADS * HEAD_DIM)
    return jnp.asarray(cum, BF16), jnp.asarray(negmask, F32), jnp.asarray(sel, BF16)


def _running_max(x, reverse):
    n, lanes = x.shape
    row = lax.broadcasted_iota(jnp.int32, (n, lanes), 0)
    shift = 1
    while shift < n:
        if shift % 8 == 0:
            pad = jnp.full((shift, lanes), -jnp.inf, x.dtype)
            shifted = (jnp.concatenate([x[shift:], pad], axis=0) if reverse
                       else jnp.concatenate([pad, x[:n - shift]], axis=0))
        elif reverse:
            shifted = jnp.where(row >= n - shift, -jnp.inf, pltpu.roll(x, n - shift, axis=0))
        else:
            shifted = jnp.where(row < shift, -jnp.inf, pltpu.roll(x, shift, axis=0))
        x = jnp.maximum(x, shifted)
        shift *= 2
    return x


def _mlstm_kernel(qf_ref, qb_ref, kf_ref, kb_ref, vf_ref, vb_ref, gf_ref, gb_ref,
                  cum_ref, neg_ref, sel_ref, hf_ref, hb_ref, cn_ref, m_ref):
    c = ML_CHUNK

    @pl.when(pl.program_id(1) == 0)
    def _():
        cn_ref[...] = jnp.zeros_like(cn_ref)
        m_ref[...] = jnp.zeros_like(m_ref)

    dirs = ((qf_ref, kf_ref, vf_ref, gf_ref, hf_ref, c - 1), (qb_ref, kb_ref, vb_ref, gb_ref, hb_ref, 0))
    streams = [(bi, d) + dirs[d] for bi in range(qf_ref.shape[0]) for d in range(2)]
    heads = [slice(h * HEAD_DIM, (h + 1) * HEAD_DIM) for h in range(HEADS)]
    lane = lax.broadcasted_iota(jnp.int32, (c, 128), 1)
    cat3 = lambda a: jnp.concatenate(_split3(a), axis=1)

    cums = []
    for bi, d, _, _, _, g_ref, _, _ in streams:
        sums = _dot(cum_ref[d], cat3(g_ref[bi]))
        cums.append(sums[:, :128] + sums[:, 128:256] + sums[:, 256:])

    small = []
    for (bi, d, _, _, _, g_ref, _, end_row), cs in zip(streams, cums):
        valid = jnp.logical_and(lane >= 8 * d, lane < 8 * d + HEADS)
        b_sh = jnp.where(valid, pltpu.roll(cs, 128 - HEADS, axis=1), 0.0)
        u = jnp.where(valid, g_ref[bi], 0.0) - b_sh
        mu = m_ref[bi, d][0:1, :]
        m_run = jnp.maximum(_running_max(u, d == 1), mu)
        m_end = m_run[end_row:end_row + 1, :]
        w_inter = jnp.exp(mu - m_run)
        m_ref[bi, d] = jnp.broadcast_to(b_sh[end_row:end_row + 1, :] + m_end, (8, 128))
        small.append(dict(
            m_run=m_run, w_inter=w_inter.astype(BF16), w_k=jnp.exp(u - m_end).astype(BF16),
            floor=jnp.exp(-(b_sh + m_run)).astype(BF16),
            carry=jnp.broadcast_to(w_inter[end_row:end_row + 1, :], (8, 128)), u_t=u.T))

    wide = []
    for (bi, d, *_), sm in zip(streams, small):
        sel3 = sel_ref[d]
        sel1 = sel_ref[d, 0:128, :]
        wide.append(dict(
            m_run=_dot(cat3(sm["m_run"]), sel3), w_inter=_dot(sm["w_inter"], sel1),
            w_k=_dot(sm["w_k"], sel1), floor=_dot(sm["floor"], sel1),
            carry=_dot(cat3(sm["carry"]), sel3)))

    qk = []
    for bi, d, q_ref, k_ref, *_ in streams:
        qk.append([_dot_nt(q_ref[bi, :, hb], k_ref[bi, :, hb]) for hb in heads])

    ones = jnp.ones((c, HEAD_DIM), BF16)
    zeros = jnp.zeros((HEAD_DIM - ML_DK, 2 * HEAD_DIM), BF16)
    operands = []
    for n, (bi, d, q_ref, _, v_ref, _, _, _) in enumerate(streams):
        for h, hb in enumerate(heads):
            u_row = jnp.broadcast_to(small[n]["u_t"][8 * d + h:8 * d + h + 1, :], (c, c))
            w = jnp.exp(u_row - wide[n]["m_run"][:, hb] + neg_ref[d])
            s = (qk[n][h] * w).astype(BF16)
            q_w = (q_ref[bi, :, hb].astype(F32) * wide[n]["w_inter"][:, hb]).astype(BF16)
            v1 = jnp.concatenate([v_ref[bi, :, hb], ones], axis=1)
            cn = cn_ref[bi, d, h]
            rhs = jnp.concatenate([v1, cn.astype(BF16), zeros], axis=0)
            operands.append((jnp.concatenate([s, q_w], axis=1), rhs, v1, cn))

    for n, (bi, d, _, k_ref, _, _, h_ref, _) in enumerate(streams):
        for h, hb in enumerate(heads):
            lhs, rhs, v1, cn = operands[n * HEADS + h]
            res = _dot(lhs, rhs)
            k_w = (k_ref[bi, :, hb].astype(F32) * wide[n]["w_k"][:, hb]).astype(BF16)
            upd = _dot_tn(k_w, v1)[:ML_DK]
            h_ref[bi, :, hb] = (res[:, :HEAD_DIM] / jnp.maximum(
                jnp.abs(res[:, HEAD_DIM:]), wide[n]["floor"][:, hb])).astype(BF16)
            carry = wide[n]["carry"][0:1, hb]
            cn_ref[bi, d, h] = cn * jnp.concatenate([carry, carry], axis=1) + upd


def _mlstm_scan(mq, mk, mv, mg, consts):
    b, t, w = mq.shape
    n_ctx = CTX_LEN // ML_CHUNK
    n_lat = t // ML_CHUNK - n_ctx
    fwd, bwd = _scan_chunk_maps(n_lat, n_ctx)
    nb = math.gcd(SCAN_ROWS, b)
    blk = lambda width, m: pl.BlockSpec((nb, ML_CHUNK, width), m)
    const = lambda a: pl.BlockSpec(a.shape, lambda i, c: (0, 0, 0))
    cum, negmask, sel = consts
    out = jax.ShapeDtypeStruct((b, t, w), BF16)
    return pl.pallas_call(
        _mlstm_kernel,
        out_shape=(out, out),
        grid=(b // nb, t // ML_CHUNK),
        in_specs=[blk(w, fwd), blk(w, bwd), blk(w, fwd), blk(w, bwd),
                  blk(w, fwd), blk(w, bwd), blk(128, fwd), blk(128, bwd),
                  const(cum), const(negmask), const(sel)],
        out_specs=(blk(w, fwd), blk(w, bwd)),
        scratch_shapes=[pltpu.VMEM((nb, 2, HEADS, ML_DK, 2 * HEAD_DIM), F32),
                        pltpu.VMEM((nb, 2, 8, 128), F32)],
        compiler_params=pltpu.CompilerParams(
            dimension_semantics=("arbitrary", "arbitrary"), vmem_limit_bytes=VMEM_LIMIT),
        name="mlstm_scan",
    )(mq, mq, mk, mk, mv, mv, mg, mg, cum, negmask, sel)


def _mixout_kernel(n_lat_blocks, x_ref, c_ref, hf_ref, hb_ref, mf_ref, mb_ref, hg_ref, mo_ref, nrm_ref,
                   w_ref, mod_ref, ln_ref, o_ref):
    x = jnp.where(pl.program_id(1) >= n_lat_blocks, c_ref[0], x_ref[0])
    parts = []
    streams = ((hf_ref, hb_ref, hg_ref, 0), (mf_ref, mb_ref, mo_ref, 1))
    for f_ref, b_ref, gate_ref, row in streams:
        o = f_ref[0].astype(F32) + b_ref[0].astype(F32)
        gate = gate_ref[0].astype(F32)
        for h in range(HEADS):
            sl = slice(h * HEAD_DIM, (h + 1) * HEAD_DIM)
            oh = o[:, sl]
            ms = jnp.mean(oh * oh, axis=-1, keepdims=True)
            parts.append((oh * lax.rsqrt(ms + RMS_EPS) * nrm_ref[row:row + 1, sl] * gate[:, sl]).astype(BF16))
    y = jnp.concatenate(parts, axis=1)
    mix = _dot(y, w_ref[...])
    r = DEEPNORM_ALPHA * x + mod_ref[0, 0, 2:3, :] * mix
    o_ref[0] = _layer_norm(r, ln_ref[0:1, :], ln_ref[1:2, :])


def _mixout(lat, ctx, ctx_block, hg_f, hg_b, ml_f, ml_b, hg_gate, ml_gate, norms, w_bf, modall, ln,
            n_lat_blocks, n_blocks):
    b, _, d = lat.shape
    tok = lambda width: pl.BlockSpec((1, TOKEN_BLOCK, width), lambda i, j: (i, j, 0))
    return pl.pallas_call(
        functools.partial(_mixout_kernel, n_lat_blocks),
        out_shape=jax.ShapeDtypeStruct((b, n_blocks * TOKEN_BLOCK, d), F32),
        grid=(b, n_blocks),
        in_specs=_stream_specs(lat, ctx, ctx_block, n_lat_blocks) + [
            tok(512), tok(512), tok(512), tok(512), tok(512), tok(512),
            pl.BlockSpec((2, 512), lambda i, j: (0, 0)),
            pl.BlockSpec((d, d), lambda i, j: (0, 0), pipeline_mode=pl.Buffered(1)),
            pl.BlockSpec((1, 1, 6, d), lambda i, j: (i, jnp.where(j >= n_lat_blocks, 1, 0), 0, 0)),
            pl.BlockSpec((2, d), lambda i, j: (0, 0)),
        ],
        out_specs=tok(d),
        compiler_params=pltpu.CompilerParams(
            dimension_semantics=("arbitrary", "arbitrary"), vmem_limit_bytes=VMEM_LIMIT),
        name="mix_out_ln",
    )(lat, ctx, hg_f, hg_b, ml_f, ml_b, hg_gate, ml_gate, norms, w_bf, modall, ln)


FF_CHUNK = 256


def _ffn_kernel(x_ref, mod_ref, wgu_ref, wd_ref, ln_ref, o_ref, h_ref):
    x = x_ref[0]
    hx = (x * (1.0 + mod_ref[0, 0, 4:5, :]) + mod_ref[0, 0, 3:4, :]).astype(BF16)
    for c0 in range(0, D_FF, FF_CHUNK):
        g = _dot(hx, wgu_ref[:, c0:c0 + FF_CHUNK])
        u = _dot(hx, wgu_ref[:, D_FF + c0:D_FF + c0 + FF_CHUNK])
        h_ref[:, c0:c0 + FF_CHUNK] = (_silu(g) * u).astype(BF16)
    f = _dot(h_ref[...], wd_ref[...])
    r = DEEPNORM_ALPHA * x + mod_ref[0, 0, 5:6, :] * f
    o_ref[0] = _layer_norm(r, ln_ref[0:1, :], ln_ref[1:2, :])


def _ffn(xt, modall, wgu_bf, wd_bf, ln, n_lat_blocks):
    b, t, d = xt.shape
    tok = pl.BlockSpec((1, TOKEN_BLOCK, d), lambda i, j: (i, j, 0))
    return pl.pallas_call(
        _ffn_kernel,
        out_shape=jax.ShapeDtypeStruct((b, t, d), F32),
        grid=(b, t // TOKEN_BLOCK),
        in_specs=[
            tok,
            pl.BlockSpec((1, 1, 6, d), lambda i, j: (i, jnp.where(j >= n_lat_blocks, 1, 0), 0, 0)),
            pl.BlockSpec((d, 2 * D_FF), lambda i, j: (0, 0), pipeline_mode=pl.Buffered(1)),
            pl.BlockSpec((D_FF, d), lambda i, j: (0, 0), pipeline_mode=pl.Buffered(1)),
            pl.BlockSpec((2, d), lambda i, j: (0, 0)),
        ],
        out_specs=tok,
        scratch_shapes=[pltpu.VMEM((TOKEN_BLOCK, D_FF), BF16)],
        compiler_params=pltpu.CompilerParams(
            dimension_semantics=("arbitrary", "arbitrary"), vmem_limit_bytes=VMEM_LIMIT),
        name="ffn_dense_ln",
    )(xt, modall, wgu_bf, wd_bf, ln)


EXPERT_CHUNKS = ((0, 256), (256, 256), (512, 256), (768, 256), (1024, 256), (1280, 128))
ROUTE_LANES = dict(e1=0, e2=1, rank1=2, rank2=3, p1=4, p2=5)
PACK_W = 256
SC_WINDOW = 128


def _pack_pairs(a, b):
    ua = lax.bitcast_convert_type(a.astype(BF16).astype(F32), jnp.uint32)
    ub = lax.bitcast_convert_type(b.astype(BF16).astype(F32), jnp.uint32)
    return (ua >> 16) | ub


def _unpack_pairs(w):
    a = lax.bitcast_convert_type(w << 16, F32)
    b = lax.bitcast_convert_type(w & jnp.uint32(0xFFFF0000), F32)
    return a, b


def _route_kernel(x_ref, mod_ref, rw_ref, rb_ref, tri_ref, hxa_o, hxb_o, route_o, cnt_o, carry_ref):
    @pl.when(jnp.logical_and(pl.program_id(0) == 0, pl.program_id(1) == 0))
    def _():
        carry_ref[...] = jnp.zeros_like(carry_ref)

    x = x_ref[0]
    hx = x * (1.0 + mod_ref[0, 0, 4:5, :]) + mod_ref[0, 0, 3:4, :]
    hxa_o[...] = _pack_pairs(hx[:, 0:256], hx[:, 256:512])
    hxb_o[...] = _pack_pairs(hx[:, 512:768], hx[:, 768:1024])
    logits = jnp.dot(hx, rw_ref[...], preferred_element_type=F32,
                     precision=lax.Precision.HIGHEST) + rb_ref[...]
    lane = lax.broadcasted_iota(jnp.int32, logits.shape, 1).astype(F32)
    logits = jnp.where(lane < N_EXPERTS, logits, -jnp.inf)
    top1 = jnp.max(logits, axis=1, keepdims=True)
    idx1 = jnp.min(jnp.where(logits == top1, lane, 128.0), axis=1, keepdims=True)
    rest = jnp.where(lane == idx1, -jnp.inf, logits)
    top2 = jnp.max(rest, axis=1, keepdims=True)
    idx2 = jnp.min(jnp.where(rest == top2, lane, 128.0), axis=1, keepdims=True)
    p1 = 1.0 / (1.0 + jnp.exp(top2 - top1))
    p2 = 1.0 - p1
    oh1 = jnp.where(lane == idx1, 1.0, 0.0)
    oh2 = jnp.where(lane == idx2, 1.0, 0.0)
    both = oh1 + oh2
    before = _dot(tri_ref[...], both.astype(BF16)) + carry_ref[0:1, :]
    rank1 = jnp.sum(before * oh1, axis=1, keepdims=True)
    rank2 = jnp.sum(before * oh2, axis=1, keepdims=True)
    total = carry_ref[0:1, :] + jnp.sum(both, axis=0, keepdims=True)
    carry_ref[...] = jnp.broadcast_to(total, carry_ref.shape)
    cnt_o[...] = jnp.broadcast_to(total, cnt_o.shape)
    route = jnp.zeros_like(logits)
    for name, val in (("e1", idx1), ("e2", idx2), ("rank1", rank1), ("rank2", rank2), ("p1", p1), ("p2", p2)):
        route = jnp.where(lane == float(ROUTE_LANES[name]), val, route)
    route_o[...] = route


def _moe_route(xt, modall, rw, rb, seq):
    b, _, d = xt.shape
    tm = TOKEN_BLOCK
    nj = seq // tm
    tri = jnp.asarray(np.tril(np.ones((tm, tm), np.float32), -1), BF16)
    rows = lambda width: pl.BlockSpec((tm, width), lambda i, j: (i * nj + j, 0))
    n = b * seq
    return pl.pallas_call(
        _route_kernel,
        out_shape=(jax.ShapeDtypeStruct((n, PACK_W), jnp.uint32), jax.ShapeDtypeStruct((n, PACK_W), jnp.uint32),
                   jax.ShapeDtypeStruct((n, 128), F32), jax.ShapeDtypeStruct((8, 128), F32)),
        grid=(b, nj),
        in_specs=[
            pl.BlockSpec((1, tm, d), lambda i, j: (i, j, 0)),
            pl.BlockSpec((1, 1, 6, d), lambda i, j: (i, 0, 0, 0)),
            pl.BlockSpec((d, 128), lambda i, j: (0, 0)),
            pl.BlockSpec((1, 128), lambda i, j: (0, 0)),
            pl.BlockSpec((tm, tm), lambda i, j: (0, 0)),
        ],
        out_specs=(rows(PACK_W), rows(PACK_W), rows(128), pl.BlockSpec((8, 128), lambda i, j: (0, 0))),
        scratch_shapes=[pltpu.VMEM((8, 128), F32)],
        compiler_params=pltpu.CompilerParams(
            dimension_semantics=("arbitrary", "arbitrary"), vmem_limit_bytes=VMEM_LIMIT),
        name="moe_route",
    )(xt, modall, rw, rb, tri)


def _sc_mesh():
    return plsc.VectorSubcoreMesh(core_axis_name="core", subcore_axis_name="subcore")


def _sc_scatter_rows(rows, idx, n_out):
    n_src, width = rows.shape
    m = idx.shape[0]
    src_blocks = n_src // SC_WINDOW

    @functools.partial(pl.kernel, out_type=jax.ShapeDtypeStruct((n_out, width), rows.dtype), mesh=_sc_mesh())
    def scatter(x_hbm, i_hbm, o_hbm):
        def body(x_vmem, i_vmem):
            pltpu.sync_copy(x_vmem, o_hbm.at[i_vmem.at[0]])

        pltpu.emit_pipeline(
            body, grid=(m // SC_WINDOW,),
            in_specs=[pl.BlockSpec((SC_WINDOW, width), lambda i: (i % src_blocks, 0)),
                      pl.BlockSpec((1, SC_WINDOW), lambda i: (0, i))],
            out_specs=[],
            core_axis_name=("core", "subcore"), dimension_semantics=(pltpu.PARALLEL,),
        )(x_hbm, i_hbm)

    return scatter(rows, idx.reshape(1, m))


def _sc_gather_rows(table, idx):
    m = idx.shape[0]
    width = table.shape[1]

    @functools.partial(pl.kernel, out_type=jax.ShapeDtypeStruct((m, width), table.dtype), mesh=_sc_mesh())
    def gather(t_hbm, i_hbm, o_hbm):
        def body(i_vmem, o_vmem):
            pltpu.sync_copy(t_hbm.at[i_vmem.at[0]], o_vmem)

        pltpu.emit_pipeline(
            body, grid=(m // SC_WINDOW,),
            in_specs=[pl.BlockSpec((1, SC_WINDOW), lambda i: (0, i))],
            out_specs=[pl.BlockSpec((SC_WINDOW, width), lambda i: (i, 0))],
            core_axis_name=("core", "subcore"), dimension_semantics=(pltpu.PARALLEL,),
        )(i_hbm, o_hbm)

    return gather(table, idx.reshape(1, m))


CAST_ROWS = 64


def _experts_kernel(te_ref, used_ref, xa_ref, xb_ref, wgu_ref, wd_ref, ya_o, yb_o, h_ref, wgu_bf, wd_bf):
    i = pl.program_id(0)
    changed = jnp.logical_or(i == 0, te_ref[i] != te_ref[jnp.maximum(i - 1, 0)])

    @pl.when(jnp.logical_and(i < used_ref[0], changed))
    def _():
        def cast_gu(r, carry):
            rows = pl.ds(pl.multiple_of(r * CAST_ROWS, CAST_ROWS), CAST_ROWS)
            wgu_bf[rows, :] = wgu_ref[0, rows, :].astype(BF16)
            return carry

        def cast_d(r, carry):
            rows = pl.ds(pl.multiple_of(r * CAST_ROWS, CAST_ROWS), CAST_ROWS)
            wd_bf[rows, :] = wd_ref[0, rows, :].astype(BF16)
            return carry

        lax.fori_loop(0, D_MODEL // CAST_ROWS, cast_gu, 0)
        lax.fori_loop(0, D_EXPERT // CAST_ROWS, cast_d, 0)

    @pl.when(i < used_ref[0])
    def _():
        hx = jnp.concatenate([p.astype(BF16) for w in (xa_ref[...], xb_ref[...]) for p in _unpack_pairs(w)],
                             axis=1)
        for c0, cw in EXPERT_CHUNKS:
            g = _dot(hx, wgu_bf[:, c0:c0 + cw])
            u = _dot(hx, wgu_bf[:, D_EXPERT + c0:D_EXPERT + c0 + cw])
            h_ref[:, c0:c0 + cw] = (_silu(g) * u).astype(BF16)
        f = _dot(h_ref[...], wd_bf[...])
        ya_o[...] = _pack_pairs(f[:, 0:256], f[:, 256:512])
        yb_o[...] = _pack_pairs(f[:, 512:768], f[:, 768:1024])


def _moe_experts(tile_expert, n_used, xa, xb, wgu, wd):
    r = xa.shape[0]
    tm = EXPERT_TILE
    rows = pl.BlockSpec((tm, PACK_W), lambda i, te, nu: (i, 0))
    out = jax.ShapeDtypeStruct((r, PACK_W), jnp.uint32)
    return pl.pallas_call(
        _experts_kernel,
        out_shape=(out, out),
        grid_spec=pltpu.PrefetchScalarGridSpec(
            num_scalar_prefetch=2,
            grid=(r // tm,),
            in_specs=[rows, rows,
                      pl.BlockSpec((1, D_MODEL, 2 * D_EXPERT), lambda i, te, nu: (te[i], 0, 0)),
                      pl.BlockSpec((1, D_EXPERT, D_MODEL), lambda i, te, nu: (te[i], 0, 0))],
            out_specs=(rows, rows),
            scratch_shapes=[pltpu.VMEM((tm, D_EXPERT), BF16),
                            pltpu.VMEM((D_MODEL, 2 * D_EXPERT), BF16),
                            pltpu.VMEM((D_EXPERT, D_MODEL), BF16)]),
        compiler_params=pltpu.CompilerParams(
            dimension_semantics=("arbitrary",), vmem_limit_bytes=VMEM_LIMIT),
        name="moe_experts",
    )(tile_expert, n_used, xa, xb, wgu, wd)


def _combine_kernel(x_ref, mod_ref, route_ref, y1a_ref, y1b_ref, y2a_ref, y2b_ref, ln_ref, o_ref):
    route = route_ref[...]
    f = None
    for ya_ref, yb_ref, name in ((y1a_ref, y1b_ref, "p1"), (y2a_ref, y2b_ref, "p2")):
        lane = ROUTE_LANES[name]
        y = jnp.concatenate(_unpack_pairs(ya_ref[...]) + _unpack_pairs(yb_ref[...]), axis=1)
        term = route[:, lane:lane + 1] * y
        f = term if f is None else f + term
    r = DEEPNORM_ALPHA * x_ref[0] + mod_ref[0, 0, 5:6, :] * f
    o_ref[0] = _layer_norm(r, ln_ref[0:1, :], ln_ref[1:2, :])


def _moe_combine(xt, modall, route, yga, ygb, ln, seq):
    b, _, d = xt.shape
    tm = TOKEN_BLOCK
    nj = seq // tm
    nblk = b * nj
    first = lambda width: pl.BlockSpec((tm, width), lambda i, j: (i * nj + j, 0))
    second = lambda width: pl.BlockSpec((tm, width), lambda i, j: (nblk + i * nj + j, 0))
    return pl.pallas_call(
        _combine_kernel,
        out_shape=jax.ShapeDtypeStruct((b, seq, d), F32),
        grid=(b, nj),
        in_specs=[
            pl.BlockSpec((1, tm, d), lambda i, j: (i, j, 0)),
            pl.BlockSpec((1, 1, 6, d), lambda i, j: (i, 0, 0, 0)),
            first(128), first(PACK_W), first(PACK_W), second(PACK_W), second(PACK_W),
            pl.BlockSpec((2, d), lambda i, j: (0, 0)),
        ],
        out_specs=pl.BlockSpec((1, tm, d), lambda i, j: (i, j, 0)),
        compiler_params=pltpu.CompilerParams(
            dimension_semantics=("arbitrary", "arbitrary"), vmem_limit_bytes=VMEM_LIMIT),
        name="moe_combine_ln",
    )(xt, modall, route, yga, ygb, yga, ygb, ln)


def _moe(xt, modall, rw, rb, wgu, wd, ln, seq):
    b = xt.shape[0]
    n = b * seq
    tm = EXPERT_TILE
    hxa, hxb, route, counts = _moe_route(xt, modall, rw, rb, seq)
    counts = counts[0, :N_EXPERTS].astype(jnp.int32)
    sizes = (counts + tm - 1) // tm * tm
    ends = jnp.cumsum(sizes)
    starts = ends - sizes
    col = lambda name: route[:, ROUTE_LANES[name]].astype(jnp.int32)
    pos = jnp.concatenate([jnp.take(starts, col("e1")) + col("rank1"),
                           jnp.take(starts, col("e2")) + col("rank2")])
    n_rows = 2 * n + N_EXPERTS * tm
    tile_start = jnp.arange(n_rows // tm, dtype=jnp.int32) * tm
    tile_expert = jnp.minimum(jnp.sum(tile_start[:, None] >= ends[None, :], axis=1), N_EXPERTS - 1)
    n_used = (ends[-1:] // tm).astype(jnp.int32)
    xa = _sc_scatter_rows(hxa, pos, n_rows)
    xb = _sc_scatter_rows(hxb, pos, n_rows)
    ya, yb = _moe_experts(tile_expert.astype(jnp.int32), n_used, xa, xb, wgu, wd)
    return _moe_combine(xt, modall, route, _sc_gather_rows(ya, pos), _sc_gather_rows(yb, pos), ln, seq)


def _pad_heads(a):
    lead = a.shape[:-1]
    a = a.reshape(*lead, HEADS, ML_DK)
    a = jnp.pad(a, [(0, 0)] * len(lead) + [(0, 0), (0, HEAD_DIM - ML_DK)])
    return a.reshape(*lead, HEADS * HEAD_DIM)


def _permute_w_in(w):
    gates = jnp.pad(w[:, 3584:3600], ((0, 0), (0, 112)))
    cols = [w[:, :2560], _pad_heads(w[:, 2560:2816]), _pad_heads(w[:, 2816:3072]),
            w[:, 3072:3584], w[:, 3600:4112], gates]
    return jnp.concatenate(cols, axis=1).astype(BF16)


def kernel(x, c, ctx, c_ctx, w_ada, b_ada, w_in, ml_conv_w, ml_conv_b, hg_lower_bound, ml_gate_bias,
           hg_norm, ml_norm, w_out, ln_g, ln_b, ffn_w_gate_up, ffn_w_down, router_w, router_b,
           moe_w_gate_up, moe_w_down):
    bsz, seq, d = x.shape
    depth = w_ada.shape[0]
    assert depth == DEPTH and d == D_MODEL and ctx.shape[1] == CTX_LEN and seq % TOKEN_BLOCK == 0
    n_lat_blocks = seq // TOKEN_BLOCK
    n_blocks = n_lat_blocks + CTX_LEN // TOKEN_BLOCK

    lb = jnp.cumsum(jax.nn.softmax(hg_lower_bound.astype(F32), axis=0), axis=0)
    lb = lb - lb[0]

    cond = jnp.concatenate([c, c_ctx[None, :], jnp.zeros((16 - bsz - 1, d), F32)], axis=0)
    mod = _modulation(cond, w_ada, b_ada)
    mod = mod.reshape(depth, 16, 6, d)

    hg_consts = _hgrn2_constants()
    ml_consts = _mlstm_constants()

    lat, cx, cx_block = x, ctx, 0
    out = None
    for l in range(depth):
        last = l == depth - 1
        mod_c = jnp.broadcast_to(mod[l, bsz][None], (bsz, 6, d))
        modall = jnp.stack([mod[l, :bsz], mod_c], axis=1)
        lbp = lb[l]
        taps = jnp.concatenate([ml_conv_w[l], ml_conv_b[l][None]], axis=0)
        convp = jnp.concatenate([_pad_heads(taps[:, :256]), _pad_heads(taps[:, 256:])], axis=1)
        gb = jnp.pad(ml_gate_bias[l].reshape(1, -1), ((0, 0), (0, 112)))
        feats = _inproj(lat, cx, cx_block, modall, _permute_w_in(w_in[l]), lbp, convp, gb, n_lat_blocks)
        hq, lff, lfb, hv, hg_gate, mq, mk, mv, ml_gate, mg = feats
        hg_f, hg_b = _hgrn2_scan(hq, hv, lff, lfb, hg_consts)
        ml_f, ml_b = _mlstm_scan(mq, mk, mv, mg, ml_consts)
        norms = jnp.stack([hg_norm[l].reshape(-1), ml_norm[l].reshape(-1)])
        xt = _mixout(lat, cx, cx_block, hg_f, hg_b, ml_f, ml_b, hg_gate, ml_gate, norms,
                     w_out[l].astype(BF16), modall, jnp.stack([ln_g[l, 0], ln_b[l, 0]]), n_lat_blocks,
                     n_lat_blocks if last else n_blocks)
        ln1 = jnp.stack([ln_g[l, 1], ln_b[l, 1]])
        jj = l // 2
        if l % 2 == 0:
            xt = _ffn(xt, modall, ffn_w_gate_up[jj].astype(BF16), ffn_w_down[jj].astype(BF16), ln1,
                      n_lat_blocks)
            lat, cx, cx_block = xt, xt, n_lat_blocks
        else:
            rw = jnp.pad(router_w[jj], ((0, 0), (0, 128 - N_EXPERTS)))
            rb = jnp.pad(router_b[jj].reshape(1, -1), ((0, 0), (0, 128 - N_EXPERTS)))
            out = _moe(xt, modall, rw, rb, moe_w_gate_up[jj], moe_w_down[jj], ln1, seq)
    return out
```

```python
import functools
import math

import numpy as np
import jax
import jax.numpy as jnp
from jax import lax
from jax.experimental import pallas as pl
from jax.experimental.pallas import tpu as pltpu
from jax.experimental.pallas import tpu_sc as plsc

F32 = jnp.float32
BF16 = jnp.bfloat16

D_MODEL = 1024
CTX_LEN = 256
GRID_W = 64
HG_WIDTH = 512
HEADS = 4
HEAD_DIM = 128
ML_DK = 64
CHUNK = 64
D_FF = 2816
N_EXPERTS = 8
D_EXPERT = 1408
DEPTH = 2
DEEPNORM_ALPHA = (2 * DEPTH) ** 0.25
LOG2_E = 1.4426950408889634
LN_EPS = 1e-5
RMS_EPS = 1e-6

TOKEN_BLOCK = 256
EXPERT_TILE = 256
SCAN_ROWS = 4
ML_CHUNK = 128
PROJ_COLS = 9 * 512 + 128
N_LEVELS = 6
VMEM_LIMIT = 56 * 1024 * 1024

NT_DIMS = (((1,), (1,)), ((), ()))
TN_DIMS = (((0,), (0,)), ((), ()))


def _compiler_params(grid_rank):
    return pltpu.CompilerParams(
        dimension_semantics=("arbitrary",) * grid_rank, vmem_limit_bytes=VMEM_LIMIT)


def _dot(a, b):
    return jnp.dot(a, b, preferred_element_type=F32)


def _dot_nt(a, b):
    return lax.dot_general(a, b, NT_DIMS, preferred_element_type=F32)


def _dot_tn(a, b):
    return lax.dot_general(a, b, TN_DIMS, preferred_element_type=F32)


def _sigmoid(z):
    return 0.5 * jnp.tanh(0.5 * z) + 0.5


def _silu(z):
    h = 0.5 * z
    return h + h * jnp.tanh(h)


def _log_sigmoid(z):
    return jnp.minimum(z, 0.0) - jnp.log(1.0 + jnp.exp(-jnp.abs(z)))


def _split3(x):
    hi = x.astype(BF16)
    r = x - hi.astype(F32)
    mid = r.astype(BF16)
    lo = (r - mid.astype(F32)).astype(BF16)
    return hi, mid, lo


def _layer_norm(r, g, b):
    mu = jnp.mean(r, axis=-1, keepdims=True)
    d = r - mu
    var = jnp.mean(d * d, axis=-1, keepdims=True)
    return d * lax.rsqrt(var + LN_EPS) * g + b


def _mod_kernel(c_ref, w_ref, b_ref, o_ref):
    c = c_ref[...]
    s = _silu(c)
    o_ref[0] = jnp.dot(s, w_ref[0], preferred_element_type=F32,
                       precision=lax.Precision.HIGHEST) + b_ref[0]


def _modulation(cond, w_ada, b_ada):
    depth, d, n = w_ada.shape
    rows = cond.shape[0]
    nb = 1536
    return pl.pallas_call(
        _mod_kernel,
        out_shape=jax.ShapeDtypeStruct((depth, rows, n), F32),
        grid=(depth, n // nb),
        in_specs=[
            pl.BlockSpec((rows, d), lambda l, j: (0, 0)),
            pl.BlockSpec((1, d, nb), lambda l, j: (l, 0, j)),
            pl.BlockSpec((1, 1, nb), lambda l, j: (l, 0, j)),
        ],
        out_specs=pl.BlockSpec((1, rows, nb), lambda l, j: (l, 0, j)),
        compiler_params=_compiler_params(2),
        name="adaln_mod",
    )(cond, w_ada, b_ada.reshape(depth, 1, n))


def _stream_specs(lat, ctx, ctx_block, n_lat_blocks):
    d = lat.shape[-1]
    return [pl.BlockSpec((1, TOKEN_BLOCK, d), lambda i, j: (i, jnp.minimum(j, n_lat_blocks - 1), 0)),
            pl.BlockSpec((1, TOKEN_BLOCK, d), lambda i, j: (i, ctx_block, 0))]


def _inproj_kernel(n_lat_blocks, x_ref, c_ref, mod_ref, w_ref, lb_ref, conv_ref, gb_ref,
                   hq_o, lff_o, lfb_o, hv_o, hg_o, mq_o, mk_o, mv_o, mo_o, mg_o):
    j = pl.program_id(1)
    x = jnp.where(j >= n_lat_blocks, c_ref[0], x_ref[0])
    hx = (x * (1.0 + mod_ref[0, 0, 1:2, :]) + mod_ref[0, 0, 0:1, :]).astype(BF16)

    rows = x.shape[0]
    seg_mask = jnp.where(j >= n_lat_blocks, CTX_LEN - 1, GRID_W - 1)
    pos = lax.broadcasted_iota(jnp.int32, (rows, 1), 0) & seg_mask

    def silu_to(out):
        def finish(z):
            out[0] = _silu(z).astype(BF16)
        return finish

    def log_forget_to(out, r):
        def finish(z):
            lb = lb_ref[r:r + 1, :]
            t = jnp.exp(-jnp.abs(z))
            num = jnp.where(z >= 0.0, 1.0 + lb * t, lb + t)
            out[0] = jnp.maximum(jnp.log2(num), jnp.minimum(z, 0.0) * LOG2_E) - jnp.log2(1.0 + t)
        return finish

    def cast_to(out):
        def finish(z):
            out[0] = z.astype(BF16)
        return finish

    def conv_silu_to(out, part, scale):
        def finish(u):
            taps = conv_ref[:, part * 512:(part + 1) * 512]
            u_prev = jnp.where(pos == 0, 0.0, pltpu.roll(u, 1, axis=0))
            u_next = jnp.where(pos == seg_mask, 0.0, pltpu.roll(u, rows - 1, axis=0))
            qk = taps[3:4, :] + u_prev * taps[0:1, :] + u * taps[1:2, :] + u_next * taps[2:3, :]
            out[0] = (_silu(qk) * scale).astype(BF16)
        return finish

    def sigmoid_to(out):
        def finish(z):
            out[0] = _sigmoid(z).astype(BF16)
        return finish

    def gates_to(out):
        def finish(z):
            g = z + gb_ref[...]
            col = lax.broadcasted_iota(jnp.int32, g.shape, 1)
            is_forget = jnp.logical_and((col & 4) != 0, col < 16)
            out[0] = jnp.where(is_forget, _log_sigmoid(g), g)
        return finish

    finishers = [silu_to(hq_o), log_forget_to(lff_o, 0), log_forget_to(lfb_o, 1), cast_to(hv_o),
                 silu_to(hg_o), conv_silu_to(mq_o, 0, ML_DK ** -0.5), conv_silu_to(mk_o, 1, 1.0),
                 cast_to(mv_o), sigmoid_to(mo_o), gates_to(mg_o)]
    bounds = [(g * 512, (g + 1) * 512) for g in range(9)] + [(9 * 512, PROJ_COLS)]
    z = _dot(hx, w_ref[:, bounds[0][0]:bounds[0][1]])
    for n, finish in enumerate(finishers):
        z_next = None
        if n + 1 < len(bounds):
            z_next = _dot(hx, w_ref[:, bounds[n + 1][0]:bounds[n + 1][1]])
        finish(z)
        z = z_next


def _inproj(lat, ctx, ctx_block, modall, w_bf, lbp, convp, gb, n_lat_blocks):
    b, _, d = lat.shape
    nblk = n_lat_blocks + CTX_LEN // TOKEN_BLOCK
    t = nblk * TOKEN_BLOCK
    tok = lambda width: pl.BlockSpec((1, TOKEN_BLOCK, width), lambda i, j: (i, j, 0))
    shp = lambda width, dt: jax.ShapeDtypeStruct((b, t, width), dt)
    return pl.pallas_call(
        functools.partial(_inproj_kernel, n_lat_blocks),
        out_shape=(shp(512, BF16), shp(512, F32), shp(512, F32), shp(512, BF16), shp(512, BF16),
                   shp(512, BF16), shp(512, BF16), shp(512, BF16), shp(512, BF16), shp(128, F32)),
        grid=(b, nblk),
        in_specs=_stream_specs(lat, ctx, ctx_block, n_lat_blocks) + [
            pl.BlockSpec((1, 1, 6, d), lambda i, j: (i, jnp.where(j >= n_lat_blocks, 1, 0), 0, 0)),
            pl.BlockSpec((d, PROJ_COLS), lambda i, j: (0, 0), pipeline_mode=pl.Buffered(1)),
            pl.BlockSpec((2, 512), lambda i, j: (0, 0)),
            pl.BlockSpec((4, 1024), lambda i, j: (0, 0)),
            pl.BlockSpec((1, 128), lambda i, j: (0, 0)),
        ],
        out_specs=(tok(512), tok(512), tok(512), tok(512), tok(512),
                   tok(512), tok(512), tok(512), tok(512), tok(128)),
        compiler_params=_compiler_params(2),
        name="inproj_features",
    )(lat, ctx, modall, w_bf, lbp, convp, gb)


def _hgrn2_constants():
    c = CHUNK
    amask = np.zeros((N_LEVELS + 1, c, c), np.float32)
    side = np.zeros((3, c, 4 * HEAD_DIM), np.float32)
    for l in range(N_LEVELS):
        s = c >> (l + 1)
        for i in range(c):
            m = (i // (2 * s)) * 2 * s + s
            if i >= m:
                amask[l, i, m - s:m] = 1.0
            if l >= 3:
                side[l - 3, i, :] = 1.0 if i >= m else -1.0
    amask[N_LEVELS] = np.eye(c)
    tri = np.tril(np.ones((c, c), np.float32))
    cum = np.stack([tri, tri.T])
    amask = np.stack([amask, amask[..., ::-1, ::-1]]).reshape(2, (N_LEVELS + 1) * c, c)
    side = np.stack([side, side[:, ::-1]]).reshape(2, 3 * c, 4 * HEAD_DIM)
    return (jnp.asarray(cum, BF16), jnp.asarray(np.ascontiguousarray(side), F32),
            jnp.asarray(np.ascontiguousarray(amask), F32))


def _hgrn2_level_exponents(g_ref, lf, side_ref, d):
    c = CHUNK
    g = g_ref[...]
    width = g.shape[1]

    def split_row(r, n):
        return jnp.broadcast_to(g_ref[pl.ds(r, 1), :], (n, width))

    out = []
    for l in range(3):
        s = c >> (l + 1)
        parts = []
        for r0 in range(0, c, 2 * s):
            lo, hi = g[r0:r0 + s], g[r0 + s:r0 + 2 * s]
            if d == 0:
                gm = split_row(r0 + s - 1, s)
                parts += [gm - lo, hi - gm]
            else:
                gm = split_row(r0 + s, s)
                parts += [lo - gm, gm - hi]
        out.append(jnp.concatenate(parts, axis=0))
    sub = lax.broadcasted_iota(jnp.int32, (8, width), 0)
    for l, s in ((3, 4), (4, 2)):
        tiles = []
        for r0 in range(0, c, 8):
            if s == 4:
                gm = split_row(r0 + (3 if d == 0 else 4), 8)
            else:
                a, b = (1, 5) if d == 0 else (2, 6)
                gm = jnp.where(sub < 4, split_row(r0 + a, 8), split_row(r0 + b, 8))
            tiles.append(gm)
        gm = jnp.concatenate(tiles, axis=0)
        out.append((g - gm) * side_ref[d, (l - 3) * c:(l - 2) * c, :])
    out.append(lf * jnp.maximum(side_ref[d, 2 * c:3 * c, :], 0.0))
    return out


def _hgrn2_kernel(qf_ref, qb_ref, vf_ref, vb_ref, lff_ref, lfb_ref, cum_ref, side_ref, amask_ref,
                  of_ref, ob_ref, st_ref, g_ref):
    c = CHUNK

    @pl.when(pl.program_id(1) == 0)
    def _():
        st_ref[...] = jnp.zeros_like(st_ref)

    dirs = ((qf_ref, vf_ref, lff_ref, of_ref, c - 1), (qb_ref, vb_ref, lfb_ref, ob_ref, 0))
    streams = [(bi, d) + dirs[d] for bi in range(qf_ref.shape[0]) for d in range(2)]
    w = HG_WIDTH
    heads = [slice(h * HEAD_DIM, (h + 1) * HEAD_DIM) for h in range(HEADS)]

    for bi, d, _, _, lf_ref, _, _ in streams:
        sums = _dot(cum_ref[d], jnp.concatenate(_split3(lf_ref[bi]), axis=1))
        g_ref[bi, d] = sums[:, :w] + sums[:, w:2 * w] + sums[:, 2 * w:]

    feats = []
    for bi, d, q_ref, _, lf_ref, _, end_row in streams:
        lf = lf_ref[bi]
        gd_ref = g_ref.at[bi, d]
        g = gd_ref[...]
        g_end = jnp.broadcast_to(gd_ref[pl.ds(end_row, 1), :], (c, w))
        q = q_ref[bi].astype(F32)
        k = 1.0 - jnp.exp2(lf)
        exps = _hgrn2_level_exponents(gd_ref, lf, side_ref, d)
        ts = []
        for l in range(N_LEVELS):
            s = c >> (l + 1)
            if l < 3:
                parts = []
                for r0 in range(0, c, 2 * s):
                    first, second = (k, q) if d == 0 else (q, k)
                    parts += [first[r0:r0 + s], second[r0 + s:r0 + 2 * s]]
                qk = jnp.concatenate(parts, axis=0)
            else:
                qk = jnp.where(side_ref[d, (l - 3) * c:(l - 2) * c, :] > 0.0, q, k)
            ts.append((qk * jnp.exp2(exps[l])).astype(BF16))
        feats.append(dict(
            ts=ts, q_bf=q.astype(BF16), k_bf=k.astype(BF16),
            q_in=(q * jnp.exp2(g)).astype(BF16),
            k_out=(k * jnp.exp2(g_end - g)).astype(BF16),
            total=jnp.exp2(gd_ref[pl.ds(end_row, 1), :])))

    diag_row = N_LEVELS * c
    prods = []
    for (bi, d, *_), f in zip(streams, feats):
        for sl in heads:
            p = [_dot_nt(t[:, sl], t[:, sl]) for t in f["ts"]]
            p.append(_dot_nt(f["q_bf"][:, sl], f["k_bf"][:, sl]))
            prods.append(p)

    intra = []
    for n, (bi, d, *_) in enumerate(streams):
        for h in range(HEADS):
            p = prods[n * HEADS + h]
            a = p[N_LEVELS] * amask_ref[d, diag_row:diag_row + c, :]
            for l in range(N_LEVELS):
                a = a + p[l] * amask_ref[d, l * c:(l + 1) * c, :]
            intra.append(a.astype(BF16))

    for n, ((bi, d, _, v_ref, _, o_ref, _), f) in enumerate(zip(streams, feats)):
        for h, sl in enumerate(heads):
            v = v_ref[bi, :, sl]
            state = st_ref[bi, d, h]
            o = _dot(intra[n * HEADS + h], v) + _dot_nt(f["q_in"][:, sl], state.astype(BF16))
            o_ref[bi, :, sl] = o.astype(BF16)
            st_ref[bi, d, h] = state * f["total"][:, sl] + _dot_tn(v, f["k_out"][:, sl])


def _scan_chunk_maps(n_lat, n_ctx):
    n = n_lat + n_ctx

    def fwd(i, c):
        return (i, jnp.where(c < n_ctx, n_lat + c, c - n_ctx), 0)

    def bwd(i, c):
        return (i, jnp.where(c < n_ctx, n - 1 - c, n - 1 - c), 0)

    return fwd, bwd


def _hgrn2_scan(hq, hv, lff, lfb, consts):
    b, t, w = hq.shape
    n_ctx = CTX_LEN // CHUNK
    n_lat = t // CHUNK - n_ctx
    fwd, bwd = _scan_chunk_maps(n_lat, n_ctx)
    nb = math.gcd(SCAN_ROWS, b)
    blk = lambda m: pl.BlockSpec((nb, CHUNK, w), m)
    const = lambda a: pl.BlockSpec(a.shape, lambda i, c: (0, 0, 0))
    cum, side, amask = consts
    out = jax.ShapeDtypeStruct((b, t, w), BF16)
    return pl.pallas_call(
        _hgrn2_kernel,
        out_shape=(out, out),
        grid=(b // nb, t // CHUNK),
        in_specs=[blk(fwd), blk(bwd), blk(fwd), blk(bwd), blk(fwd), blk(bwd),
                  const(cum), const(side), const(amask)],
        out_specs=(blk(fwd), blk(bwd)),
        scratch_shapes=[pltpu.VMEM((nb, 2, HEADS, HEAD_DIM, HEAD_DIM), F32),
                        pltpu.VMEM((nb, 2, CHUNK, w), F32)],
        compiler_params=_compiler_params(2),
        name="hgrn2_scan",
    )(hq, hq, hv, hv, lff, lfb, cum, side, amask)


def _mlstm_constants():
    c = ML_CHUNK
    tri = np.tril(np.ones((c, c), np.float32))
    cum = np.stack([tri, tri.T])
    negmask = np.where(cum > 0.5, 0.0, -1e30).astype(np.float32)
    sel = np.zeros((2, 3, 128, HEADS, HEAD_DIM), np.float32)
    for d in range(2):
        for h in range(HEADS):
            sel[d, :, 8 * d + h, h, :] = 1.0
    sel = sel.reshape(2, 3 * 128, HEADS * HEAD_DIM)
    return jnp.asarray(cum, BF16), jnp.asarray(negmask, F32), jnp.asarray(sel, BF16)


def _running_max(x, reverse):
    n, lanes = x.shape
    row = lax.broadcasted_iota(jnp.int32, (n, lanes), 0)
    shift = 1
    while shift < n:
        if shift % 8 == 0:
            pad = jnp.full((shift, lanes), -jnp.inf, x.dtype)
            shifted = (jnp.concatenate([x[shift:], pad], axis=0) if reverse
                       else jnp.concatenate([pad, x[:n - shift]], axis=0))
        elif reverse:
            shifted = jnp.where(row >= n - shift, -jnp.inf, pltpu.roll(x, n - shift, axis=0))
        else:
            shifted = jnp.where(row < shift, -jnp.inf, pltpu.roll(x, shift, axis=0))
        x = jnp.maximum(x, shifted)
        shift *= 2
    return x


def _mlstm_kernel(qf_ref, qb_ref, kf_ref, kb_ref, vf_ref, vb_ref, gf_ref, gb_ref,
                  cum_ref, neg_ref, sel_ref, hf_ref, hb_ref, cn_ref, m_ref):
    c = ML_CHUNK

    @pl.when(pl.program_id(1) == 0)
    def _():
        cn_ref[...] = jnp.zeros_like(cn_ref)
        m_ref[...] = jnp.zeros_like(m_ref)

    dirs = ((qf_ref, kf_ref, vf_ref, gf_ref, hf_ref, c - 1), (qb_ref, kb_ref, vb_ref, gb_ref, hb_ref, 0))
    streams = [(bi, d) + dirs[d] for bi in range(qf_ref.shape[0]) for d in range(2)]
    heads = [slice(h * HEAD_DIM, (h + 1) * HEAD_DIM) for h in range(HEADS)]
    lane = lax.broadcasted_iota(jnp.int32, (c, 128), 1)
    cat3 = lambda a: jnp.concatenate(_split3(a), axis=1)

    cums = []
    for bi, d, _, _, _, g_ref, _, _ in streams:
        sums = _dot(cum_ref[d], cat3(g_ref[bi]))
        cums.append(sums[:, :128] + sums[:, 128:256] + sums[:, 256:])

    small = []
    for (bi, d, _, _, _, g_ref, _, end_row), cs in zip(streams, cums):
        valid = jnp.logical_and(lane >= 8 * d, lane < 8 * d + HEADS)
        b_sh = jnp.where(valid, pltpu.roll(cs, 128 - HEADS, axis=1), 0.0)
        u = jnp.where(valid, g_ref[bi], 0.0) - b_sh
        mu = m_ref[bi, d][0:1, :]
        m_run = jnp.maximum(_running_max(u, d == 1), mu)
        m_end = m_run[end_row:end_row + 1, :]
        w_inter = jnp.exp(mu - m_run)
        m_ref[bi, d] = jnp.broadcast_to(b_sh[end_row:end_row + 1, :] + m_end, (8, 128))
        small.append(dict(
            m_run=m_run, w_inter=w_inter.astype(BF16), w_k=jnp.exp(u - m_end).astype(BF16),
            floor=jnp.exp(-(b_sh + m_run)).astype(BF16),
            carry=jnp.broadcast_to(w_inter[end_row:end_row + 1, :], (8, 128)), u_t=u.T))

    wide = []
    for (bi, d, *_), sm in zip(streams, small):
        sel3 = sel_ref[d]
        sel1 = sel_ref[d, 0:128, :]
        wide.append(dict(
            m_run=_dot(cat3(sm["m_run"]), sel3), w_inter=_dot(sm["w_inter"], sel1),
            w_k=_dot(sm["w_k"], sel1), floor=_dot(sm["floor"], sel1),
            carry=_dot(cat3(sm["carry"]), sel3)))

    qk = []
    for bi, d, q_ref, k_ref, *_ in streams:
        qk.append([_dot_nt(q_ref[bi, :, hb], k_ref[bi, :, hb]) for hb in heads])

    ones = jnp.ones((c, HEAD_DIM), BF16)
    zeros = jnp.zeros((HEAD_DIM - ML_DK, 2 * HEAD_DIM), BF16)
    operands = []
    for n, (bi, d, q_ref, _, v_ref, _, _, _) in enumerate(streams):
        for h, hb in enumerate(heads):
            u_row = jnp.broadcast_to(small[n]["u_t"][8 * d + h:8 * d + h + 1, :], (c, c))
            w = jnp.exp(u_row - wide[n]["m_run"][:, hb] + neg_ref[d])
            s = (qk[n][h] * w).astype(BF16)
            q_w = (q_ref[bi, :, hb].astype(F32) * wide[n]["w_inter"][:, hb]).astype(BF16)
            v1 = jnp.concatenate([v_ref[bi, :, hb], ones], axis=1)
            cn = cn_ref[bi, d, h]
            rhs = jnp.concatenate([v1, cn.astype(BF16), zeros], axis=0)
            operands.append((jnp.concatenate([s, q_w], axis=1), rhs, v1, cn))

    for n, (bi, d, _, k_ref, _, _, h_ref, _) in enumerate(streams):
        for h, hb in enumerate(heads):
            lhs, rhs, v1, cn = operands[n * HEADS + h]
            res = _dot(lhs, rhs)
            k_w = (k_ref[bi, :, hb].astype(F32) * wide[n]["w_k"][:, hb]).astype(BF16)
            upd = _dot_tn(k_w, v1)[:ML_DK]
            h_ref[bi, :, hb] = (res[:, :HEAD_DIM] / jnp.maximum(
                jnp.abs(res[:, HEAD_DIM:]), wide[n]["floor"][:, hb])).astype(BF16)
            carry = wide[n]["carry"][0:1, hb]
            cn_ref[bi, d, h] = cn * jnp.concatenate([carry, carry], axis=1) + upd


def _mlstm_scan(mq, mk, mv, mg, consts):
    b, t, w = mq.shape
    n_ctx = CTX_LEN // ML_CHUNK
    n_lat = t // ML_CHUNK - n_ctx
    fwd, bwd = _scan_chunk_maps(n_lat, n_ctx)
    nb = math.gcd(SCAN_ROWS, b)
    blk = lambda width, m: pl.BlockSpec((nb, ML_CHUNK, width), m)
    const = lambda a: pl.BlockSpec(a.shape, lambda i, c: (0, 0, 0))
    cum, negmask, sel = consts
    out = jax.ShapeDtypeStruct((b, t, w), BF16)
    return pl.pallas_call(
        _mlstm_kernel,
        out_shape=(out, out),
        grid=(b // nb, t // ML_CHUNK),
        in_specs=[blk(w, fwd), blk(w, bwd), blk(w, fwd), blk(w, bwd),
                  blk(w, fwd), blk(w, bwd), blk(128, fwd), blk(128, bwd),
                  const(cum), const(negmask), const(sel)],
        out_specs=(blk(w, fwd), blk(w, bwd)),
        scratch_shapes=[pltpu.VMEM((nb, 2, HEADS, ML_DK, 2 * HEAD_DIM), F32),
                        pltpu.VMEM((nb, 2, 8, 128), F32)],
        compiler_params=_compiler_params(2),
        name="mlstm_scan",
    )(mq, mq, mk, mk, mv, mv, mg, mg, cum, negmask, sel)


def _route_block(hx, rwt_ref, rbt_ref, tri_ref, hxa_o, hxb_o, route_o, cnt_o, carry_ref):
    @pl.when(jnp.logical_and(pl.program_id(0) == 0, pl.program_id(1) == 0))
    def _():
        carry_ref[...] = jnp.zeros_like(carry_ref)

    hxa_o[...] = _pack_pairs(hx[:, 0:256], hx[:, 256:512])
    hxb_o[...] = _pack_pairs(hx[:, 512:768], hx[:, 768:1024])
    h_hi = hx.astype(BF16)
    h_lo = (hx - h_hi.astype(F32)).astype(BF16)
    by_hi = _dot_nt(rwt_ref[...], h_hi)
    logits = (by_hi[:N_EXPERTS] + by_hi[N_EXPERTS:] + _dot_nt(rwt_ref[0:N_EXPERTS, :], h_lo)
              + rbt_ref[...])
    sub = lax.broadcasted_iota(jnp.int32, logits.shape, 0).astype(F32)
    top1 = jnp.max(logits, axis=0, keepdims=True)
    idx1 = jnp.min(jnp.where(logits == top1, sub, float(N_EXPERTS)), axis=0, keepdims=True)
    rest = jnp.where(sub == idx1, -jnp.inf, logits)
    top2 = jnp.max(rest, axis=0, keepdims=True)
    idx2 = jnp.min(jnp.where(rest == top2, sub, float(N_EXPERTS)), axis=0, keepdims=True)
    p1 = 1.0 / (1.0 + jnp.exp(top2 - top1))
    p2 = 1.0 - p1
    oh1 = jnp.where(sub == idx1, 1.0, 0.0)
    oh2 = jnp.where(sub == idx2, 1.0, 0.0)
    both = oh1 + oh2
    before = _dot(both.astype(BF16), tri_ref[...]) + carry_ref[...]
    rank1 = jnp.sum(before * oh1, axis=0, keepdims=True)
    rank2 = jnp.sum(before * oh2, axis=0, keepdims=True)
    total = carry_ref[...] + jnp.broadcast_to(jnp.sum(both, axis=1, keepdims=True), both.shape)
    carry_ref[...] = total
    cnt_o[...] = total
    route = jnp.zeros_like(logits)
    for name, val in (("e1", idx1), ("e2", idx2), ("rank1", rank1), ("rank2", rank2), ("p1", p1), ("p2", p2)):
        route = jnp.where(sub == float(ROUTE_ROWS[name]), val, route)
    route_o[...] = route


def _mixout_kernel(n_lat_blocks, with_route, x_ref, c_ref, hf_ref, hb_ref, mf_ref, mb_ref, hg_ref, mo_ref,
                   nrm_ref, w_ref, mod_ref, ln_ref, *rest):
    x = jnp.where(pl.program_id(1) >= n_lat_blocks, c_ref[0], x_ref[0])
    parts = []
    streams = ((hf_ref, hb_ref, hg_ref, 0), (mf_ref, mb_ref, mo_ref, 1))
    for f_ref, b_ref, gate_ref, row in streams:
        o = f_ref[0].astype(F32) + b_ref[0].astype(F32)
        gate = gate_ref[0].astype(F32)
        for h in range(HEADS):
            sl = slice(h * HEAD_DIM, (h + 1) * HEAD_DIM)
            oh = o[:, sl]
            ms = jnp.mean(oh * oh, axis=-1, keepdims=True)
            parts.append((oh * lax.rsqrt(ms + RMS_EPS) * nrm_ref[row:row + 1, sl] * gate[:, sl]).astype(BF16))
    y = jnp.concatenate(parts, axis=1)
    mix = _dot(y, w_ref[...])
    r = DEEPNORM_ALPHA * x + mod_ref[0, 0, 2:3, :] * mix
    xn = _layer_norm(r, ln_ref[0:1, :], ln_ref[1:2, :])
    if not with_route:
        (o_ref,) = rest
        o_ref[0] = xn
        return
    rwt_ref, rbt_ref, tri_ref, o_ref, hxa_o, hxb_o, route_o, cnt_o, carry_ref = rest
    o_ref[0] = xn
    hx = xn * (1.0 + mod_ref[0, 0, 4:5, :]) + mod_ref[0, 0, 3:4, :]
    _route_block(hx, rwt_ref, rbt_ref, tri_ref, hxa_o, hxb_o, route_o, cnt_o, carry_ref)


def _mixout(lat, ctx, ctx_block, hg_f, hg_b, ml_f, ml_b, hg_gate, ml_gate, norms, w_bf, modall, ln,
            n_lat_blocks, n_blocks, router=None):
    b, _, d = lat.shape
    tm = TOKEN_BLOCK
    tok = lambda width: pl.BlockSpec((1, tm, width), lambda i, j: (i, j, 0))
    const = lambda shape: pl.BlockSpec(shape, lambda i, j: (0,) * len(shape))
    in_specs = _stream_specs(lat, ctx, ctx_block, n_lat_blocks) + [
        tok(512), tok(512), tok(512), tok(512), tok(512), tok(512),
        const((2, 512)),
        pl.BlockSpec((d, d), lambda i, j: (0, 0), pipeline_mode=pl.Buffered(1)),
        pl.BlockSpec((1, 1, 6, d), lambda i, j: (i, jnp.where(j >= n_lat_blocks, 1, 0), 0, 0)),
        const((2, d)),
    ]
    args = [lat, ctx, hg_f, hg_b, ml_f, ml_b, hg_gate, ml_gate, norms, w_bf, modall, ln]
    out_shape = [jax.ShapeDtypeStruct((b, n_blocks * tm, d), F32)]
    out_specs = [tok(d)]
    scratch = []
    if router is not None:
        assert n_blocks == n_lat_blocks
        rw, rb = router
        n = b * n_blocks * tm
        rows = lambda width: pl.BlockSpec((tm, width), lambda i, j: (i * n_blocks + j, 0))
        rwt_hi = rw.T.astype(BF16)
        rwt = jnp.concatenate([rwt_hi, (rw.T - rwt_hi.astype(F32)).astype(BF16)], axis=0)
        in_specs += [const((2 * N_EXPERTS, d)), const((N_EXPERTS, tm)), const((tm, tm))]
        args += [rwt, jnp.broadcast_to(rb.reshape(N_EXPERTS, 1), (N_EXPERTS, tm)),
                 jnp.asarray(np.triu(np.ones((tm, tm), np.float32), 1), BF16)]
        out_shape += [jax.ShapeDtypeStruct((n, PACK_W), jnp.uint32), jax.ShapeDtypeStruct((n, PACK_W), jnp.uint32),
                      jax.ShapeDtypeStruct((N_EXPERTS, n), F32), jax.ShapeDtypeStruct((N_EXPERTS, tm), F32)]
        out_specs += [rows(PACK_W), rows(PACK_W),
                      pl.BlockSpec((N_EXPERTS, tm), lambda i, j: (0, i * n_blocks + j)), const((N_EXPERTS, tm))]
        scratch = [pltpu.VMEM((N_EXPERTS, tm), F32)]
    return pl.pallas_call(
        functools.partial(_mixout_kernel, n_lat_blocks, router is not None),
        out_shape=tuple(out_shape),
        grid=(b, n_blocks),
        in_specs=in_specs,
        out_specs=tuple(out_specs),
        scratch_shapes=scratch,
        compiler_params=_compiler_params(2),
        name="mix_out_ln_route" if router is not None else "mix_out_ln",
    )(*args)


FF_CHUNK = 256


def _ffn_kernel(x_ref, mod_ref, wgu_ref, wd_ref, ln_ref, o_ref, h_ref):
    x = x_ref[0]
    hx = (x * (1.0 + mod_ref[0, 0, 4:5, :]) + mod_ref[0, 0, 3:4, :]).astype(BF16)
    for c0 in range(0, D_FF, FF_CHUNK):
        g = _dot(hx, wgu_ref[:, c0:c0 + FF_CHUNK])
        u = _dot(hx, wgu_ref[:, D_FF + c0:D_FF + c0 + FF_CHUNK])
        h_ref[:, c0:c0 + FF_CHUNK] = (_silu(g) * u).astype(BF16)
    f = _dot(h_ref[...], wd_ref[...])
    r = DEEPNORM_ALPHA * x + mod_ref[0, 0, 5:6, :] * f
    o_ref[0] = _layer_norm(r, ln_ref[0:1, :], ln_ref[1:2, :])


def _ffn(xt, modall, wgu_bf, wd_bf, ln, n_lat_blocks):
    b, t, d = xt.shape
    tok = pl.BlockSpec((1, TOKEN_BLOCK, d), lambda i, j: (i, j, 0))
    return pl.pallas_call(
        _ffn_kernel,
        out_shape=jax.ShapeDtypeStruct((b, t, d), F32),
        grid=(b, t // TOKEN_BLOCK),
        in_specs=[
            tok,
            pl.BlockSpec((1, 1, 6, d), lambda i, j: (i, jnp.where(j >= n_lat_blocks, 1, 0), 0, 0)),
            pl.BlockSpec((d, 2 * D_FF), lambda i, j: (0, 0), pipeline_mode=pl.Buffered(1)),
            pl.BlockSpec((D_FF, d), lambda i, j: (0, 0), pipeline_mode=pl.Buffered(1)),
            pl.BlockSpec((2, d), lambda i, j: (0, 0)),
        ],
        out_specs=tok,
        scratch_shapes=[pltpu.VMEM((TOKEN_BLOCK, D_FF), BF16)],
        compiler_params=_compiler_params(2),
        name="ffn_dense_ln",
    )(xt, modall, wgu_bf, wd_bf, ln)


EXPERT_CHUNKS = ((0, 256), (256, 256), (512, 256), (768, 256), (1024, 256), (1280, 128))
ROUTE_ROWS = dict(e1=0, e2=1, rank1=2, rank2=3, p1=4, p2=5)
PACK_W = 256
SC_WINDOW = 128


def _pack_pairs(a, b):
    ua = lax.bitcast_convert_type(a.astype(BF16).astype(F32), jnp.uint32)
    ub = lax.bitcast_convert_type(b.astype(BF16).astype(F32), jnp.uint32)
    return (ua >> 16) | ub


def _unpack_pairs(w):
    a = lax.bitcast_convert_type(w << 16, F32)
    b = lax.bitcast_convert_type(w & jnp.uint32(0xFFFF0000), F32)
    return a, b


def _sc_mesh():
    return plsc.VectorSubcoreMesh(core_axis_name="core", subcore_axis_name="subcore")


def _sc_scatter_rows(rows, idx, n_out):
    n_src, width = rows.shape
    m = idx.shape[0]
    src_blocks = n_src // SC_WINDOW

    @functools.partial(pl.kernel, out_type=jax.ShapeDtypeStruct((n_out, width), rows.dtype), mesh=_sc_mesh())
    def scatter(x_hbm, i_hbm, o_hbm):
        def body(x_vmem, i_vmem):
            pltpu.sync_copy(x_vmem, o_hbm.at[i_vmem.at[0]])

        pltpu.emit_pipeline(
            body, grid=(m // SC_WINDOW,),
            in_specs=[pl.BlockSpec((SC_WINDOW, width), lambda i: (i % src_blocks, 0)),
                      pl.BlockSpec((1, SC_WINDOW), lambda i: (0, i))],
            out_specs=[],
            core_axis_name=("core", "subcore"), dimension_semantics=(pltpu.PARALLEL,),
        )(x_hbm, i_hbm)

    return scatter(rows, idx.reshape(1, m))


def _sc_gather_rows(table, idx):
    m = idx.shape[0]
    width = table.shape[1]

    @functools.partial(pl.kernel, out_type=jax.ShapeDtypeStruct((m, width), table.dtype), mesh=_sc_mesh())
    def gather(t_hbm, i_hbm, o_hbm):
        def body(i_vmem, o_vmem):
            pltpu.sync_copy(t_hbm.at[i_vmem.at[0]], o_vmem)

        pltpu.emit_pipeline(
            body, grid=(m // SC_WINDOW,),
            in_specs=[pl.BlockSpec((1, SC_WINDOW), lambda i: (0, i))],
            out_specs=[pl.BlockSpec((SC_WINDOW, width), lambda i: (i, 0))],
            core_axis_name=("core", "subcore"), dimension_semantics=(pltpu.PARALLEL,),
        )(i_hbm, o_hbm)

    return gather(table, idx.reshape(1, m))


CAST_ROWS = 64


def _experts_kernel(te_ref, used_ref, xa_ref, xb_ref, wgu_ref, wd_ref, ya_o, yb_o, h_ref, wgu_bf, wd_bf):
    i = pl.program_id(0)
    changed = jnp.logical_or(i == 0, te_ref[i] != te_ref[jnp.maximum(i - 1, 0)])

    @pl.when(jnp.logical_and(i < used_ref[0], changed))
    def _():
        def cast_gu(r, carry):
            rows = pl.ds(pl.multiple_of(r * CAST_ROWS, CAST_ROWS), CAST_ROWS)
            wgu_bf[rows, :] = wgu_ref[0, rows, :].astype(BF16)
            return carry

        def cast_d(r, carry):
            rows = pl.ds(pl.multiple_of(r * CAST_ROWS, CAST_ROWS), CAST_ROWS)
            wd_bf[rows, :] = wd_ref[0, rows, :].astype(BF16)
            return carry

        lax.fori_loop(0, D_MODEL // CAST_ROWS, cast_gu, 0)
        lax.fori_loop(0, D_EXPERT // CAST_ROWS, cast_d, 0)

    @pl.when(i < used_ref[0])
    def _():
        hx = jnp.concatenate([p.astype(BF16) for w in (xa_ref[...], xb_ref[...]) for p in _unpack_pairs(w)],
                             axis=1)
        for c0, cw in EXPERT_CHUNKS:
            g = _dot(hx, wgu_bf[:, c0:c0 + cw])
            u = _dot(hx, wgu_bf[:, D_EXPERT + c0:D_EXPERT + c0 + cw])
            h_ref[:, c0:c0 + cw] = (_silu(g) * u).astype(BF16)
        f = _dot(h_ref[...], wd_bf[...])
        ya_o[...] = _pack_pairs(f[:, 0:256], f[:, 256:512])
        yb_o[...] = _pack_pairs(f[:, 512:768], f[:, 768:1024])


def _moe_experts(tile_expert, n_used, xa, xb, wgu, wd):
    r = xa.shape[0]
    tm = EXPERT_TILE
    rows = pl.BlockSpec((tm, PACK_W), lambda i, te, nu: (i, 0))
    out = jax.ShapeDtypeStruct((r, PACK_W), jnp.uint32)
    return pl.pallas_call(
        _experts_kernel,
        out_shape=(out, out),
        grid_spec=pltpu.PrefetchScalarGridSpec(
            num_scalar_prefetch=2,
            grid=(r // tm,),
            in_specs=[rows, rows,
                      pl.BlockSpec((1, D_MODEL, 2 * D_EXPERT), lambda i, te, nu: (te[i], 0, 0)),
                      pl.BlockSpec((1, D_EXPERT, D_MODEL), lambda i, te, nu: (te[i], 0, 0))],
            out_specs=(rows, rows),
            scratch_shapes=[pltpu.VMEM((tm, D_EXPERT), BF16),
                            pltpu.VMEM((D_MODEL, 2 * D_EXPERT), BF16),
                            pltpu.VMEM((D_EXPERT, D_MODEL), BF16)]),
        compiler_params=_compiler_params(1),
        name="moe_experts",
    )(tile_expert, n_used, xa, xb, wgu, wd)


def _combine_kernel(x_ref, mod_ref, route_ref, y1a_ref, y1b_ref, y2a_ref, y2b_ref, ln_ref, o_ref):
    route = route_ref[...].T
    f = None
    for ya_ref, yb_ref, name in ((y1a_ref, y1b_ref, "p1"), (y2a_ref, y2b_ref, "p2")):
        col = ROUTE_ROWS[name]
        y = jnp.concatenate(_unpack_pairs(ya_ref[...]) + _unpack_pairs(yb_ref[...]), axis=1)
        term = route[:, col:col + 1] * y
        f = term if f is None else f + term
    r = DEEPNORM_ALPHA * x_ref[0] + mod_ref[0, 0, 5:6, :] * f
    o_ref[0] = _layer_norm(r, ln_ref[0:1, :], ln_ref[1:2, :])


def _moe_combine(xt, modall, route, yga, ygb, ln, seq):
    b, _, d = xt.shape
    tm = TOKEN_BLOCK
    nj = seq // tm
    nblk = b * nj
    first = lambda width: pl.BlockSpec((tm, width), lambda i, j: (i * nj + j, 0))
    second = lambda width: pl.BlockSpec((tm, width), lambda i, j: (nblk + i * nj + j, 0))
    return pl.pallas_call(
        _combine_kernel,
        out_shape=jax.ShapeDtypeStruct((b, seq, d), F32),
        grid=(b, nj),
        in_specs=[
            pl.BlockSpec((1, tm, d), lambda i, j: (i, j, 0)),
            pl.BlockSpec((1, 1, 6, d), lambda i, j: (i, 0, 0, 0)),
            pl.BlockSpec((N_EXPERTS, tm), lambda i, j: (0, i * nj + j)),
            first(PACK_W), first(PACK_W), second(PACK_W), second(PACK_W),
            pl.BlockSpec((2, d), lambda i, j: (0, 0)),
        ],
        out_specs=pl.BlockSpec((1, tm, d), lambda i, j: (i, j, 0)),
        compiler_params=_compiler_params(2),
        name="moe_combine_ln",
    )(xt, modall, route, yga, ygb, yga, ygb, ln)


def _moe(xt, modall, hxa, hxb, route, counts, wgu, wd, ln, seq):
    b = xt.shape[0]
    n = b * seq
    tm = EXPERT_TILE
    counts = counts[:, 0].astype(jnp.int32)
    sizes = (counts + tm - 1) // tm * tm
    ends = jnp.cumsum(sizes)
    starts = ends - sizes
    col = lambda name: route[ROUTE_ROWS[name]].astype(jnp.int32)
    pos = jnp.concatenate([jnp.take(starts, col("e1")) + col("rank1"),
                           jnp.take(starts, col("e2")) + col("rank2")])
    n_rows = 2 * n + N_EXPERTS * tm
    tile_start = jnp.arange(n_rows // tm, dtype=jnp.int32) * tm
    tile_expert = jnp.minimum(jnp.sum(tile_start[:, None] >= ends[None, :], axis=1), N_EXPERTS - 1)
    n_used = (ends[-1:] // tm).astype(jnp.int32)
    xa = _sc_scatter_rows(hxa, pos, n_rows)
    xb = _sc_scatter_rows(hxb, pos, n_rows)
    ya, yb = _moe_experts(tile_expert.astype(jnp.int32), n_used, xa, xb, wgu, wd)
    return _moe_combine(xt, modall, route, _sc_gather_rows(ya, pos), _sc_gather_rows(yb, pos), ln, seq)


def _pad_heads(a):
    lead = a.shape[:-1]
    a = a.reshape(*lead, HEADS, ML_DK)
    a = jnp.pad(a, [(0, 0)] * len(lead) + [(0, 0), (0, HEAD_DIM - ML_DK)])
    return a.reshape(*lead, HEADS * HEAD_DIM)


def _permute_w_in(w):
    gates = jnp.pad(w[:, 3584:3600], ((0, 0), (0, 112)))
    cols = [w[:, :2560], _pad_heads(w[:, 2560:2816]), _pad_heads(w[:, 2816:3072]),
            w[:, 3072:3584], w[:, 3600:4112], gates]
    return jnp.concatenate(cols, axis=1).astype(BF16)


def kernel(x, c, ctx, c_ctx, w_ada, b_ada, w_in, ml_conv_w, ml_conv_b, hg_lower_bound, ml_gate_bias,
           hg_norm, ml_norm, w_out, ln_g, ln_b, ffn_w_gate_up, ffn_w_down, router_w, router_b,
           moe_w_gate_up, moe_w_down):
    bsz, seq, d = x.shape
    depth = w_ada.shape[0]
    assert depth == DEPTH and d == D_MODEL and ctx.shape[1] == CTX_LEN and seq % TOKEN_BLOCK == 0
    n_lat_blocks = seq // TOKEN_BLOCK
    n_blocks = n_lat_blocks + CTX_LEN // TOKEN_BLOCK

    lb = jnp.cumsum(jax.nn.softmax(hg_lower_bound.astype(F32), axis=0), axis=0)
    lb = lb - lb[0]

    cond = jnp.concatenate([c, c_ctx[None, :], jnp.zeros((16 - bsz - 1, d), F32)], axis=0)
    mod = _modulation(cond, w_ada, b_ada)
    mod = mod.reshape(depth, 16, 6, d)

    hg_consts = _hgrn2_constants()
    ml_consts = _mlstm_constants()

    lat, cx, cx_block = x, ctx, 0
    out = None
    for l in range(depth):
        last = l == depth - 1
        mod_c = jnp.broadcast_to(mod[l, bsz][None], (bsz, 6, d))
        modall = jnp.stack([mod[l, :bsz], mod_c], axis=1)
        lbp = lb[l]
        taps = jnp.concatenate([ml_conv_w[l], ml_conv_b[l][None]], axis=0)
        convp = jnp.concatenate([_pad_heads(taps[:, :256]), _pad_heads(taps[:, 256:])], axis=1)
        gb = jnp.pad(ml_gate_bias[l].reshape(1, -1), ((0, 0), (0, 112)))
        feats = _inproj(lat, cx, cx_block, modall, _permute_w_in(w_in[l]), lbp, convp, gb, n_lat_blocks)
        hq, lff, lfb, hv, hg_gate, mq, mk, mv, ml_gate, mg = feats
        hg_f, hg_b = _hgrn2_scan(hq, hv, lff, lfb, hg_consts)
        ml_f, ml_b = _mlstm_scan(mq, mk, mv, mg, ml_consts)
        norms = jnp.stack([hg_norm[l].reshape(-1), ml_norm[l].reshape(-1)])
        jj = l // 2
        dense = l % 2 == 0
        assert dense or last
        mixed = _mixout(lat, cx, cx_block, hg_f, hg_b, ml_f, ml_b, hg_gate, ml_gate, norms,
                        w_out[l].astype(BF16), modall, jnp.stack([ln_g[l, 0], ln_b[l, 0]]), n_lat_blocks,
                        n_lat_blocks if last else n_blocks,
                        router=None if dense else (router_w[jj], router_b[jj]))
        ln1 = jnp.stack([ln_g[l, 1], ln_b[l, 1]])
        if dense:
            xt = _ffn(mixed[0], modall, ffn_w_gate_up[jj].astype(BF16), ffn_w_down[jj].astype(BF16), ln1,
                      n_lat_blocks)
            lat, cx, cx_block = xt, xt, n_lat_blocks
            out = xt[:, :seq]
        else:
            out = _moe(*mixed[:1], modall, *mixed[1:], moe_w_gate_up[jj], moe_w_down[jj], ln1, seq)
    return out
```

```python
import functools
import math

import numpy as np
import jax
import jax.numpy as jnp
from jax import lax
from jax.experimental import pallas as pl
from jax.experimental.pallas import tpu as pltpu
from jax.experimental.pallas import tpu_sc as plsc

F32 = jnp.float32
BF16 = jnp.bfloat16

D_MODEL = 1024
CTX_LEN = 256
GRID_W = 64
HG_WIDTH = 512
HEADS = 4
HEAD_DIM = 128
ML_DK = 64
CHUNK = 64
D_FF = 2816
N_EXPERTS = 8
D_EXPERT = 1408
DEPTH = 2
DEEPNORM_ALPHA = (2 * DEPTH) ** 0.25
LOG2_E = 1.4426950408889634
LN_EPS = 1e-5
RMS_EPS = 1e-6

TOKEN_BLOCK = 256
EXPERT_TILE = 256
SCAN_ROWS = 4
ML_CHUNK = 128
PROJ_COLS = 8 * 512 + 128
N_LEVELS = 6
VMEM_LIMIT = 56 * 1024 * 1024

NT_DIMS = (((1,), (1,)), ((), ()))
TN_DIMS = (((0,), (0,)), ((), ()))


def _compiler_params(grid_rank):
    return pltpu.CompilerParams(
        dimension_semantics=("arbitrary",) * grid_rank, vmem_limit_bytes=VMEM_LIMIT)


def _dot(a, b):
    return jnp.dot(a, b, preferred_element_type=F32)


def _dot_nt(a, b):
    return lax.dot_general(a, b, NT_DIMS, preferred_element_type=F32)


def _dot_tn(a, b):
    return lax.dot_general(a, b, TN_DIMS, preferred_element_type=F32)


def _sigmoid(z):
    return 0.5 * jnp.tanh(0.5 * z) + 0.5


def _silu(z):
    h = 0.5 * z
    return h + h * jnp.tanh(h)


def _log_sigmoid(z):
    return jnp.minimum(z, 0.0) - jnp.log(1.0 + jnp.exp(-jnp.abs(z)))


def _split3(x):
    hi = x.astype(BF16)
    r = x - hi.astype(F32)
    mid = r.astype(BF16)
    lo = (r - mid.astype(F32)).astype(BF16)
    return hi, mid, lo


def _layer_norm(r, g, b):
    mu = jnp.mean(r, axis=-1, keepdims=True)
    d = r - mu
    var = jnp.mean(d * d, axis=-1, keepdims=True)
    return d * lax.rsqrt(var + LN_EPS) * g + b


def _mod_kernel(c_ref, w_ref, b_ref, o_ref):
    c = c_ref[...]
    s = _silu(c)
    o_ref[0] = jnp.dot(s, w_ref[0], preferred_element_type=F32,
                       precision=lax.Precision.HIGHEST) + b_ref[0]


def _modulation(cond, w_ada, b_ada):
    depth, d, n = w_ada.shape
    rows = cond.shape[0]
    nb = 1536
    return pl.pallas_call(
        _mod_kernel,
        out_shape=jax.ShapeDtypeStruct((depth, rows, n), F32),
        grid=(depth, n // nb),
        in_specs=[
            pl.BlockSpec((rows, d), lambda l, j: (0, 0)),
            pl.BlockSpec((1, d, nb), lambda l, j: (l, 0, j)),
            pl.BlockSpec((1, 1, nb), lambda l, j: (l, 0, j)),
        ],
        out_specs=pl.BlockSpec((1, rows, nb), lambda l, j: (l, 0, j)),
        compiler_params=_compiler_params(2),
        name="adaln_mod",
    )(cond, w_ada, b_ada.reshape(depth, 1, n))


def _stream_specs(lat, ctx, ctx_block, n_lat_blocks):
    d = lat.shape[-1]
    return [pl.BlockSpec((1, TOKEN_BLOCK, d), lambda i, j: (i, jnp.minimum(j, n_lat_blocks - 1), 0)),
            pl.BlockSpec((1, TOKEN_BLOCK, d), lambda i, j: (i, ctx_block, 0))]


def _inproj_kernel(n_lat_blocks, x_ref, c_ref, mod_ref, w_ref, lb_ref, conv_ref, gb_ref,
                   hq_o, lff_o, lfb_o, hv_o, hg_o, mq_o, mk_o, mv_o, mo_o, mg_o):
    j = pl.program_id(1)
    x = jnp.where(j >= n_lat_blocks, c_ref[0], x_ref[0])
    hx = (x * (1.0 + mod_ref[0, 0, 1:2, :]) + mod_ref[0, 0, 0:1, :]).astype(BF16)

    rows = x.shape[0]
    seg_mask = jnp.where(j >= n_lat_blocks, CTX_LEN - 1, GRID_W - 1)
    pos = lax.broadcasted_iota(jnp.int32, (rows, 1), 0) & seg_mask

    def silu_to(out):
        def finish(z):
            out[0] = _silu(z).astype(BF16)
        return finish

    def log_forget_to(out, r):
        def finish(z):
            lb = lb_ref[r:r + 1, :]
            t = jnp.exp(-jnp.abs(z))
            num = jnp.where(z >= 0.0, 1.0 + lb * t, lb + t)
            out[0] = jnp.maximum(jnp.log2(num), jnp.minimum(z, 0.0) * LOG2_E) - jnp.log2(1.0 + t)
        return finish

    def cast_to(out):
        def finish(z):
            out[0] = z.astype(BF16)
        return finish

    def conv_silu_to(q_out, k_out):
        def spread(a):
            gap = jnp.zeros((rows, HEAD_DIM - ML_DK), a.dtype)
            pieces = [p for h in range(HEADS) for p in (a[:, h * ML_DK:(h + 1) * ML_DK], gap)]
            return jnp.concatenate(pieces, axis=1)

        def finish(u):
            taps = conv_ref[...]
            u_prev = jnp.where(pos == 0, 0.0, pltpu.roll(u, 1, axis=0))
            u_next = jnp.where(pos == seg_mask, 0.0, pltpu.roll(u, rows - 1, axis=0))
            qk = _silu(taps[3:4, :] + u_prev * taps[0:1, :] + u * taps[1:2, :] + u_next * taps[2:3, :])
            q_out[0] = spread((qk[:, :HEADS * ML_DK] * (ML_DK ** -0.5)).astype(BF16))
            k_out[0] = spread(qk[:, HEADS * ML_DK:].astype(BF16))
        return finish

    def sigmoid_to(out):
        def finish(z):
            out[0] = _sigmoid(z).astype(BF16)
        return finish

    def gates_to(out):
        def finish(z):
            g = z + gb_ref[...]
            col = lax.broadcasted_iota(jnp.int32, g.shape, 1)
            is_forget = jnp.logical_and((col & 4) != 0, col < 16)
            out[0] = jnp.where(is_forget, _log_sigmoid(g), g)
        return finish

    finishers = [silu_to(hq_o), log_forget_to(lff_o, 0), log_forget_to(lfb_o, 1), cast_to(hv_o),
                 silu_to(hg_o), conv_silu_to(mq_o, mk_o), cast_to(mv_o), sigmoid_to(mo_o), gates_to(mg_o)]
    bounds = [(g * 512, (g + 1) * 512) for g in range(8)] + [(8 * 512, PROJ_COLS)]
    z = _dot(hx, w_ref[:, bounds[0][0]:bounds[0][1]])
    for n, finish in enumerate(finishers):
        z_next = None
        if n + 1 < len(bounds):
            z_next = _dot(hx, w_ref[:, bounds[n + 1][0]:bounds[n + 1][1]])
        finish(z)
        z = z_next


def _inproj(lat, ctx, ctx_block, modall, w_bf, lbp, convp, gb, n_lat_blocks):
    b, _, d = lat.shape
    nblk = n_lat_blocks + CTX_LEN // TOKEN_BLOCK
    t = nblk * TOKEN_BLOCK
    tok = lambda width: pl.BlockSpec((1, TOKEN_BLOCK, width), lambda i, j: (i, j, 0))
    shp = lambda width, dt: jax.ShapeDtypeStruct((b, t, width), dt)
    return pl.pallas_call(
        functools.partial(_inproj_kernel, n_lat_blocks),
        out_shape=(shp(512, BF16), shp(512, F32), shp(512, F32), shp(512, BF16), shp(512, BF16),
                   shp(512, BF16), shp(512, BF16), shp(512, BF16), shp(512, BF16), shp(128, F32)),
        grid=(b, nblk),
        in_specs=_stream_specs(lat, ctx, ctx_block, n_lat_blocks) + [
            pl.BlockSpec((1, 1, 6, d), lambda i, j: (i, jnp.where(j >= n_lat_blocks, 1, 0), 0, 0)),
            pl.BlockSpec((d, PROJ_COLS), lambda i, j: (0, 0), pipeline_mode=pl.Buffered(1)),
            pl.BlockSpec((2, 512), lambda i, j: (0, 0)),
            pl.BlockSpec((4, 512), lambda i, j: (0, 0)),
            pl.BlockSpec((1, 128), lambda i, j: (0, 0)),
        ],
        out_specs=(tok(512), tok(512), tok(512), tok(512), tok(512),
                   tok(512), tok(512), tok(512), tok(512), tok(128)),
        compiler_params=_compiler_params(2),
        name="inproj_features",
    )(lat, ctx, modall, w_bf, lbp, convp, gb)


def _hgrn2_constants():
    c = CHUNK
    amask = np.zeros((N_LEVELS + 1, c, c), np.float32)
    side = np.zeros((3, c, 4 * HEAD_DIM), np.float32)
    for l in range(N_LEVELS):
        s = c >> (l + 1)
        for i in range(c):
            m = (i // (2 * s)) * 2 * s + s
            if i >= m:
                amask[l, i, m - s:m] = 1.0
            if l >= 3:
                side[l - 3, i, :] = 1.0 if i >= m else -1.0
    amask[N_LEVELS] = np.eye(c)
    tri = np.tril(np.ones((c, c), np.float32))
    cum = np.stack([tri, tri.T])
    amask = np.stack([amask, amask[..., ::-1, ::-1]]).reshape(2, (N_LEVELS + 1) * c, c)
    side = np.stack([side, side[:, ::-1]]).reshape(2, 3 * c, 4 * HEAD_DIM)
    return (jnp.asarray(cum, BF16), jnp.asarray(np.ascontiguousarray(side), F32),
            jnp.asarray(np.ascontiguousarray(amask), F32))


def _hgrn2_level_exponents(g_ref, lf, side_ref, d):
    c = CHUNK
    g = g_ref[...]
    width = g.shape[1]

    def split_row(r, n):
        return jnp.broadcast_to(g_ref[pl.ds(r, 1), :], (n, width))

    out = []
    for l in range(3):
        s = c >> (l + 1)
        parts = []
        for r0 in range(0, c, 2 * s):
            lo, hi = g[r0:r0 + s], g[r0 + s:r0 + 2 * s]
            if d == 0:
                gm = split_row(r0 + s - 1, s)
                parts += [gm - lo, hi - gm]
            else:
                gm = split_row(r0 + s, s)
                parts += [lo - gm, gm - hi]
        out.append(jnp.concatenate(parts, axis=0))
    sub = lax.broadcasted_iota(jnp.int32, (8, width), 0)
    for l, s in ((3, 4), (4, 2)):
        tiles = []
        for r0 in range(0, c, 8):
            if s == 4:
                gm = split_row(r0 + (3 if d == 0 else 4), 8)
            else:
                a, b = (1, 5) if d == 0 else (2, 6)
                gm = jnp.where(sub < 4, split_row(r0 + a, 8), split_row(r0 + b, 8))
            tiles.append(gm)
        gm = jnp.concatenate(tiles, axis=0)
        out.append((g - gm) * side_ref[d, (l - 3) * c:(l - 2) * c, :])
    out.append(lf * jnp.maximum(side_ref[d, 2 * c:3 * c, :], 0.0))
    return out


def _hgrn2_kernel(qf_ref, qb_ref, vf_ref, vb_ref, lff_ref, lfb_ref, cum_ref, side_ref, amask_ref,
                  of_ref, ob_ref, st_ref, g_ref):
    c = CHUNK

    @pl.when(pl.program_id(1) == 0)
    def _():
        st_ref[...] = jnp.zeros_like(st_ref)

    dirs = ((qf_ref, vf_ref, lff_ref, of_ref, c - 1), (qb_ref, vb_ref, lfb_ref, ob_ref, 0))
    streams = [(bi, d) + dirs[d] for bi in range(qf_ref.shape[0]) for d in range(2)]
    w = HG_WIDTH
    heads = [slice(h * HEAD_DIM, (h + 1) * HEAD_DIM) for h in range(HEADS)]

    for bi, d, _, _, lf_ref, _, _ in streams:
        sums = _dot(cum_ref[d], jnp.concatenate(_split3(lf_ref[bi]), axis=1))
        g_ref[bi, d] = sums[:, :w] + sums[:, w:2 * w] + sums[:, 2 * w:]

    feats = []
    for bi, d, q_ref, _, lf_ref, _, end_row in streams:
        lf = lf_ref[bi]
        gd_ref = g_ref.at[bi, d]
        g = gd_ref[...]
        g_end = jnp.broadcast_to(gd_ref[pl.ds(end_row, 1), :], (c, w))
        q = q_ref[bi].astype(F32)
        k = 1.0 - jnp.exp2(lf)
        exps = _hgrn2_level_exponents(gd_ref, lf, side_ref, d)
        ts = []
        for l in range(N_LEVELS):
            s = c >> (l + 1)
            if l < 3:
                parts = []
                for r0 in range(0, c, 2 * s):
                    first, second = (k, q) if d == 0 else (q, k)
                    parts += [first[r0:r0 + s], second[r0 + s:r0 + 2 * s]]
                qk = jnp.concatenate(parts, axis=0)
            else:
                qk = jnp.where(side_ref[d, (l - 3) * c:(l - 2) * c, :] > 0.0, q, k)
            ts.append((qk * jnp.exp2(exps[l])).astype(BF16))
        feats.append(dict(
            ts=ts, q_bf=q.astype(BF16), k_bf=k.astype(BF16),
            q_in=(q * jnp.exp2(g)).astype(BF16),
            k_out=(k * jnp.exp2(g_end - g)).astype(BF16),
            total=jnp.exp2(gd_ref[pl.ds(end_row, 1), :])))

    diag_row = N_LEVELS * c
    prods = []
    for (bi, d, *_), f in zip(streams, feats):
        for sl in heads:
            p = [_dot_nt(t[:, sl], t[:, sl]) for t in f["ts"]]
            p.append(_dot_nt(f["q_bf"][:, sl], f["k_bf"][:, sl]))
            prods.append(p)

    intra = []
    for n, (bi, d, *_) in enumerate(streams):
        for h in range(HEADS):
            p = prods[n * HEADS + h]
            a = p[N_LEVELS] * amask_ref[d, diag_row:diag_row + c, :]
            for l in range(N_LEVELS):
                a = a + p[l] * amask_ref[d, l * c:(l + 1) * c, :]
            intra.append(a.astype(BF16))

    for n, ((bi, d, _, v_ref, _, o_ref, _), f) in enumerate(zip(streams, feats)):
        for h, sl in enumerate(heads):
            v = v_ref[bi, :, sl]
            state = st_ref[bi, d, h]
            o = _dot(intra[n * HEADS + h], v) + _dot_nt(f["q_in"][:, sl], state.astype(BF16))
            o_ref[bi, :, sl] = o.astype(BF16)
            st_ref[bi, d, h] = state * f["total"][:, sl] + _dot_tn(v, f["k_out"][:, sl])


def _scan_chunk_maps(n_lat, n_ctx):
    n = n_lat + n_ctx

    def fwd(i, c):
        return (i, jnp.where(c < n_ctx, n_lat + c, c - n_ctx), 0)

    def bwd(i, c):
        return (i, jnp.where(c < n_ctx, n - 1 - c, n - 1 - c), 0)

    return fwd, bwd


def _hgrn2_scan(hq, hv, lff, lfb, consts):
    b, t, w = hq.shape
    n_ctx = CTX_LEN // CHUNK
    n_lat = t // CHUNK - n_ctx
    fwd, bwd = _scan_chunk_maps(n_lat, n_ctx)
    nb = math.gcd(SCAN_ROWS, b)
    blk = lambda m: pl.BlockSpec((nb, CHUNK, w), m)
    const = lambda a: pl.BlockSpec(a.shape, lambda i, c: (0, 0, 0))
    cum, side, amask = consts
    out = jax.ShapeDtypeStruct((b, t, w), BF16)
    return pl.pallas_call(
        _hgrn2_kernel,
        out_shape=(out, out),
        grid=(b // nb, t // CHUNK),
        in_specs=[blk(fwd), blk(bwd), blk(fwd), blk(bwd), blk(fwd), blk(bwd),
                  const(cum), const(side), const(amask)],
        out_specs=(blk(fwd), blk(bwd)),
        scratch_shapes=[pltpu.VMEM((nb, 2, HEADS, HEAD_DIM, HEAD_DIM), F32),
                        pltpu.VMEM((nb, 2, CHUNK, w), F32)],
        compiler_params=_compiler_params(2),
        name="hgrn2_scan",
    )(hq, hq, hv, hv, lff, lfb, cum, side, amask)


def _mlstm_constants():
    c = ML_CHUNK
    tri = np.tril(np.ones((c, c), np.float32))
    cum = np.stack([tri, tri.T])
    negmask = np.where(cum > 0.5, 0.0, -1e30).astype(np.float32)
    sel = np.zeros((2, 3, 128, HEADS, HEAD_DIM), np.float32)
    for d in range(2):
        for h in range(HEADS):
            sel[d, :, 8 * d + h, h, :] = 1.0
    sel = sel.reshape(2, 3 * 128, HEADS * HEAD_DIM)
    return jnp.asarray(cum, BF16), jnp.asarray(negmask, F32), jnp.asarray(sel, BF16)


def _running_max(x, reverse):
    n, lanes = x.shape
    row = lax.broadcasted_iota(jnp.int32, (n, lanes), 0)
    shift = 1
    while shift < n:
        if shift % 8 == 0:
            pad = jnp.full((shift, lanes), -jnp.inf, x.dtype)
            shifted = (jnp.concatenate([x[shift:], pad], axis=0) if reverse
                       else jnp.concatenate([pad, x[:n - shift]], axis=0))
        elif reverse:
            shifted = jnp.where(row >= n - shift, -jnp.inf, pltpu.roll(x, n - shift, axis=0))
        else:
            shifted = jnp.where(row < shift, -jnp.inf, pltpu.roll(x, shift, axis=0))
        x = jnp.maximum(x, shifted)
        shift *= 2
    return x


def _mlstm_kernel(qf_ref, qb_ref, kf_ref, kb_ref, vf_ref, vb_ref, gf_ref, gb_ref,
                  cum_ref, neg_ref, sel_ref, hf_ref, hb_ref, cn_ref, m_ref):
    c = ML_CHUNK

    @pl.when(pl.program_id(1) == 0)
    def _():
        cn_ref[...] = jnp.zeros_like(cn_ref)
        m_ref[...] = jnp.zeros_like(m_ref)

    dirs = ((qf_ref, kf_ref, vf_ref, gf_ref, hf_ref, c - 1), (qb_ref, kb_ref, vb_ref, gb_ref, hb_ref, 0))
    streams = [(bi, d) + dirs[d] for bi in range(qf_ref.shape[0]) for d in range(2)]
    heads = [slice(h * HEAD_DIM, (h + 1) * HEAD_DIM) for h in range(HEADS)]
    lane = lax.broadcasted_iota(jnp.int32, (c, 128), 1)
    cat3 = lambda a: jnp.concatenate(_split3(a), axis=1)

    cums = []
    for bi, d, _, _, _, g_ref, _, _ in streams:
        sums = _dot(cum_ref[d], cat3(g_ref[bi]))
        cums.append(sums[:, :128] + sums[:, 128:256] + sums[:, 256:])

    small = []
    for (bi, d, _, _, _, g_ref, _, end_row), cs in zip(streams, cums):
        valid = jnp.logical_and(lane >= 8 * d, lane < 8 * d + HEADS)
        b_sh = jnp.where(valid, pltpu.roll(cs, 128 - HEADS, axis=1), 0.0)
        u = jnp.where(valid, g_ref[bi], 0.0) - b_sh
        mu = m_ref[bi, d][0:1, :]
        m_run = jnp.maximum(_running_max(u, d == 1), mu)
        m_end = m_run[end_row:end_row + 1, :]
        w_inter = jnp.exp(mu - m_run)
        m_ref[bi, d] = jnp.broadcast_to(b_sh[end_row:end_row + 1, :] + m_end, (8, 128))
        small.append(dict(
            m_run=m_run, w_inter=w_inter.astype(BF16), w_k=jnp.exp(u - m_end).astype(BF16),
            floor=jnp.exp(-(b_sh + m_run)).astype(BF16),
            carry=jnp.broadcast_to(w_inter[end_row:end_row + 1, :], (8, 128)), u_t=u.T))

    wide = []
    for (bi, d, *_), sm in zip(streams, small):
        sel3 = sel_ref[d]
        sel1 = sel_ref[d, 0:128, :]
        wide.append(dict(
            m_run=_dot(cat3(sm["m_run"]), sel3), w_inter=_dot(sm["w_inter"], sel1),
            w_k=_dot(sm["w_k"], sel1), floor=_dot(sm["floor"], sel1),
            carry=_dot(cat3(sm["carry"]), sel3)))

    qk = []
    for bi, d, q_ref, k_ref, *_ in streams:
        qk.append([_dot_nt(q_ref[bi, :, hb], k_ref[bi, :, hb]) for hb in heads])

    ones = jnp.ones((c, HEAD_DIM), BF16)
    zeros = jnp.zeros((HEAD_DIM - ML_DK, 2 * HEAD_DIM), BF16)
    operands = []
    for n, (bi, d, q_ref, _, v_ref, _, _, _) in enumerate(streams):
        for h, hb in enumerate(heads):
            u_row = jnp.broadcast_to(small[n]["u_t"][8 * d + h:8 * d + h + 1, :], (c, c))
            w = jnp.exp(u_row - wide[n]["m_run"][:, hb] + neg_ref[d])
            s = (qk[n][h] * w).astype(BF16)
            q_w = (q_ref[bi, :, hb].astype(F32) * wide[n]["w_inter"][:, hb]).astype(BF16)
            v1 = jnp.concatenate([v_ref[bi, :, hb], ones], axis=1)
            cn = cn_ref[bi, d, h]
            rhs = jnp.concatenate([v1, cn.astype(BF16), zeros], axis=0)
            operands.append((jnp.concatenate([s, q_w], axis=1), rhs, v1, cn))

    for n, (bi, d, _, k_ref, _, _, h_ref, _) in enumerate(streams):
        for h, hb in enumerate(heads):
            lhs, rhs, v1, cn = operands[n * HEADS + h]
            res = _dot(lhs, rhs)
            k_w = (k_ref[bi, :, hb].astype(F32) * wide[n]["w_k"][:, hb]).astype(BF16)
            upd = _dot_tn(k_w, v1)[:ML_DK]
            h_ref[bi, :, hb] = (res[:, :HEAD_DIM] / jnp.maximum(
                jnp.abs(res[:, HEAD_DIM:]), wide[n]["floor"][:, hb])).astype(BF16)
            carry = wide[n]["carry"][0:1, hb]
            cn_ref[bi, d, h] = cn * jnp.concatenate([carry, carry], axis=1) + upd


def _mlstm_scan(mq, mk, mv, mg, consts):
    b, t, w = mq.shape
    n_ctx = CTX_LEN // ML_CHUNK
    n_lat = t // ML_CHUNK - n_ctx
    fwd, bwd = _scan_chunk_maps(n_lat, n_ctx)
    nb = math.gcd(SCAN_ROWS, b)
    blk = lambda width, m: pl.BlockSpec((nb, ML_CHUNK, width), m)
    const = lambda a: pl.BlockSpec(a.shape, lambda i, c: (0, 0, 0))
    cum, negmask, sel = consts
    out = jax.ShapeDtypeStruct((b, t, w), BF16)
    return pl.pallas_call(
        _mlstm_kernel,
        out_shape=(out, out),
        grid=(b // nb, t // ML_CHUNK),
        in_specs=[blk(w, fwd), blk(w, bwd), blk(w, fwd), blk(w, bwd),
                  blk(w, fwd), blk(w, bwd), blk(128, fwd), blk(128, bwd),
                  const(cum), const(negmask), const(sel)],
        out_specs=(blk(w, fwd), blk(w, bwd)),
        scratch_shapes=[pltpu.VMEM((nb, 2, HEADS, ML_DK, 2 * HEAD_DIM), F32),
                        pltpu.VMEM((nb, 2, 8, 128), F32)],
        compiler_params=_compiler_params(2),
        name="mlstm_scan",
    )(mq, mq, mk, mk, mv, mv, mg, mg, cum, negmask, sel)


def _route_block(hx, rwt_ref, rbt_ref, tri_ref, hxa_o, hxb_o, route_o, cnt_o, carry_ref):
    @pl.when(jnp.logical_and(pl.program_id(0) == 0, pl.program_id(1) == 0))
    def _():
        carry_ref[...] = jnp.zeros_like(carry_ref)

    hxa_o[...] = _pack_pairs(hx[:, 0:256], hx[:, 256:512])
    hxb_o[...] = _pack_pairs(hx[:, 512:768], hx[:, 768:1024])
    h_hi, h_mid, h_lo = _split3(hx)
    r_hi, r_mid, _ = _split3(rwt_ref[...])
    r_both = jnp.concatenate([r_hi.astype(F32), r_mid.astype(F32)], axis=0).astype(BF16)
    by_hi = _dot_nt(r_both, h_hi)
    by_mid = _dot_nt(r_both, h_mid)
    by_lo = _dot_nt(r_hi, h_lo)
    logits = ((by_hi[:N_EXPERTS] + by_hi[N_EXPERTS:]) + (by_mid[:N_EXPERTS] + by_mid[N_EXPERTS:])
              + by_lo + rbt_ref[...])
    sub = lax.broadcasted_iota(jnp.int32, logits.shape, 0).astype(F32)
    top1 = jnp.max(logits, axis=0, keepdims=True)
    idx1 = jnp.min(jnp.where(logits == top1, sub, float(N_EXPERTS)), axis=0, keepdims=True)
    rest = jnp.where(sub == idx1, -jnp.inf, logits)
    top2 = jnp.max(rest, axis=0, keepdims=True)
    idx2 = jnp.min(jnp.where(rest == top2, sub, float(N_EXPERTS)), axis=0, keepdims=True)
    p1 = 1.0 / (1.0 + jnp.exp(top2 - top1))
    p2 = 1.0 - p1
    oh1 = jnp.where(sub == idx1, 1.0, 0.0)
    oh2 = jnp.where(sub == idx2, 1.0, 0.0)
    both = oh1 + oh2
    before = _dot(both.astype(BF16), tri_ref[...]) + carry_ref[...]
    rank1 = jnp.sum(before * oh1, axis=0, keepdims=True)
    rank2 = jnp.sum(before * oh2, axis=0, keepdims=True)
    total = carry_ref[...] + jnp.broadcast_to(jnp.sum(both, axis=1, keepdims=True), both.shape)
    carry_ref[...] = total
    cnt_o[...] = total
    route = jnp.zeros_like(logits)
    for name, val in (("e1", idx1), ("e2", idx2), ("rank1", rank1), ("rank2", rank2), ("p1", p1), ("p2", p2)):
        route = jnp.where(sub == float(ROUTE_ROWS[name]), val, route)
    route_o[...] = route


def _mixout_kernel(n_lat_blocks, with_route, x_ref, c_ref, hf_ref, hb_ref, mf_ref, mb_ref, hg_ref, mo_ref,
                   nrm_ref, w_ref, mod_ref, ln_ref, *rest):
    x = jnp.where(pl.program_id(1) >= n_lat_blocks, c_ref[0], x_ref[0])
    parts = []
    streams = ((hf_ref, hb_ref, hg_ref, 0), (mf_ref, mb_ref, mo_ref, 1))
    for f_ref, b_ref, gate_ref, row in streams:
        o = f_ref[0].astype(F32) + b_ref[0].astype(F32)
        gate = gate_ref[0].astype(F32)
        for h in range(HEADS):
            sl = slice(h * HEAD_DIM, (h + 1) * HEAD_DIM)
            oh = o[:, sl]
            ms = jnp.mean(oh * oh, axis=-1, keepdims=True)
            parts.append((oh * lax.rsqrt(ms + RMS_EPS) * nrm_ref[row:row + 1, sl] * gate[:, sl]).astype(BF16))
    y = jnp.concatenate(parts, axis=1)
    mix = _dot(y, w_ref[...])
    r = DEEPNORM_ALPHA * x + mod_ref[0, 0, 2:3, :] * mix
    xn = _layer_norm(r, ln_ref[0:1, :], ln_ref[1:2, :])
    if not with_route:
        (o_ref,) = rest
        o_ref[0] = xn
        return
    rwt_ref, rbt_ref, tri_ref, o_ref, hxa_o, hxb_o, route_o, cnt_o, carry_ref = rest
    o_ref[0] = xn
    hx = xn * (1.0 + mod_ref[0, 0, 4:5, :]) + mod_ref[0, 0, 3:4, :]
    _route_block(hx, rwt_ref, rbt_ref, tri_ref, hxa_o, hxb_o, route_o, cnt_o, carry_ref)


def _mixout(lat, ctx, ctx_block, hg_f, hg_b, ml_f, ml_b, hg_gate, ml_gate, norms, w_bf, modall, ln,
            n_lat_blocks, n_blocks, router=None):
    b, _, d = lat.shape
    tm = TOKEN_BLOCK
    tok = lambda width: pl.BlockSpec((1, tm, width), lambda i, j: (i, j, 0))
    const = lambda shape: pl.BlockSpec(shape, lambda i, j: (0,) * len(shape))
    in_specs = _stream_specs(lat, ctx, ctx_block, n_lat_blocks) + [
        tok(512), tok(512), tok(512), tok(512), tok(512), tok(512),
        const((2, 512)),
        pl.BlockSpec((d, d), lambda i, j: (0, 0), pipeline_mode=pl.Buffered(1)),
        pl.BlockSpec((1, 1, 6, d), lambda i, j: (i, jnp.where(j >= n_lat_blocks, 1, 0), 0, 0)),
        const((2, d)),
    ]
    args = [lat, ctx, hg_f, hg_b, ml_f, ml_b, hg_gate, ml_gate, norms, w_bf, modall, ln]
    out_shape = [jax.ShapeDtypeStruct((b, n_blocks * tm, d), F32)]
    out_specs = [tok(d)]
    scratch = []
    if router is not None:
        assert n_blocks == n_lat_blocks
        rw, rb = router
        n = b * n_blocks * tm
        rows = lambda width: pl.BlockSpec((tm, width), lambda i, j: (i * n_blocks + j, 0))
        in_specs += [const((N_EXPERTS, d)), const((N_EXPERTS, tm)), const((tm, tm))]
        args += [rw.T, jnp.broadcast_to(rb.reshape(N_EXPERTS, 1), (N_EXPERTS, tm)),
                 jnp.asarray(np.triu(np.ones((tm, tm), np.float32), 1), BF16)]
        out_shape += [jax.ShapeDtypeStruct((n, PACK_W), jnp.uint32), jax.ShapeDtypeStruct((n, PACK_W), jnp.uint32),
                      jax.ShapeDtypeStruct((N_EXPERTS, n), F32), jax.ShapeDtypeStruct((N_EXPERTS, tm), F32)]
        out_specs += [rows(PACK_W), rows(PACK_W),
                      pl.BlockSpec((N_EXPERTS, tm), lambda i, j: (0, i * n_blocks + j)), const((N_EXPERTS, tm))]
        scratch = [pltpu.VMEM((N_EXPERTS, tm), F32)]
    return pl.pallas_call(
        functools.partial(_mixout_kernel, n_lat_blocks, router is not None),
        out_shape=tuple(out_shape),
        grid=(b, n_blocks),
        in_specs=in_specs,
        out_specs=tuple(out_specs),
        scratch_shapes=scratch,
        compiler_params=_compiler_params(2),
        name="mix_out_ln_route" if router is not None else "mix_out_ln",
    )(*args)


FF_CHUNK = 256


def _ffn_kernel(x_ref, mod_ref, wgu_ref, wd_ref, ln_ref, o_ref, h_ref):
    x = x_ref[0]
    hx = (x * (1.0 + mod_ref[0, 0, 4:5, :]) + mod_ref[0, 0, 3:4, :]).astype(BF16)
    for c0 in range(0, D_FF, FF_CHUNK):
        g = _dot(hx, wgu_ref[:, c0:c0 + FF_CHUNK])
        u = _dot(hx, wgu_ref[:, D_FF + c0:D_FF + c0 + FF_CHUNK])
        h_ref[:, c0:c0 + FF_CHUNK] = (_silu(g) * u).astype(BF16)
    f = _dot(h_ref[...], wd_ref[...])
    r = DEEPNORM_ALPHA * x + mod_ref[0, 0, 5:6, :] * f
    o_ref[0] = _layer_norm(r, ln_ref[0:1, :], ln_ref[1:2, :])


def _ffn(xt, modall, wgu_bf, wd_bf, ln, n_lat_blocks):
    b, t, d = xt.shape
    tok = pl.BlockSpec((1, TOKEN_BLOCK, d), lambda i, j: (i, j, 0))
    return pl.pallas_call(
        _ffn_kernel,
        out_shape=jax.ShapeDtypeStruct((b, t, d), F32),
        grid=(b, t // TOKEN_BLOCK),
        in_specs=[
            tok,
            pl.BlockSpec((1, 1, 6, d), lambda i, j: (i, jnp.where(j >= n_lat_blocks, 1, 0), 0, 0)),
            pl.BlockSpec((d, 2 * D_FF), lambda i, j: (0, 0), pipeline_mode=pl.Buffered(1)),
            pl.BlockSpec((D_FF, d), lambda i, j: (0, 0), pipeline_mode=pl.Buffered(1)),
            pl.BlockSpec((2, d), lambda i, j: (0, 0)),
        ],
        out_specs=tok,
        scratch_shapes=[pltpu.VMEM((TOKEN_BLOCK, D_FF), BF16)],
        compiler_params=_compiler_params(2),
        name="ffn_dense_ln",
    )(xt, modall, wgu_bf, wd_bf, ln)


EXPERT_CHUNKS = ((0, 256), (256, 256), (512, 256), (768, 256), (1024, 256), (1280, 128))
ROUTE_ROWS = dict(e1=0, e2=1, rank1=2, rank2=3, p1=4, p2=5)
PACK_W = 256
SC_WINDOW = 128


def _pack_pairs(a, b):
    ua = lax.bitcast_convert_type(a.astype(BF16).astype(F32), jnp.uint32)
    ub = lax.bitcast_convert_type(b.astype(BF16).astype(F32), jnp.uint32)
    return (ua >> 16) | ub


def _unpack_pairs(w):
    a = lax.bitcast_convert_type(w << 16, F32)
    b = lax.bitcast_convert_type(w & jnp.uint32(0xFFFF0000), F32)
    return a, b


def _sc_mesh():
    return plsc.VectorSubcoreMesh(core_axis_name="core", subcore_axis_name="subcore")


def _sc_scatter_rows(rows, idx, n_out):
    n_src, width = rows.shape
    m = idx.shape[0]
    src_blocks = n_src // SC_WINDOW

    @functools.partial(pl.kernel, out_type=jax.ShapeDtypeStruct((n_out, width), rows.dtype), mesh=_sc_mesh())
    def scatter(x_hbm, i_hbm, o_hbm):
        def body(x_vmem, i_vmem):
            pltpu.sync_copy(x_vmem, o_hbm.at[i_vmem.at[0]])

        pltpu.emit_pipeline(
            body, grid=(m // SC_WINDOW,),
            in_specs=[pl.BlockSpec((SC_WINDOW, width), lambda i: (i % src_blocks, 0)),
                      pl.BlockSpec((1, SC_WINDOW), lambda i: (0, i))],
            out_specs=[],
            core_axis_name=("core", "subcore"), dimension_semantics=(pltpu.PARALLEL,),
        )(x_hbm, i_hbm)

    return scatter(rows, idx.reshape(1, m))


def _sc_gather_rows(table, idx):
    m = idx.shape[0]
    width = table.shape[1]

    @functools.partial(pl.kernel, out_type=jax.ShapeDtypeStruct((m, width), table.dtype), mesh=_sc_mesh())
    def gather(t_hbm, i_hbm, o_hbm):
        def body(i_vmem, o_vmem):
            pltpu.sync_copy(t_hbm.at[i_vmem.at[0]], o_vmem)

        pltpu.emit_pipeline(
            body, grid=(m // SC_WINDOW,),
            in_specs=[pl.BlockSpec((1, SC_WINDOW), lambda i: (0, i))],
            out_specs=[pl.BlockSpec((SC_WINDOW, width), lambda i: (i, 0))],
            core_axis_name=("core", "subcore"), dimension_semantics=(pltpu.PARALLEL,),
        )(i_hbm, o_hbm)

    return gather(table, idx.reshape(1, m))


CAST_ROWS = 64


def _experts_kernel(te_ref, used_ref, xa_ref, xb_ref, wgu_ref, wd_ref, ya_o, yb_o, h_ref, wgu_bf, wd_bf):
    i = pl.program_id(0)
    changed = jnp.logical_or(i == 0, te_ref[i] != te_ref[jnp.maximum(i - 1, 0)])

    @pl.when(jnp.logical_and(i < used_ref[0], changed))
    def _():
        def cast_gu(r, carry):
            rows = pl.ds(pl.multiple_of(r * CAST_ROWS, CAST_ROWS), CAST_ROWS)
            wgu_bf[rows, :] = wgu_ref[0, rows, :].astype(BF16)
            return carry

        def cast_d(r, carry):
            rows = pl.ds(pl.multiple_of(r * CAST_ROWS, CAST_ROWS), CAST_ROWS)
            wd_bf[rows, :] = wd_ref[0, rows, :].astype(BF16)
            return carry

        lax.fori_loop(0, D_MODEL // CAST_ROWS, cast_gu, 0)
        lax.fori_loop(0, D_EXPERT // CAST_ROWS, cast_d, 0)

    @pl.when(i < used_ref[0])
    def _():
        hx = jnp.concatenate([p.astype(BF16) for w in (xa_ref[...], xb_ref[...]) for p in _unpack_pairs(w)],
                             axis=1)
        for c0, cw in EXPERT_CHUNKS:
            g = _dot(hx, wgu_bf[:, c0:c0 + cw])
            u = _dot(hx, wgu_bf[:, D_EXPERT + c0:D_EXPERT + c0 + cw])
            h_ref[:, c0:c0 + cw] = (_silu(g) * u).astype(BF16)
        f = _dot(h_ref[...], wd_bf[...])
        ya_o[...] = _pack_pairs(f[:, 0:256], f[:, 256:512])
        yb_o[...] = _pack_pairs(f[:, 512:768], f[:, 768:1024])


def _moe_experts(tile_expert, n_used, xa, xb, wgu, wd):
    r = xa.shape[0]
    tm = EXPERT_TILE
    rows = pl.BlockSpec((tm, PACK_W), lambda i, te, nu: (i, 0))
    out = jax.ShapeDtypeStruct((r, PACK_W), jnp.uint32)
    return pl.pallas_call(
        _experts_kernel,
        out_shape=(out, out),
        grid_spec=pltpu.PrefetchScalarGridSpec(
            num_scalar_prefetch=2,
            grid=(r // tm,),
            in_specs=[rows, rows,
                      pl.BlockSpec((1, D_MODEL, 2 * D_EXPERT), lambda i, te, nu: (te[i], 0, 0)),
                      pl.BlockSpec((1, D_EXPERT, D_MODEL), lambda i, te, nu: (te[i], 0, 0))],
            out_specs=(rows, rows),
            scratch_shapes=[pltpu.VMEM((tm, D_EXPERT), BF16),
                            pltpu.VMEM((D_MODEL, 2 * D_EXPERT), BF16),
                            pltpu.VMEM((D_EXPERT, D_MODEL), BF16)]),
        compiler_params=_compiler_params(1),
        name="moe_experts",
    )(tile_expert, n_used, xa, xb, wgu, wd)


def _combine_kernel(x_ref, mod_ref, route_ref, y1a_ref, y1b_ref, y2a_ref, y2b_ref, ln_ref, o_ref):
    route = route_ref[...].T
    f = None
    for ya_ref, yb_ref, name in ((y1a_ref, y1b_ref, "p1"), (y2a_ref, y2b_ref, "p2")):
        col = ROUTE_ROWS[name]
        y = jnp.concatenate(_unpack_pairs(ya_ref[...]) + _unpack_pairs(yb_ref[...]), axis=1)
        term = route[:, col:col + 1] * y
        f = term if f is None else f + term
    r = DEEPNORM_ALPHA * x_ref[0] + mod_ref[0, 0, 5:6, :] * f
    o_ref[0] = _layer_norm(r, ln_ref[0:1, :], ln_ref[1:2, :])


def _moe_combine(xt, modall, route, yga, ygb, ln, seq):
    b, _, d = xt.shape
    tm = TOKEN_BLOCK
    nj = seq // tm
    nblk = b * nj
    first = lambda width: pl.BlockSpec((tm, width), lambda i, j: (i * nj + j, 0))
    second = lambda width: pl.BlockSpec((tm, width), lambda i, j: (nblk + i * nj + j, 0))
    return pl.pallas_call(
        _combine_kernel,
        out_shape=jax.ShapeDtypeStruct((b, seq, d), F32),
        grid=(b, nj),
        in_specs=[
            pl.BlockSpec((1, tm, d), lambda i, j: (i, j, 0)),
            pl.BlockSpec((1, 1, 6, d), lambda i, j: (i, 0, 0, 0)),
            pl.BlockSpec((N_EXPERTS, tm), lambda i, j: (0, i * nj + j)),
            first(PACK_W), first(PACK_W), second(PACK_W), second(PACK_W),
            pl.BlockSpec((2, d), lambda i, j: (0, 0)),
        ],
        out_specs=pl.BlockSpec((1, tm, d), lambda i, j: (i, j, 0)),
        compiler_params=_compiler_params(2),
        name="moe_combine_ln",
    )(xt, modall, route, yga, ygb, yga, ygb, ln)


def _moe(xt, modall, hxa, hxb, route, counts, wgu, wd, ln, seq):
    b = xt.shape[0]
    n = b * seq
    tm = EXPERT_TILE
    counts = counts[:, 0].astype(jnp.int32)
    sizes = (counts + tm - 1) // tm * tm
    ends = jnp.cumsum(sizes)
    starts = ends - sizes
    col = lambda name: route[ROUTE_ROWS[name]].astype(jnp.int32)
    pos = jnp.concatenate([jnp.take(starts, col("e1")) + col("rank1"),
                           jnp.take(starts, col("e2")) + col("rank2")])
    n_rows = 2 * n + N_EXPERTS * tm
    tile_start = jnp.arange(n_rows // tm, dtype=jnp.int32) * tm
    tile_expert = jnp.minimum(jnp.sum(tile_start[:, None] >= ends[None, :], axis=1), N_EXPERTS - 1)
    n_used = (ends[-1:] // tm).astype(jnp.int32)
    xa = _sc_scatter_rows(hxa, pos, n_rows)
    xb = _sc_scatter_rows(hxb, pos, n_rows)
    ya, yb = _moe_experts(tile_expert.astype(jnp.int32), n_used, xa, xb, wgu, wd)
    return _moe_combine(xt, modall, route, _sc_gather_rows(ya, pos), _sc_gather_rows(yb, pos), ln, seq)


def _permute_w_in(w):
    gates = jnp.pad(w[:, 3584:3600], ((0, 0), (0, 112)))
    return jnp.concatenate([w[:, :3584], w[:, 3600:4112], gates], axis=1).astype(BF16)


def kernel(x, c, ctx, c_ctx, w_ada, b_ada, w_in, ml_conv_w, ml_conv_b, hg_lower_bound, ml_gate_bias,
           hg_norm, ml_norm, w_out, ln_g, ln_b, ffn_w_gate_up, ffn_w_down, router_w, router_b,
           moe_w_gate_up, moe_w_down):
    bsz, seq, d = x.shape
    depth = w_ada.shape[0]
    assert depth == DEPTH and d == D_MODEL and ctx.shape[1] == CTX_LEN and seq % TOKEN_BLOCK == 0
    n_lat_blocks = seq // TOKEN_BLOCK
    n_blocks = n_lat_blocks + CTX_LEN // TOKEN_BLOCK

    lb = jnp.cumsum(jax.nn.softmax(hg_lower_bound.astype(F32), axis=0), axis=0)
    lb = lb - lb[0]

    cond = jnp.concatenate([c, c_ctx[None, :], jnp.zeros((16 - bsz - 1, d), F32)], axis=0)
    mod = _modulation(cond, w_ada, b_ada)
    mod = mod.reshape(depth, 16, 6, d)

    hg_consts = _hgrn2_constants()
    ml_consts = _mlstm_constants()

    lat, cx, cx_block = x, ctx, 0
    out = None
    for l in range(depth):
        last = l == depth - 1
        mod_c = jnp.broadcast_to(mod[l, bsz][None], (bsz, 6, d))
        modall = jnp.stack([mod[l, :bsz], mod_c], axis=1)
        lbp = lb[l]
        convp = jnp.concatenate([ml_conv_w[l], ml_conv_b[l][None]], axis=0)
        gb = jnp.pad(ml_gate_bias[l].reshape(1, -1), ((0, 0), (0, 112)))
        feats = _inproj(lat, cx, cx_block, modall, _permute_w_in(w_in[l]), lbp, convp, gb, n_lat_blocks)
        hq, lff, lfb, hv, hg_gate, mq, mk, mv, ml_gate, mg = feats
        hg_f, hg_b = _hgrn2_scan(hq, hv, lff, lfb, hg_consts)
        ml_f, ml_b = _mlstm_scan(mq, mk, mv, mg, ml_consts)
        norms = jnp.stack([hg_norm[l].reshape(-1), ml_norm[l].reshape(-1)])
        jj = l // 2
        dense = l % 2 == 0
        assert dense or last
        mixed = _mixout(lat, cx, cx_block, hg_f, hg_b, ml_f, ml_b, hg_gate, ml_gate, norms,
                        w_out[l].astype(BF16), modall, jnp.stack([ln_g[l, 0], ln_b[l, 0]]), n_lat_blocks,
                        n_lat_blocks if last else n_blocks,
                        router=None if dense else (router_w[jj], router_b[jj]))
        ln1 = jnp.stack([ln_g[l, 1], ln_b[l, 1]])
        if dense:
            xt = _ffn(mixed[0], modall, ffn_w_gate_up[jj].astype(BF16), ffn_w_down[jj].astype(BF16), ln1,
                      n_lat_blocks)
            lat, cx, cx_block = xt, xt, n_lat_blocks
            out = xt[:, :seq]
        else:
            out = _moe(*mixed[:1], modall, *mixed[1:], moe_w_gate_up[jj], moe_w_down[jj], ln1, seq)
    return out
```

```python
import functools
import itertools
import math

import numpy as np
import jax
import jax.numpy as jnp
from jax import lax
from jax.experimental import pallas as pl
from jax.experimental.pallas import tpu as pltpu
from jax.experimental.pallas import tpu_sc as plsc

F32 = jnp.float32
BF16 = jnp.bfloat16

D_MODEL = 1024
CTX_LEN = 256
GRID_W = 64
HG_WIDTH = 512
HEADS = 4
HEAD_DIM = 128
ML_DK = 64
CHUNK = 64
D_FF = 2816
N_EXPERTS = 8
D_EXPERT = 1408
DEPTH = 2
DEEPNORM_ALPHA = (2 * DEPTH) ** 0.25
LOG2_E = 1.4426950408889634
LN_EPS = 1e-5
RMS_EPS = 1e-6

TOKEN_BLOCK = 256
EXPERT_TILE = 512
SCAN_ROWS = 4
ML_CHUNK = 128
PROJ_COLS = 8 * 512 + 128
N_LEVELS = 6
VMEM_LIMIT = 56 * 1024 * 1024

NT_DIMS = (((1,), (1,)), ((), ()))
TN_DIMS = (((0,), (0,)), ((), ()))


def _compiler_params(grid_rank):
    return pltpu.CompilerParams(
        dimension_semantics=("arbitrary",) * grid_rank, vmem_limit_bytes=VMEM_LIMIT)


def _dot(a, b):
    return jnp.dot(a, b, preferred_element_type=F32)


def _dot_nt(a, b):
    return lax.dot_general(a, b, NT_DIMS, preferred_element_type=F32)


def _dot_tn(a, b):
    return lax.dot_general(a, b, TN_DIMS, preferred_element_type=F32)


def _sigmoid(z):
    return 0.5 * jnp.tanh(0.5 * z) + 0.5


def _silu(z):
    h = 0.5 * z
    return h + h * jnp.tanh(h)


def _log_sigmoid(z):
    return jnp.minimum(z, 0.0) - jnp.log(1.0 + jnp.exp(-jnp.abs(z)))


def _split3(x):
    hi = x.astype(BF16)
    r = x - hi.astype(F32)
    mid = r.astype(BF16)
    lo = (r - mid.astype(F32)).astype(BF16)
    return hi, mid, lo


def _layer_norm(r, g, b):
    mu = jnp.mean(r, axis=-1, keepdims=True)
    d = r - mu
    var = jnp.mean(d * d, axis=-1, keepdims=True)
    return d * lax.rsqrt(var + LN_EPS) * g + b


def _mod_kernel(c_ref, w_ref, b_ref, o_ref):
    c = c_ref[...]
    s = _silu(c)
    o_ref[0] = jnp.dot(s, w_ref[0], preferred_element_type=F32,
                       precision=lax.Precision.HIGHEST) + b_ref[0]


def _modulation(cond, w_ada, b_ada):
    depth, d, n = w_ada.shape
    rows = cond.shape[0]
    nb = 1536
    return pl.pallas_call(
        _mod_kernel,
        out_shape=jax.ShapeDtypeStruct((depth, rows, n), F32),
        grid=(depth, n // nb),
        in_specs=[
            pl.BlockSpec((rows, d), lambda l, j: (0, 0)),
            pl.BlockSpec((1, d, nb), lambda l, j: (l, 0, j)),
            pl.BlockSpec((1, 1, nb), lambda l, j: (l, 0, j)),
        ],
        out_specs=pl.BlockSpec((1, rows, nb), lambda l, j: (l, 0, j)),
        compiler_params=_compiler_params(2),
        name="adaln_mod",
    )(cond, w_ada, b_ada.reshape(depth, 1, n))


def _stream_specs(lat, ctx, ctx_block, n_lat_blocks):
    d = lat.shape[-1]
    return [pl.BlockSpec((1, TOKEN_BLOCK, d), lambda i, j: (i, jnp.minimum(j, n_lat_blocks - 1), 0)),
            pl.BlockSpec((1, TOKEN_BLOCK, d), lambda i, j: (i, ctx_block, 0))]


def _inproj_kernel(n_lat_blocks, x_ref, c_ref, mod_ref, w_ref, lb_ref, conv_ref, gb_ref,
                   hq_o, lff_o, lfb_o, hv_o, hg_o, mq_o, mk_o, mv_o, mo_o, mg_o):
    j = pl.program_id(1)
    x = jnp.where(j >= n_lat_blocks, c_ref[0], x_ref[0])
    hx = (x * (1.0 + mod_ref[0, 0, 1:2, :]) + mod_ref[0, 0, 0:1, :]).astype(BF16)

    rows = x.shape[0]
    seg_mask = jnp.where(j >= n_lat_blocks, CTX_LEN - 1, GRID_W - 1)
    pos = lax.broadcasted_iota(jnp.int32, (rows, 1), 0) & seg_mask

    def silu_to(out):
        def finish(z):
            out[0] = _silu(z).astype(BF16)
        return finish

    def log_forget_to(out, r):
        def finish(z):
            lb = lb_ref[r:r + 1, :]
            t = jnp.exp(-jnp.abs(z))
            num = jnp.where(z >= 0.0, 1.0 + lb * t, lb + t)
            out[0] = jnp.maximum(jnp.log2(num), jnp.minimum(z, 0.0) * LOG2_E) - jnp.log2(1.0 + t)
        return finish

    def cast_to(out):
        def finish(z):
            out[0] = z.astype(BF16)
        return finish

    def conv_silu_to(q_out, k_out):
        def spread(a):
            gap = jnp.zeros((rows, HEAD_DIM - ML_DK), a.dtype)
            pieces = [p for h in range(HEADS) for p in (a[:, h * ML_DK:(h + 1) * ML_DK], gap)]
            return jnp.concatenate(pieces, axis=1)

        def finish(u):
            taps = conv_ref[...]
            u_prev = jnp.where(pos == 0, 0.0, pltpu.roll(u, 1, axis=0))
            u_next = jnp.where(pos == seg_mask, 0.0, pltpu.roll(u, rows - 1, axis=0))
            qk = _silu(taps[3:4, :] + u_prev * taps[0:1, :] + u * taps[1:2, :] + u_next * taps[2:3, :])
            q_out[0] = spread((qk[:, :HEADS * ML_DK] * (ML_DK ** -0.5)).astype(BF16))
            k_out[0] = spread(qk[:, HEADS * ML_DK:].astype(BF16))
        return finish

    def sigmoid_to(out):
        def finish(z):
            out[0] = _sigmoid(z).astype(BF16)
        return finish

    def gates_to(out):
        def finish(z):
            g = z + gb_ref[...]
            col = lax.broadcasted_iota(jnp.int32, g.shape, 1)
            is_forget = jnp.logical_and((col & 4) != 0, col < 16)
            out[0] = jnp.where(is_forget, _log_sigmoid(g), g)
        return finish

    finishers = [silu_to(hq_o), log_forget_to(lff_o, 0), log_forget_to(lfb_o, 1), cast_to(hv_o),
                 silu_to(hg_o), conv_silu_to(mq_o, mk_o), cast_to(mv_o), sigmoid_to(mo_o), gates_to(mg_o)]
    bounds = [(g * 512, (g + 1) * 512) for g in range(8)] + [(8 * 512, PROJ_COLS)]
    z = _dot(hx, w_ref[:, bounds[0][0]:bounds[0][1]])
    for n, finish in enumerate(finishers):
        z_next = None
        if n + 1 < len(bounds):
            z_next = _dot(hx, w_ref[:, bounds[n + 1][0]:bounds[n + 1][1]])
        finish(z)
        z = z_next


def _inproj(lat, ctx, ctx_block, modall, w_bf, lbp, convp, gb, n_lat_blocks):
    b, _, d = lat.shape
    nblk = n_lat_blocks + CTX_LEN // TOKEN_BLOCK
    t = nblk * TOKEN_BLOCK
    tok = lambda width: pl.BlockSpec((1, TOKEN_BLOCK, width), lambda i, j: (i, j, 0))
    shp = lambda width, dt: jax.ShapeDtypeStruct((b, t, width), dt)
    return pl.pallas_call(
        functools.partial(_inproj_kernel, n_lat_blocks),
        out_shape=(shp(512, BF16), shp(512, F32), shp(512, F32), shp(512, BF16), shp(512, BF16),
                   shp(512, BF16), shp(512, BF16), shp(512, BF16), shp(512, BF16), shp(128, F32)),
        grid=(b, nblk),
        in_specs=_stream_specs(lat, ctx, ctx_block, n_lat_blocks) + [
            pl.BlockSpec((1, 1, 6, d), lambda i, j: (i, jnp.where(j >= n_lat_blocks, 1, 0), 0, 0)),
            pl.BlockSpec((d, PROJ_COLS), lambda i, j: (0, 0), pipeline_mode=pl.Buffered(1)),
            pl.BlockSpec((2, 512), lambda i, j: (0, 0)),
            pl.BlockSpec((4, 512), lambda i, j: (0, 0)),
            pl.BlockSpec((1, 128), lambda i, j: (0, 0)),
        ],
        out_specs=(tok(512), tok(512), tok(512), tok(512), tok(512),
                   tok(512), tok(512), tok(512), tok(512), tok(128)),
        compiler_params=_compiler_params(2),
        name="inproj_features",
    )(lat, ctx, modall, w_bf, lbp, convp, gb)


def _hgrn2_constants():
    c = CHUNK
    amask = np.zeros((N_LEVELS + 1, c, c), np.float32)
    side = np.zeros((3, c, 4 * HEAD_DIM), np.float32)
    for l in range(N_LEVELS):
        s = c >> (l + 1)
        for i in range(c):
            m = (i // (2 * s)) * 2 * s + s
            if i >= m:
                amask[l, i, m - s:m] = 1.0
            if l >= 3:
                side[l - 3, i, :] = 1.0 if i >= m else -1.0
    amask[N_LEVELS] = np.eye(c)
    tri = np.tril(np.ones((c, c), np.float32))
    cum = np.stack([tri, tri.T])
    amask = np.stack([amask, amask[..., ::-1, ::-1]]).reshape(2, (N_LEVELS + 1) * c, c)
    side = np.stack([side, side[:, ::-1]]).reshape(2, 3 * c, 4 * HEAD_DIM)
    return (jnp.asarray(cum, BF16), jnp.asarray(np.ascontiguousarray(side), F32),
            jnp.asarray(np.ascontiguousarray(amask), F32))


def _hgrn2_level_exponents(g_ref, lf, side_ref, d):
    c = CHUNK
    g = g_ref[...]
    width = g.shape[1]

    def split_row(r, n):
        return jnp.broadcast_to(g_ref[pl.ds(r, 1), :], (n, width))

    out = []
    for l in range(3):
        s = c >> (l + 1)
        parts = []
        for r0 in range(0, c, 2 * s):
            lo, hi = g[r0:r0 + s], g[r0 + s:r0 + 2 * s]
            if d == 0:
                gm = split_row(r0 + s - 1, s)
                parts += [gm - lo, hi - gm]
            else:
                gm = split_row(r0 + s, s)
                parts += [lo - gm, gm - hi]
        out.append(jnp.concatenate(parts, axis=0))
    sub = lax.broadcasted_iota(jnp.int32, (8, width), 0)
    for l, s in ((3, 4), (4, 2)):
        tiles = []
        for r0 in range(0, c, 8):
            if s == 4:
                gm = split_row(r0 + (3 if d == 0 else 4), 8)
            else:
                a, b = (1, 5) if d == 0 else (2, 6)
                gm = jnp.where(sub < 4, split_row(r0 + a, 8), split_row(r0 + b, 8))
            tiles.append(gm)
        gm = jnp.concatenate(tiles, axis=0)
        out.append((g - gm) * side_ref[d, (l - 3) * c:(l - 2) * c, :])
    out.append(lf * jnp.maximum(side_ref[d, 2 * c:3 * c, :], 0.0))
    return out


def _hgrn2_phases(step, qf_ref, qb_ref, vf_ref, vb_ref, lff_ref, lfb_ref, cum_ref, side_ref, amask_ref,
                  of_ref, ob_ref, st_ref, g_ref):
    c = CHUNK
    rows_of = lambda d: slice(c, 2 * c) if step != d else slice(0, c)
    dirs = ((qf_ref, vf_ref, lff_ref, of_ref, c - 1), (qb_ref, vb_ref, lfb_ref, ob_ref, 0))
    streams = [(bi, d) + dirs[d] for bi in range(qf_ref.shape[0]) for d in range(2)]
    w = HG_WIDTH
    heads = [slice(h * HEAD_DIM, (h + 1) * HEAD_DIM) for h in range(HEADS)]

    for bi, d, _, _, lf_ref, _, _ in streams:
        sums = _dot(cum_ref[d], jnp.concatenate(_split3(lf_ref[bi, rows_of(d), :]), axis=1))
        g_ref[bi, d] = sums[:, :w] + sums[:, w:2 * w] + sums[:, 2 * w:]
    yield

    feats = []
    for bi, d, q_ref, _, lf_ref, _, end_row in streams:
        lf = lf_ref[bi, rows_of(d), :]
        gd_ref = g_ref.at[bi, d]
        g = gd_ref[...]
        g_end = jnp.broadcast_to(gd_ref[pl.ds(end_row, 1), :], (c, w))
        q = q_ref[bi, rows_of(d), :].astype(F32)
        k = 1.0 - jnp.exp2(lf)
        exps = _hgrn2_level_exponents(gd_ref, lf, side_ref, d)
        ts = []
        for l in range(N_LEVELS):
            s = c >> (l + 1)
            if l < 3:
                parts = []
                for r0 in range(0, c, 2 * s):
                    first, second = (k, q) if d == 0 else (q, k)
                    parts += [first[r0:r0 + s], second[r0 + s:r0 + 2 * s]]
                qk = jnp.concatenate(parts, axis=0)
            else:
                qk = jnp.where(side_ref[d, (l - 3) * c:(l - 2) * c, :] > 0.0, q, k)
            ts.append((qk * jnp.exp2(exps[l])).astype(BF16))
        feats.append(dict(
            ts=ts, q_bf=q.astype(BF16), k_bf=k.astype(BF16),
            q_in=(q * jnp.exp2(g)).astype(BF16),
            k_out=(k * jnp.exp2(g_end - g)).astype(BF16),
            total=jnp.exp2(gd_ref[pl.ds(end_row, 1), :])))
    yield

    diag_row = N_LEVELS * c
    prods = []
    for (bi, d, *_), f in zip(streams, feats):
        for sl in heads:
            p = [_dot_nt(t[:, sl], t[:, sl]) for t in f["ts"]]
            p.append(_dot_nt(f["q_bf"][:, sl], f["k_bf"][:, sl]))
            prods.append(p)
    yield

    intra = []
    for n, (bi, d, *_) in enumerate(streams):
        for h in range(HEADS):
            p = prods[n * HEADS + h]
            a = p[N_LEVELS] * amask_ref[d, diag_row:diag_row + c, :]
            for l in range(N_LEVELS):
                a = a + p[l] * amask_ref[d, l * c:(l + 1) * c, :]
            intra.append(a.astype(BF16))
    yield

    for n, ((bi, d, _, v_ref, _, o_ref, _), f) in enumerate(zip(streams, feats)):
        for h, sl in enumerate(heads):
            v = v_ref[bi, rows_of(d), sl]
            state = st_ref[bi, d, h]
            o = _dot(intra[n * HEADS + h], v) + _dot_nt(f["q_in"][:, sl], state.astype(BF16))
            o_ref[bi, rows_of(d), sl] = o.astype(BF16)
            st_ref[bi, d, h] = state * f["total"][:, sl] + _dot_tn(v, f["k_out"][:, sl])
    yield


def _scan_chunk_maps(n_lat, n_ctx):
    n = n_lat + n_ctx

    def fwd(i, c):
        return (i, jnp.where(c < n_ctx, n_lat + c, c - n_ctx), 0)

    def bwd(i, c):
        return (i, jnp.where(c < n_ctx, n - 1 - c, n - 1 - c), 0)

    return fwd, bwd


def _mlstm_constants():
    c = ML_CHUNK
    tri = np.tril(np.ones((c, c), np.float32))
    cum = np.stack([tri, tri.T])
    negmask = np.where(cum > 0.5, 0.0, -1e30).astype(np.float32)
    sel = np.zeros((2, 3, 128, HEADS, HEAD_DIM), np.float32)
    for d in range(2):
        for h in range(HEADS):
            sel[d, :, 8 * d + h, h, :] = 1.0
    sel = sel.reshape(2, 3 * 128, HEADS * HEAD_DIM)
    return jnp.asarray(cum, BF16), jnp.asarray(negmask, F32), jnp.asarray(sel, BF16)


def _running_max(x, reverse):
    n, lanes = x.shape
    row = lax.broadcasted_iota(jnp.int32, (n, lanes), 0)
    shift = 1
    while shift < n:
        if shift % 8 == 0:
            pad = jnp.full((shift, lanes), -jnp.inf, x.dtype)
            shifted = (jnp.concatenate([x[shift:], pad], axis=0) if reverse
                       else jnp.concatenate([pad, x[:n - shift]], axis=0))
        elif reverse:
            shifted = jnp.where(row >= n - shift, -jnp.inf, pltpu.roll(x, n - shift, axis=0))
        else:
            shifted = jnp.where(row < shift, -jnp.inf, pltpu.roll(x, shift, axis=0))
        x = jnp.maximum(x, shifted)
        shift *= 2
    return x


def _mlstm_phases(qf_ref, qb_ref, kf_ref, kb_ref, vf_ref, vb_ref, gf_ref, gb_ref,
                  cum_ref, neg_ref, sel_ref, hf_ref, hb_ref, cn_ref, m_ref):
    c = ML_CHUNK
    dirs = ((qf_ref, kf_ref, vf_ref, gf_ref, hf_ref, c - 1), (qb_ref, kb_ref, vb_ref, gb_ref, hb_ref, 0))
    streams = [(bi, d) + dirs[d] for bi in range(qf_ref.shape[0]) for d in range(2)]
    heads = [slice(h * HEAD_DIM, (h + 1) * HEAD_DIM) for h in range(HEADS)]
    lane = lax.broadcasted_iota(jnp.int32, (c, 128), 1)
    cat3 = lambda a: jnp.concatenate(_split3(a), axis=1)

    cums = []
    for bi, d, _, _, _, g_ref, _, _ in streams:
        sums = _dot(cum_ref[d], cat3(g_ref[bi]))
        cums.append(sums[:, :128] + sums[:, 128:256] + sums[:, 256:])
    yield

    small = []
    for (bi, d, _, _, _, g_ref, _, end_row), cs in zip(streams, cums):
        valid = jnp.logical_and(lane >= 8 * d, lane < 8 * d + HEADS)
        b_sh = jnp.where(valid, pltpu.roll(cs, 128 - HEADS, axis=1), 0.0)
        u = jnp.where(valid, g_ref[bi], 0.0) - b_sh
        mu = m_ref[bi, d][0:1, :]
        m_run = jnp.maximum(_running_max(u, d == 1), mu)
        m_end = m_run[end_row:end_row + 1, :]
        w_inter = jnp.exp(mu - m_run)
        m_ref[bi, d] = jnp.broadcast_to(b_sh[end_row:end_row + 1, :] + m_end, (8, 128))
        small.append(dict(
            m_run=m_run, w_inter=w_inter.astype(BF16), w_k=jnp.exp(u - m_end).astype(BF16),
            floor=jnp.exp(-(b_sh + m_run)).astype(BF16),
            carry=jnp.broadcast_to(w_inter[end_row:end_row + 1, :], (8, 128)), u_t=u.T))
    yield

    wide = []
    for (bi, d, *_), sm in zip(streams, small):
        sel3 = sel_ref[d]
        sel1 = sel_ref[d, 0:128, :]
        wide.append(dict(
            m_run=_dot(cat3(sm["m_run"]), sel3), w_inter=_dot(sm["w_inter"], sel1),
            w_k=_dot(sm["w_k"], sel1), floor=_dot(sm["floor"], sel1),
            carry=_dot(cat3(sm["carry"]), sel3)))

    qk = []
    for bi, d, q_ref, k_ref, *_ in streams:
        qk.append([_dot_nt(q_ref[bi, :, hb], k_ref[bi, :, hb]) for hb in heads])
    yield

    ones = jnp.ones((c, HEAD_DIM), BF16)
    zeros = jnp.zeros((HEAD_DIM - ML_DK, 2 * HEAD_DIM), BF16)
    operands = []
    for n, (bi, d, q_ref, _, v_ref, _, _, _) in enumerate(streams):
        for h, hb in enumerate(heads):
            u_row = jnp.broadcast_to(small[n]["u_t"][8 * d + h:8 * d + h + 1, :], (c, c))
            w = jnp.exp(u_row - wide[n]["m_run"][:, hb] + neg_ref[d])
            s = (qk[n][h] * w).astype(BF16)
            q_w = (q_ref[bi, :, hb].astype(F32) * wide[n]["w_inter"][:, hb]).astype(BF16)
            v1 = jnp.concatenate([v_ref[bi, :, hb], ones], axis=1)
            cn = cn_ref[bi, d, h]
            rhs = jnp.concatenate([v1, cn.astype(BF16), zeros], axis=0)
            operands.append((jnp.concatenate([s, q_w], axis=1), rhs, v1, cn))
    yield

    for n, (bi, d, _, k_ref, _, _, h_ref, _) in enumerate(streams):
        for h, hb in enumerate(heads):
            lhs, rhs, v1, cn = operands[n * HEADS + h]
            res = _dot(lhs, rhs)
            k_w = (k_ref[bi, :, hb].astype(F32) * wide[n]["w_k"][:, hb]).astype(BF16)
            upd = _dot_tn(k_w, v1)[:ML_DK]
            h_ref[bi, :, hb] = (res[:, :HEAD_DIM] / jnp.maximum(
                jnp.abs(res[:, HEAD_DIM:]), wide[n]["floor"][:, hb])).astype(BF16)
            carry = wide[n]["carry"][0:1, hb]
            cn_ref[bi, d, h] = cn * jnp.concatenate([carry, carry], axis=1) + upd
    yield


def _scans_kernel(hq_f, hq_b, hv_f, hv_b, lf_f, lf_b, mq_f, mq_b, mk_f, mk_b, mv_f, mv_b, mg_f, mg_b,
                  hg_cum, hg_side, hg_amask, ml_cum, ml_neg, ml_sel,
                  hgo_f, hgo_b, mlo_f, mlo_b, st_ref, g_ref, cn_ref, m_ref):
    @pl.when(pl.program_id(1) == 0)
    def _():
        st_ref[...] = jnp.zeros_like(st_ref)
        cn_ref[...] = jnp.zeros_like(cn_ref)
        m_ref[...] = jnp.zeros_like(m_ref)

    hg_args = (hq_f, hq_b, hv_f, hv_b, lf_f, lf_b, hg_cum, hg_side, hg_amask, hgo_f, hgo_b, st_ref, g_ref)
    hgrn2 = itertools.chain(_hgrn2_phases(0, *hg_args), _hgrn2_phases(1, *hg_args))
    mlstm = _mlstm_phases(mq_f, mq_b, mk_f, mk_b, mv_f, mv_b, mg_f, mg_b, ml_cum, ml_neg, ml_sel,
                          mlo_f, mlo_b, cn_ref, m_ref)
    exhausted = object()
    progressed = True
    while progressed:
        progressed = False
        for phases, count in ((hgrn2, 2), (mlstm, 1)):
            for _ in range(count):
                progressed = (next(phases, exhausted) is not exhausted) or progressed


def _scans(hq, hv, lff, lfb, mq, mk, mv, mg, hg_consts, ml_consts):
    b, t, w = hq.shape
    n_ctx = CTX_LEN // ML_CHUNK
    n_lat = t // ML_CHUNK - n_ctx
    fwd, bwd = _scan_chunk_maps(n_lat, n_ctx)
    nb = math.gcd(SCAN_ROWS, b)
    blk = lambda width, m: pl.BlockSpec((nb, ML_CHUNK, width), m)
    both = lambda width: [blk(width, fwd), blk(width, bwd)]
    const = lambda a: pl.BlockSpec(a.shape, lambda i, c: (0, 0, 0))
    consts = list(hg_consts) + list(ml_consts)
    out = jax.ShapeDtypeStruct((b, t, w), BF16)
    return pl.pallas_call(
        _scans_kernel,
        out_shape=(out, out, out, out),
        grid=(b // nb, t // ML_CHUNK),
        in_specs=(both(w) + both(w) + [blk(w, fwd), blk(w, bwd)] + both(w) + both(w) + both(w) + both(128)
                  + [const(a) for a in consts]),
        out_specs=tuple(both(w) + both(w)),
        scratch_shapes=[pltpu.VMEM((nb, 2, HEADS, HEAD_DIM, HEAD_DIM), F32),
                        pltpu.VMEM((nb, 2, CHUNK, w), F32),
                        pltpu.VMEM((nb, 2, HEADS, ML_DK, 2 * HEAD_DIM), F32),
                        pltpu.VMEM((nb, 2, 8, 128), F32)],
        compiler_params=_compiler_params(2),
        name="scans",
    )(hq, hq, hv, hv, lff, lfb, mq, mq, mk, mk, mv, mv, mg, mg, *consts)


def _route_block(hx, rwt_ref, rbt_ref, tri_ref, hxa_o, hxb_o, route_o, cnt_o, carry_ref):
    @pl.when(jnp.logical_and(pl.program_id(0) == 0, pl.program_id(1) == 0))
    def _():
        carry_ref[...] = jnp.zeros_like(carry_ref)

    hxa_o[...] = _pack_pairs(hx[:, 0:256], hx[:, 256:512])
    hxb_o[...] = _pack_pairs(hx[:, 512:768], hx[:, 768:1024])
    h_hi, h_mid, h_lo = _split3(hx)
    r_hi, r_mid, _ = _split3(rwt_ref[...])
    r_both = jnp.concatenate([r_hi.astype(F32), r_mid.astype(F32)], axis=0).astype(BF16)
    by_hi = _dot_nt(r_both, h_hi)
    by_mid = _dot_nt(r_both, h_mid)
    by_lo = _dot_nt(r_hi, h_lo)
    logits = ((by_hi[:N_EXPERTS] + by_hi[N_EXPERTS:]) + (by_mid[:N_EXPERTS] + by_mid[N_EXPERTS:])
              + by_lo + rbt_ref[...])
    sub = lax.broadcasted_iota(jnp.int32, logits.shape, 0).astype(F32)
    top1 = jnp.max(logits, axis=0, keepdims=True)
    idx1 = jnp.min(jnp.where(logits == top1, sub, float(N_EXPERTS)), axis=0, keepdims=True)
    rest = jnp.where(sub == idx1, -jnp.inf, logits)
    top2 = jnp.max(rest, axis=0, keepdims=True)
    idx2 = jnp.min(jnp.where(rest == top2, sub, float(N_EXPERTS)), axis=0, keepdims=True)
    p1 = 1.0 / (1.0 + jnp.exp(top2 - top1))
    p2 = 1.0 - p1
    oh1 = jnp.where(sub == idx1, 1.0, 0.0)
    oh2 = jnp.where(sub == idx2, 1.0, 0.0)
    both = oh1 + oh2
    before = _dot(both.astype(BF16), tri_ref[...]) + carry_ref[...]
    rank1 = jnp.sum(before * oh1, axis=0, keepdims=True)
    rank2 = jnp.sum(before * oh2, axis=0, keepdims=True)
    total = carry_ref[...] + jnp.broadcast_to(jnp.sum(both, axis=1, keepdims=True), both.shape)
    carry_ref[...] = total
    cnt_o[...] = total
    route = jnp.zeros_like(logits)
    for name, val in (("e1", idx1), ("e2", idx2), ("rank1", rank1), ("rank2", rank2), ("p1", p1), ("p2", p2)):
        route = jnp.where(sub == float(ROUTE_ROWS[name]), val, route)
    route_o[...] = route


def _mixout_kernel(n_lat_blocks, with_route, x_ref, c_ref, hf_ref, hb_ref, mf_ref, mb_ref, hg_ref, mo_ref,
                   nrm_ref, w_ref, mod_ref, ln_ref, *rest):
    x = jnp.where(pl.program_id(1) >= n_lat_blocks, c_ref[0], x_ref[0])
    parts = []
    streams = ((hf_ref, hb_ref, hg_ref, 0), (mf_ref, mb_ref, mo_ref, 1))
    for f_ref, b_ref, gate_ref, row in streams:
        o = f_ref[0].astype(F32) + b_ref[0].astype(F32)
        gate = gate_ref[0].astype(F32)
        for h in range(HEADS):
            sl = slice(h * HEAD_DIM, (h + 1) * HEAD_DIM)
            oh = o[:, sl]
            ms = jnp.mean(oh * oh, axis=-1, keepdims=True)
            parts.append((oh * lax.rsqrt(ms + RMS_EPS) * nrm_ref[row:row + 1, sl] * gate[:, sl]).astype(BF16))
    y = jnp.concatenate(parts, axis=1)
    mix = _dot(y, w_ref[...])
    r = DEEPNORM_ALPHA * x + mod_ref[0, 0, 2:3, :] * mix
    xn = _layer_norm(r, ln_ref[0:1, :], ln_ref[1:2, :])
    if not with_route:
        (o_ref,) = rest
        o_ref[0] = xn
        return
    rwt_ref, rbt_ref, tri_ref, o_ref, hxa_o, hxb_o, route_o, cnt_o, carry_ref = rest
    o_ref[0] = xn
    hx = xn * (1.0 + mod_ref[0, 0, 4:5, :]) + mod_ref[0, 0, 3:4, :]
    _route_block(hx, rwt_ref, rbt_ref, tri_ref, hxa_o, hxb_o, route_o, cnt_o, carry_ref)


def _mixout(lat, ctx, ctx_block, hg_f, hg_b, ml_f, ml_b, hg_gate, ml_gate, norms, w_bf, modall, ln,
            n_lat_blocks, n_blocks, router=None):
    b, _, d = lat.shape
    tm = TOKEN_BLOCK
    tok = lambda width: pl.BlockSpec((1, tm, width), lambda i, j: (i, j, 0))
    const = lambda shape: pl.BlockSpec(shape, lambda i, j: (0,) * len(shape))
    in_specs = _stream_specs(lat, ctx, ctx_block, n_lat_blocks) + [
        tok(512), tok(512), tok(512), tok(512), tok(512), tok(512),
        const((2, 512)),
        pl.BlockSpec((d, d), lambda i, j: (0, 0), pipeline_mode=pl.Buffered(1)),
        pl.BlockSpec((1, 1, 6, d), lambda i, j: (i, jnp.where(j >= n_lat_blocks, 1, 0), 0, 0)),
        const((2, d)),
    ]
    args = [lat, ctx, hg_f, hg_b, ml_f, ml_b, hg_gate, ml_gate, norms, w_bf, modall, ln]
    out_shape = [jax.ShapeDtypeStruct((b, n_blocks * tm, d), F32)]
    out_specs = [tok(d)]
    scratch = []
    if router is not None:
        assert n_blocks == n_lat_blocks
        rw, rb = router
        n = b * n_blocks * tm
        rows = lambda width: pl.BlockSpec((tm, width), lambda i, j: (i * n_blocks + j, 0))
        in_specs += [const((N_EXPERTS, d)), const((N_EXPERTS, tm)), const((tm, tm))]
        args += [rw.T, jnp.broadcast_to(rb.reshape(N_EXPERTS, 1), (N_EXPERTS, tm)),
                 jnp.asarray(np.triu(np.ones((tm, tm), np.float32), 1), BF16)]
        out_shape += [jax.ShapeDtypeStruct((n, PACK_W), jnp.uint32), jax.ShapeDtypeStruct((n, PACK_W), jnp.uint32),
                      jax.ShapeDtypeStruct((N_EXPERTS, n), F32), jax.ShapeDtypeStruct((N_EXPERTS, tm), F32)]
        out_specs += [rows(PACK_W), rows(PACK_W),
                      pl.BlockSpec((N_EXPERTS, tm), lambda i, j: (0, i * n_blocks + j)), const((N_EXPERTS, tm))]
        scratch = [pltpu.VMEM((N_EXPERTS, tm), F32)]
    return pl.pallas_call(
        functools.partial(_mixout_kernel, n_lat_blocks, router is not None),
        out_shape=tuple(out_shape),
        grid=(b, n_blocks),
        in_specs=in_specs,
        out_specs=tuple(out_specs),
        scratch_shapes=scratch,
        compiler_params=_compiler_params(2),
        name="mix_out_ln_route" if router is not None else "mix_out_ln",
    )(*args)


FF_CHUNK = 256


def _ffn_kernel(x_ref, mod_ref, wgu_ref, wd_ref, ln_ref, o_ref, h_ref):
    x = x_ref[0]
    hx = (x * (1.0 + mod_ref[0, 0, 4:5, :]) + mod_ref[0, 0, 3:4, :]).astype(BF16)
    for c0 in range(0, D_FF, FF_CHUNK):
        g = _dot(hx, wgu_ref[:, c0:c0 + FF_CHUNK])
        u = _dot(hx, wgu_ref[:, D_FF + c0:D_FF + c0 + FF_CHUNK])
        h_ref[:, c0:c0 + FF_CHUNK] = (_silu(g) * u).astype(BF16)
    f = _dot(h_ref[...], wd_ref[...])
    r = DEEPNORM_ALPHA * x + mod_ref[0, 0, 5:6, :] * f
    o_ref[0] = _layer_norm(r, ln_ref[0:1, :], ln_ref[1:2, :])


def _ffn(xt, modall, wgu_bf, wd_bf, ln, n_lat_blocks):
    b, t, d = xt.shape
    tok = pl.BlockSpec((1, TOKEN_BLOCK, d), lambda i, j: (i, j, 0))
    return pl.pallas_call(
        _ffn_kernel,
        out_shape=jax.ShapeDtypeStruct((b, t, d), F32),
        grid=(b, t // TOKEN_BLOCK),
        in_specs=[
            tok,
            pl.BlockSpec((1, 1, 6, d), lambda i, j: (i, jnp.where(j >= n_lat_blocks, 1, 0), 0, 0)),
            pl.BlockSpec((d, 2 * D_FF), lambda i, j: (0, 0), pipeline_mode=pl.Buffered(1)),
            pl.BlockSpec((D_FF, d), lambda i, j: (0, 0), pipeline_mode=pl.Buffered(1)),
            pl.BlockSpec((2, d), lambda i, j: (0, 0)),
        ],
        out_specs=tok,
        scratch_shapes=[pltpu.VMEM((TOKEN_BLOCK, D_FF), BF16)],
        compiler_params=_compiler_params(2),
        name="ffn_dense_ln",
    )(xt, modall, wgu_bf, wd_bf, ln)


EXPERT_CHUNKS = ((0, 256), (256, 256), (512, 256), (768, 256), (1024, 256), (1280, 128))
ROUTE_ROWS = dict(e1=0, e2=1, rank1=2, rank2=3, p1=4, p2=5)
PACK_W = 256
SC_WINDOW = 128


def _pack_pairs(a, b):
    ua = lax.bitcast_convert_type(a.astype(BF16).astype(F32), jnp.uint32)
    ub = lax.bitcast_convert_type(b.astype(BF16).astype(F32), jnp.uint32)
    return (ua >> 16) | ub


def _unpack_pairs(w):
    a = lax.bitcast_convert_type(w << 16, F32)
    b = lax.bitcast_convert_type(w & jnp.uint32(0xFFFF0000), F32)
    return a, b


def _sc_mesh():
    return plsc.VectorSubcoreMesh(core_axis_name="core", subcore_axis_name="subcore")


def _sc_scatter_rows(rows, idx, n_out):
    n_src, width = rows.shape
    m = idx.shape[0]
    src_blocks = n_src // SC_WINDOW

    @functools.partial(pl.kernel, out_type=jax.ShapeDtypeStruct((n_out, width), rows.dtype), mesh=_sc_mesh())
    def scatter(x_hbm, i_hbm, o_hbm):
        def body(x_vmem, i_vmem):
            pltpu.sync_copy(x_vmem, o_hbm.at[i_vmem.at[0]])

        pltpu.emit_pipeline(
            body, grid=(m // SC_WINDOW,),
            in_specs=[pl.BlockSpec((SC_WINDOW, width), lambda i: (i % src_blocks, 0)),
                      pl.BlockSpec((1, SC_WINDOW), lambda i: (0, i))],
            out_specs=[],
            core_axis_name=("core", "subcore"), dimension_semantics=(pltpu.PARALLEL,),
        )(x_hbm, i_hbm)

    return scatter(rows, idx.reshape(1, m))


def _sc_gather_rows(table, idx):
    m = idx.shape[0]
    width = table.shape[1]

    @functools.partial(pl.kernel, out_type=jax.ShapeDtypeStruct((m, width), table.dtype), mesh=_sc_mesh())
    def gather(t_hbm, i_hbm, o_hbm):
        def body(i_vmem, o_vmem):
            pltpu.sync_copy(t_hbm.at[i_vmem.at[0]], o_vmem)

        pltpu.emit_pipeline(
            body, grid=(m // SC_WINDOW,),
            in_specs=[pl.BlockSpec((1, SC_WINDOW), lambda i: (0, i))],
            out_specs=[pl.BlockSpec((SC_WINDOW, width), lambda i: (i, 0))],
            core_axis_name=("core", "subcore"), dimension_semantics=(pltpu.PARALLEL,),
        )(i_hbm, o_hbm)

    return gather(table, idx.reshape(1, m))


CAST_ROWS = 64


def _experts_kernel(te_ref, used_ref, xa_ref, xb_ref, wgu_ref, wd_ref, ya_o, yb_o, h_ref, wgu_bf, wd_bf):
    i = pl.program_id(0)
    changed = jnp.logical_or(i == 0, te_ref[i] != te_ref[jnp.maximum(i - 1, 0)])

    @pl.when(jnp.logical_and(i < used_ref[0], changed))
    def _():
        def cast_gu(r, carry):
            rows = pl.ds(pl.multiple_of(r * CAST_ROWS, CAST_ROWS), CAST_ROWS)
            wgu_bf[rows, :] = wgu_ref[0, rows, :].astype(BF16)
            return carry

        def cast_d(r, carry):
            rows = pl.ds(pl.multiple_of(r * CAST_ROWS, CAST_ROWS), CAST_ROWS)
            wd_bf[rows, :] = wd_ref[0, rows, :].astype(BF16)
            return carry

        lax.fori_loop(0, D_MODEL // CAST_ROWS, cast_gu, 0)
        lax.fori_loop(0, D_EXPERT // CAST_ROWS, cast_d, 0)

    @pl.when(i < used_ref[0])
    def _():
        hx = jnp.concatenate([p.astype(BF16) for w in (xa_ref[...], xb_ref[...]) for p in _unpack_pairs(w)],
                             axis=1)
        for c0, cw in EXPERT_CHUNKS:
            g = _dot(hx, wgu_bf[:, c0:c0 + cw])
            u = _dot(hx, wgu_bf[:, D_EXPERT + c0:D_EXPERT + c0 + cw])
            h_ref[:, c0:c0 + cw] = (_silu(g) * u).astype(BF16)
        f = _dot(h_ref[...], wd_bf[...])
        ya_o[...] = _pack_pairs(f[:, 0:256], f[:, 256:512])
        yb_o[...] = _pack_pairs(f[:, 512:768], f[:, 768:1024])


def _moe_experts(tile_expert, n_used, xa, xb, wgu, wd):
    r = xa.shape[0]
    tm = EXPERT_TILE
    rows = pl.BlockSpec((tm, PACK_W), lambda i, te, nu: (i, 0))
    out = jax.ShapeDtypeStruct((r, PACK_W), jnp.uint32)
    return pl.pallas_call(
        _experts_kernel,
        out_shape=(out, out),
        grid_spec=pltpu.PrefetchScalarGridSpec(
            num_scalar_prefetch=2,
            grid=(r // tm,),
            in_specs=[rows, rows,
                      pl.BlockSpec((1, D_MODEL, 2 * D_EXPERT), lambda i, te, nu: (te[i], 0, 0)),
                      pl.BlockSpec((1, D_EXPERT, D_MODEL), lambda i, te, nu: (te[i], 0, 0))],
            out_specs=(rows, rows),
            scratch_shapes=[pltpu.VMEM((tm, D_EXPERT), BF16),
                            pltpu.VMEM((D_MODEL, 2 * D_EXPERT), BF16),
                            pltpu.VMEM((D_EXPERT, D_MODEL), BF16)]),
        compiler_params=_compiler_params(1),
        name="moe_experts",
    )(tile_expert, n_used, xa, xb, wgu, wd)


def _combine_kernel(x_ref, mod_ref, route_ref, y1a_ref, y1b_ref, y2a_ref, y2b_ref, ln_ref, o_ref):
    route = route_ref[...].T
    f = None
    for ya_ref, yb_ref, name in ((y1a_ref, y1b_ref, "p1"), (y2a_ref, y2b_ref, "p2")):
        col = ROUTE_ROWS[name]
        y = jnp.concatenate(_unpack_pairs(ya_ref[...]) + _unpack_pairs(yb_ref[...]), axis=1)
        term = route[:, col:col + 1] * y
        f = term if f is None else f + term
    r = DEEPNORM_ALPHA * x_ref[0] + mod_ref[0, 0, 5:6, :] * f
    o_ref[0] = _layer_norm(r, ln_ref[0:1, :], ln_ref[1:2, :])


def _moe_combine(xt, modall, route, yga, ygb, ln, seq):
    b, _, d = xt.shape
    tm = TOKEN_BLOCK
    nj = seq // tm
    nblk = b * nj
    first = lambda width: pl.BlockSpec((tm, width), lambda i, j: (i * nj + j, 0))
    second = lambda width: pl.BlockSpec((tm, width), lambda i, j: (nblk + i * nj + j, 0))
    return pl.pallas_call(
        _combine_kernel,
        out_shape=jax.ShapeDtypeStruct((b, seq, d), F32),
        grid=(b, nj),
        in_specs=[
            pl.BlockSpec((1, tm, d), lambda i, j: (i, j, 0)),
            pl.BlockSpec((1, 1, 6, d), lambda i, j: (i, 0, 0, 0)),
            pl.BlockSpec((N_EXPERTS, tm), lambda i, j: (0, i * nj + j)),
            first(PACK_W), first(PACK_W), second(PACK_W), second(PACK_W),
            pl.BlockSpec((2, d), lambda i, j: (0, 0)),
        ],
        out_specs=pl.BlockSpec((1, tm, d), lambda i, j: (i, j, 0)),
        compiler_params=_compiler_params(2),
        name="moe_combine_ln",
    )(xt, modall, route, yga, ygb, yga, ygb, ln)


def _moe(xt, modall, hxa, hxb, route, counts, wgu, wd, ln, seq):
    b = xt.shape[0]
    n = b * seq
    tm = EXPERT_TILE
    counts = counts[:, 0].astype(jnp.int32)
    sizes = (counts + tm - 1) // tm * tm
    ends = jnp.cumsum(sizes)
    starts = ends - sizes
    col = lambda name: route[ROUTE_ROWS[name]].astype(jnp.int32)
    pos = jnp.concatenate([jnp.take(starts, col("e1")) + col("rank1"),
                           jnp.take(starts, col("e2")) + col("rank2")])
    n_rows = 2 * n + N_EXPERTS * tm
    tile_start = jnp.arange(n_rows // tm, dtype=jnp.int32) * tm
    tile_expert = jnp.minimum(jnp.sum(tile_start[:, None] >= ends[None, :], axis=1), N_EXPERTS - 1)
    n_used = (ends[-1:] // tm).astype(jnp.int32)
    xa = _sc_scatter_rows(hxa, pos, n_rows)
    xb = _sc_scatter_rows(hxb, pos, n_rows)
    ya, yb = _moe_experts(tile_expert.astype(jnp.int32), n_used, xa, xb, wgu, wd)
    return _moe_combine(xt, modall, route, _sc_gather_rows(ya, pos), _sc_gather_rows(yb, pos), ln, seq)


def _permute_w_in(w):
    gates = jnp.pad(w[:, 3584:3600], ((0, 0), (0, 112)))
    return jnp.concatenate([w[:, :3584], w[:, 3600:4112], gates], axis=1).astype(BF16)


def kernel(x, c, ctx, c_ctx, w_ada, b_ada, w_in, ml_conv_w, ml_conv_b, hg_lower_bound, ml_gate_bias,
           hg_norm, ml_norm, w_out, ln_g, ln_b, ffn_w_gate_up, ffn_w_down, router_w, router_b,
           moe_w_gate_up, moe_w_down):
    bsz, seq, d = x.shape
    depth = w_ada.shape[0]
    assert depth == DEPTH and d == D_MODEL and ctx.shape[1] == CTX_LEN and seq % TOKEN_BLOCK == 0
    n_lat_blocks = seq // TOKEN_BLOCK
    n_blocks = n_lat_blocks + CTX_LEN // TOKEN_BLOCK

    lb = jnp.cumsum(jax.nn.softmax(hg_lower_bound.astype(F32), axis=0), axis=0)
    lb = lb - lb[0]

    cond = jnp.concatenate([c, c_ctx[None, :], jnp.zeros((16 - bsz - 1, d), F32)], axis=0)
    mod = _modulation(cond, w_ada, b_ada)
    mod = mod.reshape(depth, 16, 6, d)

    hg_consts = _hgrn2_constants()
    ml_consts = _mlstm_constants()

    lat, cx, cx_block = x, ctx, 0
    out = None
    for l in range(depth):
        last = l == depth - 1
        mod_c = jnp.broadcast_to(mod[l, bsz][None], (bsz, 6, d))
        modall = jnp.stack([mod[l, :bsz], mod_c], axis=1)
        lbp = lb[l]
        convp = jnp.concatenate([ml_conv_w[l], ml_conv_b[l][None]], axis=0)
        gb = jnp.pad(ml_gate_bias[l].reshape(1, -1), ((0, 0), (0, 112)))
        feats = _inproj(lat, cx, cx_block, modall, _permute_w_in(w_in[l]), lbp, convp, gb, n_lat_blocks)
        hq, lff, lfb, hv, hg_gate, mq, mk, mv, ml_gate, mg = feats
        hg_f, hg_b, ml_f, ml_b = _scans(hq, hv, lff, lfb, mq, mk, mv, mg, hg_consts, ml_consts)
        norms = jnp.stack([hg_norm[l].reshape(-1), ml_norm[l].reshape(-1)])
        jj = l // 2
        dense = l % 2 == 0
        assert dense or last
        mixed = _mixout(lat, cx, cx_block, hg_f, hg_b, ml_f, ml_b, hg_gate, ml_gate, norms,
                        w_out[l].astype(BF16), modall, jnp.stack([ln_g[l, 0], ln_b[l, 0]]), n_lat_blocks,
                        n_lat_blocks if last else n_blocks,
                        router=None if dense else (router_w[jj], router_b[jj]))
        ln1 = jnp.stack([ln_g[l, 1], ln_b[l, 1]])
        if dense:
            xt = _ffn(mixed[0], modall, ffn_w_gate_up[jj].astype(BF16), ffn_w_down[jj].astype(BF16), ln1,
                      n_lat_blocks)
            lat, cx, cx_block = xt, xt, n_lat_blocks
            out = xt[:, :seq]
        else:
            out = _moe(*mixed[:1], modall, *mixed[1:], moe_w_gate_up[jj], moe_w_down[jj], ln1, seq)
    return out
```

```python
import functools
import itertools
import math

import numpy as np
import jax
import jax.numpy as jnp
from jax import lax
from jax.experimental import pallas as pl
from jax.experimental.pallas import tpu as pltpu
from jax.experimental.pallas import tpu_sc as plsc

F32 = jnp.float32
BF16 = jnp.bfloat16

D_MODEL = 1024
CTX_LEN = 256
GRID_W = 64
HG_WIDTH = 512
HEADS = 4
HEAD_DIM = 128
ML_DK = 64
CHUNK = 64
D_FF = 2816
N_EXPERTS = 8
D_EXPERT = 1408
DEPTH = 2
DEEPNORM_ALPHA = (2 * DEPTH) ** 0.25
LOG2_E = 1.4426950408889634
LN_EPS = 1e-5
RMS_EPS = 1e-6

TOKEN_BLOCK = 256
ROUTE_BLOCK = 512
EXPERT_TILE = 512
SCAN_ROWS = 4
ML_CHUNK = 128
PROJ_COLS = 8 * 512 + 128
N_LEVELS = 6
VMEM_LIMIT = 56 * 1024 * 1024

NT_DIMS = (((1,), (1,)), ((), ()))
TN_DIMS = (((0,), (0,)), ((), ()))


def _compiler_params(grid_rank):
    return pltpu.CompilerParams(
        dimension_semantics=("arbitrary",) * grid_rank, vmem_limit_bytes=VMEM_LIMIT)


def _dot(a, b):
    return jnp.dot(a, b, preferred_element_type=F32)


def _dot_nt(a, b):
    return lax.dot_general(a, b, NT_DIMS, preferred_element_type=F32)


def _dot_tn(a, b):
    return lax.dot_general(a, b, TN_DIMS, preferred_element_type=F32)


def _sigmoid(z):
    return 0.5 * jnp.tanh(0.5 * z) + 0.5


def _silu(z):
    h = 0.5 * z
    return h + h * jnp.tanh(h)


def _log_sigmoid(z):
    return jnp.minimum(z, 0.0) - jnp.log(1.0 + jnp.exp(-jnp.abs(z)))


def _split3(x):
    hi = x.astype(BF16)
    r = x - hi.astype(F32)
    mid = r.astype(BF16)
    lo = (r - mid.astype(F32)).astype(BF16)
    return hi, mid, lo


def _layer_norm(r, g, b):
    mu = jnp.mean(r, axis=-1, keepdims=True)
    d = r - mu
    var = jnp.mean(d * d, axis=-1, keepdims=True)
    return d * lax.rsqrt(var + LN_EPS) * g + b


def _mod_kernel(c_ref, w_ref, b_ref, o_ref):
    c = c_ref[...]
    s = _silu(c)
    o_ref[0] = jnp.dot(s, w_ref[0], preferred_element_type=F32,
                       precision=lax.Precision.HIGHEST) + b_ref[0]


def _modulation(cond, w_ada, b_ada):
    depth, d, n = w_ada.shape
    rows = cond.shape[0]
    nb = 1536
    return pl.pallas_call(
        _mod_kernel,
        out_shape=jax.ShapeDtypeStruct((depth, rows, n), F32),
        grid=(depth, n // nb),
        in_specs=[
            pl.BlockSpec((rows, d), lambda l, j: (0, 0)),
            pl.BlockSpec((1, d, nb), lambda l, j: (l, 0, j)),
            pl.BlockSpec((1, 1, nb), lambda l, j: (l, 0, j)),
        ],
        out_specs=pl.BlockSpec((1, rows, nb), lambda l, j: (l, 0, j)),
        compiler_params=_compiler_params(2),
        name="adaln_mod",
    )(cond, w_ada, b_ada.reshape(depth, 1, n))


def _stream_specs(lat, ctx, ctx_block, n_lat_blocks, rows=TOKEN_BLOCK):
    d = lat.shape[-1]
    return [pl.BlockSpec((1, rows, d), lambda i, j: (i, jnp.minimum(j, n_lat_blocks - 1), 0)),
            pl.BlockSpec((1, rows, d), lambda i, j: (i, ctx_block, 0))]


def _inproj_kernel(n_lat_blocks, x_ref, c_ref, mod_ref, w_ref, lb_ref, conv_ref, gb_ref,
                   hq_o, lff_o, lfb_o, hv_o, hg_o, mq_o, mk_o, mv_o, mo_o, mg_o):
    j = pl.program_id(1)
    x = jnp.where(j >= n_lat_blocks, c_ref[0], x_ref[0])
    hx = (x * (1.0 + mod_ref[0, 0, 1:2, :]) + mod_ref[0, 0, 0:1, :]).astype(BF16)

    rows = x.shape[0]
    seg_mask = jnp.where(j >= n_lat_blocks, CTX_LEN - 1, GRID_W - 1)
    pos = lax.broadcasted_iota(jnp.int32, (rows, 1), 0) & seg_mask

    def silu_to(out):
        def finish(z):
            out[0] = _silu(z).astype(BF16)
        return finish

    def log_forget_to(out, r):
        def finish(z):
            lb = lb_ref[r:r + 1, :]
            t = jnp.exp(-jnp.abs(z))
            num = jnp.where(z >= 0.0, 1.0 + lb * t, lb + t)
            out[0] = jnp.maximum(jnp.log2(num), jnp.minimum(z, 0.0) * LOG2_E) - jnp.log2(1.0 + t)
        return finish

    def cast_to(out):
        def finish(z):
            out[0] = z.astype(BF16)
        return finish

    def conv_silu_to(q_out, k_out):
        def spread(a):
            gap = jnp.zeros((rows, HEAD_DIM - ML_DK), a.dtype)
            pieces = [p for h in range(HEADS) for p in (a[:, h * ML_DK:(h + 1) * ML_DK], gap)]
            return jnp.concatenate(pieces, axis=1)

        def finish(u):
            taps = conv_ref[...]
            u_prev = jnp.where(pos == 0, 0.0, pltpu.roll(u, 1, axis=0))
            u_next = jnp.where(pos == seg_mask, 0.0, pltpu.roll(u, rows - 1, axis=0))
            qk = _silu(taps[3:4, :] + u_prev * taps[0:1, :] + u * taps[1:2, :] + u_next * taps[2:3, :])
            q_out[0] = spread((qk[:, :HEADS * ML_DK] * (ML_DK ** -0.5)).astype(BF16))
            k_out[0] = spread(qk[:, HEADS * ML_DK:].astype(BF16))
        return finish

    def sigmoid_to(out):
        def finish(z):
            out[0] = _sigmoid(z).astype(BF16)
        return finish

    def gates_to(out):
        def finish(z):
            g = z + gb_ref[...]
            col = lax.broadcasted_iota(jnp.int32, g.shape, 1)
            is_forget = jnp.logical_and((col & 4) != 0, col < 16)
            out[0] = jnp.where(is_forget, _log_sigmoid(g), g)
        return finish

    finishers = [silu_to(hq_o), log_forget_to(lff_o, 0), log_forget_to(lfb_o, 1), cast_to(hv_o),
                 silu_to(hg_o), conv_silu_to(mq_o, mk_o), cast_to(mv_o), sigmoid_to(mo_o), gates_to(mg_o)]
    bounds = [(g * 512, (g + 1) * 512) for g in range(8)] + [(8 * 512, PROJ_COLS)]
    z = _dot(hx, w_ref[:, bounds[0][0]:bounds[0][1]])
    for n, finish in enumerate(finishers):
        z_next = None
        if n + 1 < len(bounds):
            z_next = _dot(hx, w_ref[:, bounds[n + 1][0]:bounds[n + 1][1]])
        finish(z)
        z = z_next


def _inproj(lat, ctx, ctx_block, modall, w_bf, lbp, convp, gb, n_lat_blocks):
    b, _, d = lat.shape
    nblk = n_lat_blocks + CTX_LEN // TOKEN_BLOCK
    t = nblk * TOKEN_BLOCK
    tok = lambda width: pl.BlockSpec((1, TOKEN_BLOCK, width), lambda i, j: (i, j, 0))
    shp = lambda width, dt: jax.ShapeDtypeStruct((b, t, width), dt)
    return pl.pallas_call(
        functools.partial(_inproj_kernel, n_lat_blocks),
        out_shape=(shp(512, BF16), shp(512, F32), shp(512, F32), shp(512, BF16), shp(512, BF16),
                   shp(512, BF16), shp(512, BF16), shp(512, BF16), shp(512, BF16), shp(128, F32)),
        grid=(b, nblk),
        in_specs=_stream_specs(lat, ctx, ctx_block, n_lat_blocks) + [
            pl.BlockSpec((1, 1, 6, d), lambda i, j: (i, jnp.where(j >= n_lat_blocks, 1, 0), 0, 0)),
            pl.BlockSpec((d, PROJ_COLS), lambda i, j: (0, 0), pipeline_mode=pl.Buffered(1)),
            pl.BlockSpec((2, 512), lambda i, j: (0, 0)),
            pl.BlockSpec((4, 512), lambda i, j: (0, 0)),
            pl.BlockSpec((1, 128), lambda i, j: (0, 0)),
        ],
        out_specs=(tok(512), tok(512), tok(512), tok(512), tok(512),
                   tok(512), tok(512), tok(512), tok(512), tok(128)),
        compiler_params=_compiler_params(2),
        name="inproj_features",
    )(lat, ctx, modall, w_bf, lbp, convp, gb)


def _hgrn2_constants():
    c = CHUNK
    amask = np.zeros((N_LEVELS + 1, c, c), np.float32)
    side = np.zeros((3, c, 4 * HEAD_DIM), np.float32)
    for l in range(N_LEVELS):
        s = c >> (l + 1)
        for i in range(c):
            m = (i // (2 * s)) * 2 * s + s
            if i >= m:
                amask[l, i, m - s:m] = 1.0
            if l >= 3:
                side[l - 3, i, :] = 1.0 if i >= m else -1.0
    amask[N_LEVELS] = np.eye(c)
    tri = np.tril(np.ones((c, c), np.float32))
    cum = np.stack([tri, tri.T])
    amask = np.stack([amask, amask[..., ::-1, ::-1]]).reshape(2, (N_LEVELS + 1) * c, c)
    side = np.stack([side, side[:, ::-1]]).reshape(2, 3 * c, 4 * HEAD_DIM)
    return (jnp.asarray(cum, BF16), jnp.asarray(np.ascontiguousarray(side), F32),
            jnp.asarray(np.ascontiguousarray(amask), F32))


def _hgrn2_level_exponents(g_ref, lf, side_ref, d):
    c = CHUNK
    g = g_ref[...]
    width = g.shape[1]

    def split_row(r, n):
        return jnp.broadcast_to(g_ref[pl.ds(r, 1), :], (n, width))

    out = []
    for l in range(3):
        s = c >> (l + 1)
        parts = []
        for r0 in range(0, c, 2 * s):
            lo, hi = g[r0:r0 + s], g[r0 + s:r0 + 2 * s]
            if d == 0:
                gm = split_row(r0 + s - 1, s)
                parts += [gm - lo, hi - gm]
            else:
                gm = split_row(r0 + s, s)
                parts += [lo - gm, gm - hi]
        out.append(jnp.concatenate(parts, axis=0))
    sub = lax.broadcasted_iota(jnp.int32, (8, width), 0)
    for l, s in ((3, 4), (4, 2)):
        tiles = []
        for r0 in range(0, c, 8):
            if s == 4:
                gm = split_row(r0 + (3 if d == 0 else 4), 8)
            else:
                a, b = (1, 5) if d == 0 else (2, 6)
                gm = jnp.where(sub < 4, split_row(r0 + a, 8), split_row(r0 + b, 8))
            tiles.append(gm)
        gm = jnp.concatenate(tiles, axis=0)
        out.append((g - gm) * side_ref[d, (l - 3) * c:(l - 2) * c, :])
    out.append(lf * jnp.maximum(side_ref[d, 2 * c:3 * c, :], 0.0))
    return out


def _hgrn2_phases(step, qf_ref, qb_ref, vf_ref, vb_ref, lff_ref, lfb_ref, cum_ref, side_ref, amask_ref,
                  of_ref, ob_ref, st_ref, g_ref):
    c = CHUNK
    rows_of = lambda d: slice(c, 2 * c) if step != d else slice(0, c)
    dirs = ((qf_ref, vf_ref, lff_ref, of_ref, c - 1), (qb_ref, vb_ref, lfb_ref, ob_ref, 0))
    streams = [(bi, d) + dirs[d] for bi in range(qf_ref.shape[0]) for d in range(2)]
    w = HG_WIDTH
    heads = [slice(h * HEAD_DIM, (h + 1) * HEAD_DIM) for h in range(HEADS)]

    for bi, d, _, _, lf_ref, _, _ in streams:
        sums = _dot(cum_ref[d], jnp.concatenate(_split3(lf_ref[bi, rows_of(d), :]), axis=1))
        g_ref[bi, d] = sums[:, :w] + sums[:, w:2 * w] + sums[:, 2 * w:]
    yield

    feats = []
    for bi, d, q_ref, _, lf_ref, _, end_row in streams:
        lf = lf_ref[bi, rows_of(d), :]
        gd_ref = g_ref.at[bi, d]
        g = gd_ref[...]
        g_end = jnp.broadcast_to(gd_ref[pl.ds(end_row, 1), :], (c, w))
        q = q_ref[bi, rows_of(d), :].astype(F32)
        k = 1.0 - jnp.exp2(lf)
        exps = _hgrn2_level_exponents(gd_ref, lf, side_ref, d)
        ts = []
        for l in range(N_LEVELS):
            s = c >> (l + 1)
            if l < 3:
                parts = []
                for r0 in range(0, c, 2 * s):
                    first, second = (k, q) if d == 0 else (q, k)
                    parts += [first[r0:r0 + s], second[r0 + s:r0 + 2 * s]]
                qk = jnp.concatenate(parts, axis=0)
            else:
                qk = jnp.where(side_ref[d, (l - 3) * c:(l - 2) * c, :] > 0.0, q, k)
            ts.append((qk * jnp.exp2(exps[l])).astype(BF16))
        feats.append(dict(
            ts=ts, q_bf=q.astype(BF16), k_bf=k.astype(BF16),
            q_in=(q * jnp.exp2(g)).astype(BF16),
            k_out=(k * jnp.exp2(g_end - g)).astype(BF16),
            total=jnp.exp2(gd_ref[pl.ds(end_row, 1), :])))
        yield

    diag_row = N_LEVELS * c
    prods = []
    for (bi, d, *_), f in zip(streams, feats):
        for sl in heads:
            p = [_dot_nt(t[:, sl], t[:, sl]) for t in f["ts"]]
            p.append(_dot_nt(f["q_bf"][:, sl], f["k_bf"][:, sl]))
            prods.append(p)
    yield

    intra = []
    for n, (bi, d, *_) in enumerate(streams):
        for h in range(HEADS):
            p = prods[n * HEADS + h]
            a = p[N_LEVELS] * amask_ref[d, diag_row:diag_row + c, :]
            for l in range(N_LEVELS):
                a = a + p[l] * amask_ref[d, l * c:(l + 1) * c, :]
            intra.append(a.astype(BF16))
    yield

    for n, ((bi, d, _, v_ref, _, o_ref, _), f) in enumerate(zip(streams, feats)):
        for h, sl in enumerate(heads):
            v = v_ref[bi, rows_of(d), sl]
            state = st_ref[bi, d, h]
            o = _dot(intra[n * HEADS + h], v) + _dot_nt(f["q_in"][:, sl], state.astype(BF16))
            o_ref[bi, rows_of(d), sl] = o.astype(BF16)
            st_ref[bi, d, h] = state * f["total"][:, sl] + _dot_tn(v, f["k_out"][:, sl])
    yield


def _scan_chunk_maps(n_lat, n_ctx):
    n = n_lat + n_ctx

    def fwd(i, c):
        return (i, jnp.where(c < n_ctx, n_lat + c, c - n_ctx), 0)

    def bwd(i, c):
        return (i, jnp.where(c < n_ctx, n - 1 - c, n - 1 - c), 0)

    return fwd, bwd


def _mlstm_constants():
    c = ML_CHUNK
    tri = np.tril(np.ones((c, c), np.float32))
    cum = np.stack([tri, tri.T])
    negmask = np.where(cum > 0.5, 0.0, -1e30).astype(np.float32)
    sel = np.zeros((2, 3, 128, HEADS, HEAD_DIM), np.float32)
    for d in range(2):
        for h in range(HEADS):
            sel[d, :, 8 * d + h, h, :] = 1.0
    sel = sel.reshape(2, 3 * 128, HEADS * HEAD_DIM)
    return jnp.asarray(cum, BF16), jnp.asarray(negmask, F32), jnp.asarray(sel, BF16)


def _running_max(x, reverse):
    n, lanes = x.shape
    row = lax.broadcasted_iota(jnp.int32, (n, lanes), 0)
    shift = 1
    while shift < n:
        if shift % 8 == 0:
            pad = jnp.full((shift, lanes), -jnp.inf, x.dtype)
            shifted = (jnp.concatenate([x[shift:], pad], axis=0) if reverse
                       else jnp.concatenate([pad, x[:n - shift]], axis=0))
        elif reverse:
            shifted = jnp.where(row >= n - shift, -jnp.inf, pltpu.roll(x, n - shift, axis=0))
        else:
            shifted = jnp.where(row < shift, -jnp.inf, pltpu.roll(x, shift, axis=0))
        x = jnp.maximum(x, shifted)
        shift *= 2
    return x


def _mlstm_phases(qf_ref, qb_ref, kf_ref, kb_ref, vf_ref, vb_ref, gf_ref, gb_ref,
                  cum_ref, neg_ref, sel_ref, hf_ref, hb_ref, cn_ref, m_ref):
    c = ML_CHUNK
    dirs = ((qf_ref, kf_ref, vf_ref, gf_ref, hf_ref, c - 1), (qb_ref, kb_ref, vb_ref, gb_ref, hb_ref, 0))
    streams = [(bi, d) + dirs[d] for bi in range(qf_ref.shape[0]) for d in range(2)]
    heads = [slice(h * HEAD_DIM, (h + 1) * HEAD_DIM) for h in range(HEADS)]
    lane = lax.broadcasted_iota(jnp.int32, (c, 128), 1)
    cat3 = lambda a: jnp.concatenate(_split3(a), axis=1)

    cums = []
    for bi, d, _, _, _, g_ref, _, _ in streams:
        sums = _dot(cum_ref[d], cat3(g_ref[bi]))
        cums.append(sums[:, :128] + sums[:, 128:256] + sums[:, 256:])
    yield

    small = []
    for (bi, d, _, _, _, g_ref, _, end_row), cs in zip(streams, cums):
        valid = jnp.logical_and(lane >= 8 * d, lane < 8 * d + HEADS)
        b_sh = jnp.where(valid, pltpu.roll(cs, 128 - HEADS, axis=1), 0.0)
        u = jnp.where(valid, g_ref[bi], 0.0) - b_sh
        mu = m_ref[bi, d][0:1, :]
        m_run = jnp.maximum(_running_max(u, d == 1), mu)
        m_end = m_run[end_row:end_row + 1, :]
        w_inter = jnp.exp(mu - m_run)
        m_ref[bi, d] = jnp.broadcast_to(b_sh[end_row:end_row + 1, :] + m_end, (8, 128))
        small.append(dict(
            m_run=m_run, w_inter=w_inter.astype(BF16), w_k=jnp.exp(u - m_end).astype(BF16),
            floor=jnp.exp(-(b_sh + m_run)).astype(BF16),
            carry=jnp.broadcast_to(w_inter[end_row:end_row + 1, :], (8, 128)), u_t=u.T))
    yield

    wide = []
    for (bi, d, *_), sm in zip(streams, small):
        sel3 = sel_ref[d]
        sel1 = sel_ref[d, 0:128, :]
        wide.append(dict(
            m_run=_dot(cat3(sm["m_run"]), sel3), w_inter=_dot(sm["w_inter"], sel1),
            w_k=_dot(sm["w_k"], sel1), floor=_dot(sm["floor"], sel1),
            carry=_dot(cat3(sm["carry"]), sel3)))

    qk = []
    for bi, d, q_ref, k_ref, *_ in streams:
        qk.append([_dot_nt(q_ref[bi, :, hb], k_ref[bi, :, hb]) for hb in heads])
    yield

    ones = jnp.ones((c, HEAD_DIM), BF16)
    zeros = jnp.zeros((HEAD_DIM - ML_DK, 2 * HEAD_DIM), BF16)
    operands = []
    for n, (bi, d, q_ref, _, v_ref, _, _, _) in enumerate(streams):
        for h, hb in enumerate(heads):
            u_row = jnp.broadcast_to(small[n]["u_t"][8 * d + h:8 * d + h + 1, :], (c, c))
            w = jnp.exp(u_row - wide[n]["m_run"][:, hb] + neg_ref[d])
            s = (qk[n][h] * w).astype(BF16)
            q_w = (q_ref[bi, :, hb].astype(F32) * wide[n]["w_inter"][:, hb]).astype(BF16)
            v1 = jnp.concatenate([v_ref[bi, :, hb], ones], axis=1)
            cn = cn_ref[bi, d, h]
            rhs = jnp.concatenate([v1, cn.astype(BF16), zeros], axis=0)
            operands.append((jnp.concatenate([s, q_w], axis=1), rhs, v1, cn))
        yield

    for n, (bi, d, _, k_ref, _, _, h_ref, _) in enumerate(streams):
        for h, hb in enumerate(heads):
            lhs, rhs, v1, cn = operands[n * HEADS + h]
            res = _dot(lhs, rhs)
            k_w = (k_ref[bi, :, hb].astype(F32) * wide[n]["w_k"][:, hb]).astype(BF16)
            upd = _dot_tn(k_w, v1)[:ML_DK]
            h_ref[bi, :, hb] = (res[:, :HEAD_DIM] / jnp.maximum(
                jnp.abs(res[:, HEAD_DIM:]), wide[n]["floor"][:, hb])).astype(BF16)
            carry = wide[n]["carry"][0:1, hb]
            cn_ref[bi, d, h] = cn * jnp.concatenate([carry, carry], axis=1) + upd
    yield


def _scans_kernel(hq_f, hq_b, hv_f, hv_b, lf_f, lf_b, mq_f, mq_b, mk_f, mk_b, mv_f, mv_b, mg_f, mg_b,
                  hg_cum, hg_side, hg_amask, ml_cum, ml_neg, ml_sel,
                  hgo_f, hgo_b, mlo_f, mlo_b, st_ref, g_ref, cn_ref, m_ref):
    @pl.when(pl.program_id(1) == 0)
    def _():
        st_ref[...] = jnp.zeros_like(st_ref)
        cn_ref[...] = jnp.zeros_like(cn_ref)
        m_ref[...] = jnp.zeros_like(m_ref)

    hg_args = (hq_f, hq_b, hv_f, hv_b, lf_f, lf_b, hg_cum, hg_side, hg_amask, hgo_f, hgo_b, st_ref, g_ref)
    hgrn2 = itertools.chain(_hgrn2_phases(0, *hg_args), _hgrn2_phases(1, *hg_args))
    mlstm = _mlstm_phases(mq_f, mq_b, mk_f, mk_b, mv_f, mv_b, mg_f, mg_b, ml_cum, ml_neg, ml_sel,
                          mlo_f, mlo_b, cn_ref, m_ref)
    exhausted = object()
    progressed = True
    while progressed:
        progressed = False
        for phases, count in ((hgrn2, 2), (mlstm, 1)):
            for _ in range(count):
                progressed = (next(phases, exhausted) is not exhausted) or progressed


def _scans(hq, hv, lff, lfb, mq, mk, mv, mg, hg_consts, ml_consts):
    b, t, w = hq.shape
    n_ctx = CTX_LEN // ML_CHUNK
    n_lat = t // ML_CHUNK - n_ctx
    fwd, bwd = _scan_chunk_maps(n_lat, n_ctx)
    nb = math.gcd(SCAN_ROWS, b)
    blk = lambda width, m: pl.BlockSpec((nb, ML_CHUNK, width), m)
    both = lambda width: [blk(width, fwd), blk(width, bwd)]
    const = lambda a: pl.BlockSpec(a.shape, lambda i, c: (0, 0, 0))
    consts = list(hg_consts) + list(ml_consts)
    out = jax.ShapeDtypeStruct((b, t, w), BF16)
    return pl.pallas_call(
        _scans_kernel,
        out_shape=(out, out, out, out),
        grid=(b // nb, t // ML_CHUNK),
        in_specs=(both(w) + both(w) + [blk(w, fwd), blk(w, bwd)] + both(w) + both(w) + both(w) + both(128)
                  + [const(a) for a in consts]),
        out_specs=tuple(both(w) + both(w)),
        scratch_shapes=[pltpu.VMEM((nb, 2, HEADS, HEAD_DIM, HEAD_DIM), F32),
                        pltpu.VMEM((nb, 2, CHUNK, w), F32),
                        pltpu.VMEM((nb, 2, HEADS, ML_DK, 2 * HEAD_DIM), F32),
                        pltpu.VMEM((nb, 2, 8, 128), F32)],
        compiler_params=_compiler_params(2),
        name="scans",
    )(hq, hq, hv, hv, lff, lfb, mq, mq, mk, mk, mv, mv, mg, mg, *consts)


def _route_block(hx, rwt_ref, rbt_ref, tri_ref, hxa_o, hxb_o, route_o, cnt_o, carry_ref):
    @pl.when(jnp.logical_and(pl.program_id(0) == 0, pl.program_id(1) == 0))
    def _():
        carry_ref[...] = jnp.zeros_like(carry_ref)

    hxa_o[...] = _pack_pairs(hx[:, 0:256], hx[:, 256:512])
    hxb_o[...] = _pack_pairs(hx[:, 512:768], hx[:, 768:1024])
    h_hi, h_mid, h_lo = _split3(hx)
    r_hi, r_mid, _ = _split3(rwt_ref[...])
    r_both = jnp.concatenate([r_hi.astype(F32), r_mid.astype(F32)], axis=0).astype(BF16)
    by_hi = _dot_nt(r_both, h_hi)
    by_mid = _dot_nt(r_both, h_mid)
    by_lo = _dot_nt(r_hi, h_lo)
    logits = ((by_hi[:N_EXPERTS] + by_hi[N_EXPERTS:]) + (by_mid[:N_EXPERTS] + by_mid[N_EXPERTS:])
              + by_lo + rbt_ref[...])
    sub = lax.broadcasted_iota(jnp.int32, logits.shape, 0).astype(F32)
    top1 = jnp.max(logits, axis=0, keepdims=True)
    idx1 = jnp.min(jnp.where(logits == top1, sub, float(N_EXPERTS)), axis=0, keepdims=True)
    rest = jnp.where(sub == idx1, -jnp.inf, logits)
    top2 = jnp.max(rest, axis=0, keepdims=True)
    idx2 = jnp.min(jnp.where(rest == top2, sub, float(N_EXPERTS)), axis=0, keepdims=True)
    p1 = 1.0 / (1.0 + jnp.exp(top2 - top1))
    p2 = 1.0 - p1
    oh1 = jnp.where(sub == idx1, 1.0, 0.0)
    oh2 = jnp.where(sub == idx2, 1.0, 0.0)
    both = oh1 + oh2
    before = _dot(both.astype(BF16), tri_ref[...]) + carry_ref[...]
    rank1 = jnp.sum(before * oh1, axis=0, keepdims=True)
    rank2 = jnp.sum(before * oh2, axis=0, keepdims=True)
    total = carry_ref[...] + jnp.broadcast_to(jnp.sum(both, axis=1, keepdims=True), both.shape)
    carry_ref[...] = total
    cnt_o[...] = total
    route = jnp.zeros_like(logits)
    for name, val in (("e1", idx1), ("e2", idx2), ("rank1", rank1), ("rank2", rank2), ("p1", p1), ("p2", p2)):
        route = jnp.where(sub == float(ROUTE_ROWS[name]), val, route)
    route_o[...] = route


def _mixout_kernel(n_lat_blocks, with_route, x_ref, c_ref, hf_ref, hb_ref, mf_ref, mb_ref, hg_ref, mo_ref,
                   nrm_ref, w_ref, mod_ref, ln_ref, *rest):
    x = x_ref[0] if with_route else jnp.where(pl.program_id(1) >= n_lat_blocks, c_ref[0], x_ref[0])
    parts = []
    streams = ((hf_ref, hb_ref, hg_ref, 0), (mf_ref, mb_ref, mo_ref, 1))
    for f_ref, b_ref, gate_ref, row in streams:
        o = f_ref[0].astype(F32) + b_ref[0].astype(F32)
        gate = gate_ref[0].astype(F32)
        for h in range(HEADS):
            sl = slice(h * HEAD_DIM, (h + 1) * HEAD_DIM)
            oh = o[:, sl]
            ms = jnp.mean(oh * oh, axis=-1, keepdims=True)
            parts.append((oh * lax.rsqrt(ms + RMS_EPS) * nrm_ref[row:row + 1, sl] * gate[:, sl]).astype(BF16))
    y = jnp.concatenate(parts, axis=1)
    mix = _dot(y, w_ref[...])
    r = DEEPNORM_ALPHA * x + mod_ref[0, 0, 2:3, :] * mix
    xn = _layer_norm(r, ln_ref[0:1, :], ln_ref[1:2, :])
    if not with_route:
        (o_ref,) = rest
        o_ref[0] = xn
        return
    rwt_ref, rbt_ref, tri_ref, o_ref, hxa_o, hxb_o, route_o, cnt_o, carry_ref = rest
    o_ref[0] = xn
    hx = xn * (1.0 + mod_ref[0, 0, 4:5, :]) + mod_ref[0, 0, 3:4, :]
    _route_block(hx, rwt_ref, rbt_ref, tri_ref, hxa_o, hxb_o, route_o, cnt_o, carry_ref)


def _mixout(lat, ctx, ctx_block, hg_f, hg_b, ml_f, ml_b, hg_gate, ml_gate, norms, w_bf, modall, ln,
            n_lat_blocks, n_blocks, router=None):
    b, _, d = lat.shape
    tm = TOKEN_BLOCK
    if router is not None:
        assert n_blocks == n_lat_blocks
        tm = ROUTE_BLOCK
        n_blocks = n_lat_blocks = n_blocks * TOKEN_BLOCK // tm
        ctx_block = 0
    tok = lambda width: pl.BlockSpec((1, tm, width), lambda i, j: (i, j, 0))
    const = lambda shape: pl.BlockSpec(shape, lambda i, j: (0,) * len(shape))
    in_specs = _stream_specs(lat, ctx, ctx_block, n_lat_blocks, tm) + [
        tok(512), tok(512), tok(512), tok(512), tok(512), tok(512),
        const((2, 512)),
        pl.BlockSpec((d, d), lambda i, j: (0, 0), pipeline_mode=pl.Buffered(1)),
        pl.BlockSpec((1, 1, 6, d), lambda i, j: (i, jnp.where(j >= n_lat_blocks, 1, 0), 0, 0)),
        const((2, d)),
    ]
    args = [lat, ctx, hg_f, hg_b, ml_f, ml_b, hg_gate, ml_gate, norms, w_bf, modall, ln]
    out_shape = [jax.ShapeDtypeStruct((b, n_blocks * tm, d), F32)]
    out_specs = [tok(d)]
    scratch = []
    if router is not None:
        rw, rb = router
        n = b * n_blocks * tm
        rows = lambda width: pl.BlockSpec((tm, width), lambda i, j: (i * n_blocks + j, 0))
        in_specs += [const((N_EXPERTS, d)), const((N_EXPERTS, tm)), const((tm, tm))]
        args += [rw.T, jnp.broadcast_to(rb.reshape(N_EXPERTS, 1), (N_EXPERTS, tm)),
                 jnp.asarray(np.triu(np.ones((tm, tm), np.float32), 1), BF16)]
        out_shape += [jax.ShapeDtypeStruct((n, PACK_W), jnp.uint32), jax.ShapeDtypeStruct((n, PACK_W), jnp.uint32),
                      jax.ShapeDtypeStruct((N_EXPERTS, n), F32), jax.ShapeDtypeStruct((N_EXPERTS, tm), F32)]
        out_specs += [rows(PACK_W), rows(PACK_W),
                      pl.BlockSpec((N_EXPERTS, tm), lambda i, j: (0, i * n_blocks + j)), const((N_EXPERTS, tm))]
        scratch = [pltpu.VMEM((N_EXPERTS, tm), F32)]
    return pl.pallas_call(
        functools.partial(_mixout_kernel, n_lat_blocks, router is not None),
        out_shape=tuple(out_shape),
        grid=(b, n_blocks),
        in_specs=in_specs,
        out_specs=tuple(out_specs),
        scratch_shapes=scratch,
        compiler_params=_compiler_params(2),
        name="mix_out_ln_route" if router is not None else "mix_out_ln",
    )(*args)


FF_CHUNK = 256


def _ffn_kernel(x_ref, mod_ref, wgu_ref, wd_ref, ln_ref, o_ref, h_ref):
    x = x_ref[0]
    hx = (x * (1.0 + mod_ref[0, 0, 4:5, :]) + mod_ref[0, 0, 3:4, :]).astype(BF16)
    for c0 in range(0, D_FF, FF_CHUNK):
        g = _dot(hx, wgu_ref[:, c0:c0 + FF_CHUNK])
        u = _dot(hx, wgu_ref[:, D_FF + c0:D_FF + c0 + FF_CHUNK])
        h_ref[:, c0:c0 + FF_CHUNK] = (_silu(g) * u).astype(BF16)
    f = _dot(h_ref[...], wd_ref[...])
    r = DEEPNORM_ALPHA * x + mod_ref[0, 0, 5:6, :] * f
    o_ref[0] = _layer_norm(r, ln_ref[0:1, :], ln_ref[1:2, :])


def _ffn(xt, modall, wgu_bf, wd_bf, ln, n_lat_blocks):
    b, t, d = xt.shape
    tok = pl.BlockSpec((1, TOKEN_BLOCK, d), lambda i, j: (i, j, 0))
    return pl.pallas_call(
        _ffn_kernel,
        out_shape=jax.ShapeDtypeStruct((b, t, d), F32),
        grid=(b, t // TOKEN_BLOCK),
        in_specs=[
            tok,
            pl.BlockSpec((1, 1, 6, d), lambda i, j: (i, jnp.where(j >= n_lat_blocks, 1, 0), 0, 0)),
            pl.BlockSpec((d, 2 * D_FF), lambda i, j: (0, 0), pipeline_mode=pl.Buffered(1)),
            pl.BlockSpec((D_FF, d), lambda i, j: (0, 0), pipeline_mode=pl.Buffered(1)),
            pl.BlockSpec((2, d), lambda i, j: (0, 0)),
        ],
        out_specs=tok,
        scratch_shapes=[pltpu.VMEM((TOKEN_BLOCK, D_FF), BF16)],
        compiler_params=_compiler_params(2),
        name="ffn_dense_ln",
    )(xt, modall, wgu_bf, wd_bf, ln)


EXPERT_CHUNKS = ((0, 256), (256, 256), (512, 256), (768, 256), (1024, 256), (1280, 128))
ROUTE_ROWS = dict(e1=0, e2=1, rank1=2, rank2=3, p1=4, p2=5)
PACK_W = 256
SC_WINDOW = 128


def _pack_pairs(a, b):
    ua = lax.bitcast_convert_type(a.astype(BF16).astype(F32), jnp.uint32)
    ub = lax.bitcast_convert_type(b.astype(BF16).astype(F32), jnp.uint32)
    return (ua >> 16) | ub


def _unpack_pairs(w):
    a = lax.bitcast_convert_type(w << 16, F32)
    b = lax.bitcast_convert_type(w & jnp.uint32(0xFFFF0000), F32)
    return a, b


def _sc_mesh():
    return plsc.VectorSubcoreMesh(core_axis_name="core", subcore_axis_name="subcore")


def _sc_scatter_rows(rows, idx, n_out):
    n_src, width = rows.shape
    m = idx.shape[0]
    src_blocks = n_src // SC_WINDOW

    @functools.partial(pl.kernel, out_type=jax.ShapeDtypeStruct((n_out, width), rows.dtype), mesh=_sc_mesh())
    def scatter(x_hbm, i_hbm, o_hbm):
        def body(x_vmem, i_vmem):
            pltpu.sync_copy(x_vmem, o_hbm.at[i_vmem.at[0]])

        pltpu.emit_pipeline(
            body, grid=(m // SC_WINDOW,),
            in_specs=[pl.BlockSpec((SC_WINDOW, width), lambda i: (i % src_blocks, 0)),
                      pl.BlockSpec((1, SC_WINDOW), lambda i: (0, i))],
            out_specs=[],
            core_axis_name=("core", "subcore"), dimension_semantics=(pltpu.PARALLEL,),
        )(x_hbm, i_hbm)

    return scatter(rows, idx.reshape(1, m))


def _sc_gather_rows(table, idx):
    m = idx.shape[0]
    width = table.shape[1]

    @functools.partial(pl.kernel, out_type=jax.ShapeDtypeStruct((m, width), table.dtype), mesh=_sc_mesh())
    def gather(t_hbm, i_hbm, o_hbm):
        def body(i_vmem, o_vmem):
            pltpu.sync_copy(t_hbm.at[i_vmem.at[0]], o_vmem)

        pltpu.emit_pipeline(
            body, grid=(m // SC_WINDOW,),
            in_specs=[pl.BlockSpec((1, SC_WINDOW), lambda i: (0, i))],
            out_specs=[pl.BlockSpec((SC_WINDOW, width), lambda i: (i, 0))],
            core_axis_name=("core", "subcore"), dimension_semantics=(pltpu.PARALLEL,),
        )(i_hbm, o_hbm)

    return gather(table, idx.reshape(1, m))


CAST_ROWS = 64


def _experts_kernel(te_ref, used_ref, xa_ref, xb_ref, wgu_ref, wd_ref, ya_o, yb_o, h_ref, wgu_bf, wd_bf):
    i = pl.program_id(0)
    changed = jnp.logical_or(i == 0, te_ref[i] != te_ref[jnp.maximum(i - 1, 0)])

    @pl.when(jnp.logical_and(i < used_ref[0], changed))
    def _():
        def cast_gu(r, carry):
            rows = pl.ds(pl.multiple_of(r * CAST_ROWS, CAST_ROWS), CAST_ROWS)
            wgu_bf[rows, :] = wgu_ref[0, rows, :].astype(BF16)
            return carry

        def cast_d(r, carry):
            rows = pl.ds(pl.multiple_of(r * CAST_ROWS, CAST_ROWS), CAST_ROWS)
            wd_bf[rows, :] = wd_ref[0, rows, :].astype(BF16)
            return carry

        lax.fori_loop(0, D_MODEL // CAST_ROWS, cast_gu, 0)
        lax.fori_loop(0, D_EXPERT // CAST_ROWS, cast_d, 0)

    @pl.when(i < used_ref[0])
    def _():
        hx = jnp.concatenate([p.astype(BF16) for w in (xa_ref[...], xb_ref[...]) for p in _unpack_pairs(w)],
                             axis=1)
        for c0, cw in EXPERT_CHUNKS:
            g = _dot(hx, wgu_bf[:, c0:c0 + cw])
            u = _dot(hx, wgu_bf[:, D_EXPERT + c0:D_EXPERT + c0 + cw])
            h_ref[:, c0:c0 + cw] = (_silu(g) * u).astype(BF16)
        f = _dot(h_ref[...], wd_bf[...])
        ya_o[...] = _pack_pairs(f[:, 0:256], f[:, 256:512])
        yb_o[...] = _pack_pairs(f[:, 512:768], f[:, 768:1024])


def _moe_experts(tile_expert, n_used, xa, xb, wgu, wd):
    r = xa.shape[0]
    tm = EXPERT_TILE
    rows = pl.BlockSpec((tm, PACK_W), lambda i, te, nu: (i, 0))
    out = jax.ShapeDtypeStruct((r, PACK_W), jnp.uint32)
    return pl.pallas_call(
        _experts_kernel,
        out_shape=(out, out),
        grid_spec=pltpu.PrefetchScalarGridSpec(
            num_scalar_prefetch=2,
            grid=(r // tm,),
            in_specs=[rows, rows,
                      pl.BlockSpec((1, D_MODEL, 2 * D_EXPERT), lambda i, te, nu: (te[i], 0, 0)),
                      pl.BlockSpec((1, D_EXPERT, D_MODEL), lambda i, te, nu: (te[i], 0, 0))],
            out_specs=(rows, rows),
            scratch_shapes=[pltpu.VMEM((tm, D_EXPERT), BF16),
                            pltpu.VMEM((D_MODEL, 2 * D_EXPERT), BF16),
                            pltpu.VMEM((D_EXPERT, D_MODEL), BF16)]),
        compiler_params=_compiler_params(1),
        name="moe_experts",
    )(tile_expert, n_used, xa, xb, wgu, wd)


def _combine_kernel(x_ref, mod_ref, route_ref, y1a_ref, y1b_ref, y2a_ref, y2b_ref, ln_ref, o_ref):
    route = route_ref[...].T
    f = None
    for ya_ref, yb_ref, name in ((y1a_ref, y1b_ref, "p1"), (y2a_ref, y2b_ref, "p2")):
        col = ROUTE_ROWS[name]
        y = jnp.concatenate(_unpack_pairs(ya_ref[...]) + _unpack_pairs(yb_ref[...]), axis=1)
        term = route[:, col:col + 1] * y
        f = term if f is None else f + term
    r = DEEPNORM_ALPHA * x_ref[0] + mod_ref[0, 0, 5:6, :] * f
    o_ref[0] = _layer_norm(r, ln_ref[0:1, :], ln_ref[1:2, :])


def _moe_combine(xt, modall, route, yga, ygb, ln, seq):
    b, _, d = xt.shape
    tm = ROUTE_BLOCK
    nj = seq // tm
    nblk = b * nj
    first = lambda width: pl.BlockSpec((tm, width), lambda i, j: (i * nj + j, 0))
    second = lambda width: pl.BlockSpec((tm, width), lambda i, j: (nblk + i * nj + j, 0))
    return pl.pallas_call(
        _combine_kernel,
        out_shape=jax.ShapeDtypeStruct((b, seq, d), F32),
        grid=(b, nj),
        in_specs=[
            pl.BlockSpec((1, tm, d), lambda i, j: (i, j, 0)),
            pl.BlockSpec((1, 1, 6, d), lambda i, j: (i, 0, 0, 0)),
            pl.BlockSpec((N_EXPERTS, tm), lambda i, j: (0, i * nj + j)),
            first(PACK_W), first(PACK_W), second(PACK_W), second(PACK_W),
            pl.BlockSpec((2, d), lambda i, j: (0, 0)),
        ],
        out_specs=pl.BlockSpec((1, tm, d), lambda i, j: (i, j, 0)),
        compiler_params=_compiler_params(2),
        name="moe_combine_ln",
    )(xt, modall, route, yga, ygb, yga, ygb, ln)


def _moe(xt, modall, hxa, hxb, route, counts, wgu, wd, ln, seq):
    b = xt.shape[0]
    n = b * seq
    tm = EXPERT_TILE
    counts = counts[:, 0].astype(jnp.int32)
    sizes = (counts + tm - 1) // tm * tm
    ends = jnp.cumsum(sizes)
    starts = ends - sizes
    col = lambda name: route[ROUTE_ROWS[name]].astype(jnp.int32)
    pos = jnp.concatenate([jnp.take(starts, col("e1")) + col("rank1"),
                           jnp.take(starts, col("e2")) + col("rank2")])
    n_rows = 2 * n + N_EXPERTS * tm
    tile_start = jnp.arange(n_rows // tm, dtype=jnp.int32) * tm
    tile_expert = jnp.minimum(jnp.sum(tile_start[:, None] >= ends[None, :], axis=1), N_EXPERTS - 1)
    n_used = (ends[-1:] // tm).astype(jnp.int32)
    xa = _sc_scatter_rows(hxa, pos, n_rows)
    xb = _sc_scatter_rows(hxb, pos, n_rows)
    ya, yb = _moe_experts(tile_expert.astype(jnp.int32), n_used, xa, xb, wgu, wd)
    return _moe_combine(xt, modall, route, _sc_gather_rows(ya, pos), _sc_gather_rows(yb, pos), ln, seq)


def _permute_w_in(w):
    gates = jnp.pad(w[:, 3584:3600], ((0, 0), (0, 112)))
    return jnp.concatenate([w[:, :3584], w[:, 3600:4112], gates], axis=1).astype(BF16)


def kernel(x, c, ctx, c_ctx, w_ada, b_ada, w_in, ml_conv_w, ml_conv_b, hg_lower_bound, ml_gate_bias,
           hg_norm, ml_norm, w_out, ln_g, ln_b, ffn_w_gate_up, ffn_w_down, router_w, router_b,
           moe_w_gate_up, moe_w_down):
    bsz, seq, d = x.shape
    depth = w_ada.shape[0]
    assert depth == DEPTH and d == D_MODEL and ctx.shape[1] == CTX_LEN and seq % ROUTE_BLOCK == 0
    n_lat_blocks = seq // TOKEN_BLOCK
    n_blocks = n_lat_blocks + CTX_LEN // TOKEN_BLOCK

    lb = jnp.cumsum(jax.nn.softmax(hg_lower_bound.astype(F32), axis=0), axis=0)
    lb = lb - lb[0]

    cond = jnp.concatenate([c, c_ctx[None, :], jnp.zeros((16 - bsz - 1, d), F32)], axis=0)
    mod = _modulation(cond, w_ada, b_ada)
    mod = mod.reshape(depth, 16, 6, d)

    hg_consts = _hgrn2_constants()
    ml_consts = _mlstm_constants()

    lat, cx, cx_block = x, ctx, 0
    out = None
    for l in range(depth):
        last = l == depth - 1
        mod_c = jnp.broadcast_to(mod[l, bsz][None], (bsz, 6, d))
        modall = jnp.stack([mod[l, :bsz], mod_c], axis=1)
        lbp = lb[l]
        convp = jnp.concatenate([ml_conv_w[l], ml_conv_b[l][None]], axis=0)
        gb = jnp.pad(ml_gate_bias[l].reshape(1, -1), ((0, 0), (0, 112)))
        feats = _inproj(lat, cx, cx_block, modall, _permute_w_in(w_in[l]), lbp, convp, gb, n_lat_blocks)
        hq, lff, lfb, hv, hg_gate, mq, mk, mv, ml_gate, mg = feats
        hg_f, hg_b, ml_f, ml_b = _scans(hq, hv, lff, lfb, mq, mk, mv, mg, hg_consts, ml_consts)
        norms = jnp.stack([hg_norm[l].reshape(-1), ml_norm[l].reshape(-1)])
        jj = l // 2
        dense = l % 2 == 0
        assert dense or last
        mixed = _mixout(lat, cx, cx_block, hg_f, hg_b, ml_f, ml_b, hg_gate, ml_gate, norms,
                        w_out[l].astype(BF16), modall, jnp.stack([ln_g[l, 0], ln_b[l, 0]]), n_lat_blocks,
                        n_lat_blocks if last else n_blocks,
                        router=None if dense else (router_w[jj], router_b[jj]))
        ln1 = jnp.stack([ln_g[l, 1], ln_b[l, 1]])
        if dense:
            xt = _ffn(mixed[0], modall, ffn_w_gate_up[jj].astype(BF16), ffn_w_down[jj].astype(BF16), ln1,
                      n_lat_blocks)
            lat, cx, cx_block = xt, xt, n_lat_blocks
            out = xt[:, :seq]
        else:
            out = _moe(*mixed[:1], modall, *mixed[1:], moe_w_gate_up[jj], moe_w_down[jj], ln1, seq)
    return out
```

```python
import functools
import itertools
import math

import numpy as np
import jax
import jax.numpy as jnp
from jax import lax
from jax.experimental import pallas as pl
from jax.experimental.pallas import tpu as pltpu
from jax.experimental.pallas import tpu_sc as plsc

F32 = jnp.float32
BF16 = jnp.bfloat16

D_MODEL = 1024
CTX_LEN = 256
GRID_W = 64
HG_WIDTH = 512
HEADS = 4
HEAD_DIM = 128
ML_DK = 64
CHUNK = 64
D_FF = 2816
N_EXPERTS = 8
D_EXPERT = 1408
DEPTH = 2
DEEPNORM_ALPHA = (2 * DEPTH) ** 0.25
LOG2_E = 1.4426950408889634
LN_EPS = 1e-5
RMS_EPS = 1e-6

TOKEN_BLOCK = 256
ROUTE_BLOCK = 512
EXPERT_TILE = 512
SCAN_ROWS = 4
ML_CHUNK = 128
PROJ_COLS = 8 * 512 + 128
N_LEVELS = 6
VMEM_LIMIT = 56 * 1024 * 1024

NT_DIMS = (((1,), (1,)), ((), ()))
TN_DIMS = (((0,), (0,)), ((), ()))


def _compiler_params(grid_rank):
    return pltpu.CompilerParams(
        dimension_semantics=("arbitrary",) * grid_rank, vmem_limit_bytes=VMEM_LIMIT)


def _dot(a, b):
    return jnp.dot(a, b, preferred_element_type=F32)


def _dot_nt(a, b):
    return lax.dot_general(a, b, NT_DIMS, preferred_element_type=F32)


def _dot_tn(a, b):
    return lax.dot_general(a, b, TN_DIMS, preferred_element_type=F32)


def _sigmoid(z):
    return 0.5 * jnp.tanh(0.5 * z) + 0.5


def _silu(z):
    h = 0.5 * z
    return h + h * jnp.tanh(h)


def _log_sigmoid(z):
    return jnp.minimum(z, 0.0) - jnp.log(1.0 + jnp.exp(-jnp.abs(z)))


def _split3(x):
    hi = x.astype(BF16)
    r = x - hi.astype(F32)
    mid = r.astype(BF16)
    lo = (r - mid.astype(F32)).astype(BF16)
    return hi, mid, lo


def _layer_norm(r, g, b):
    mu = jnp.mean(r, axis=-1, keepdims=True)
    d = r - mu
    var = jnp.mean(d * d, axis=-1, keepdims=True)
    return d * lax.rsqrt(var + LN_EPS) * g + b


def _mod_kernel(c_ref, w_ref, b_ref, o_ref):
    c = c_ref[...]
    s = _silu(c)
    o_ref[0] = jnp.dot(s, w_ref[0], preferred_element_type=F32,
                       precision=lax.Precision.HIGHEST) + b_ref[0]


def _modulation(cond, w_ada, b_ada):
    depth, d, n = w_ada.shape
    rows = cond.shape[0]
    nb = 1536
    return pl.pallas_call(
        _mod_kernel,
        out_shape=jax.ShapeDtypeStruct((depth, rows, n), F32),
        grid=(depth, n // nb),
        in_specs=[
            pl.BlockSpec((rows, d), lambda l, j: (0, 0)),
            pl.BlockSpec((1, d, nb), lambda l, j: (l, 0, j)),
            pl.BlockSpec((1, 1, nb), lambda l, j: (l, 0, j)),
        ],
        out_specs=pl.BlockSpec((1, rows, nb), lambda l, j: (l, 0, j)),
        compiler_params=_compiler_params(2),
        name="adaln_mod",
    )(cond, w_ada, b_ada.reshape(depth, 1, n))


def _stream_specs(lat, ctx, ctx_block, n_lat_blocks, rows=TOKEN_BLOCK):
    d = lat.shape[-1]
    return [pl.BlockSpec((1, rows, d), lambda i, j: (i, jnp.minimum(j, n_lat_blocks - 1), 0)),
            pl.BlockSpec((1, rows, d), lambda i, j: (i, ctx_block, 0))]


def _inproj_kernel(n_lat_blocks, x_ref, c_ref, mod_ref, w_ref, lb_ref, conv_ref, gb_ref,
                   hq_o, lff_o, lfb_o, hv_o, hg_o, mq_o, mk_o, mv_o, mo_o, mg_o):
    j = pl.program_id(1)
    x = jnp.where(j >= n_lat_blocks, c_ref[0], x_ref[0])
    hx = (x * (1.0 + mod_ref[0, 0, 1:2, :]) + mod_ref[0, 0, 0:1, :]).astype(BF16)

    rows = x.shape[0]
    seg_mask = jnp.where(j >= n_lat_blocks, CTX_LEN - 1, GRID_W - 1)
    pos = lax.broadcasted_iota(jnp.int32, (rows, 1), 0) & seg_mask

    def silu_to(out):
        def finish(z):
            out[0] = _silu(z).astype(BF16)
        return finish

    def log_forget_to(out, r):
        def finish(z):
            lb = lb_ref[r:r + 1, :]
            t = jnp.exp(-jnp.abs(z))
            num = jnp.where(z >= 0.0, 1.0 + lb * t, lb + t)
            out[0] = jnp.maximum(jnp.log2(num), jnp.minimum(z, 0.0) * LOG2_E) - jnp.log2(1.0 + t)
        return finish

    def cast_to(out):
        def finish(z):
            out[0] = z.astype(BF16)
        return finish

    def conv_silu_to(q_out, k_out):
        def spread(a):
            gap = jnp.zeros((rows, HEAD_DIM - ML_DK), a.dtype)
            pieces = [p for h in range(HEADS) for p in (a[:, h * ML_DK:(h + 1) * ML_DK], gap)]
            return jnp.concatenate(pieces, axis=1)

        def finish(u):
            taps = conv_ref[...]
            u_prev = jnp.where(pos == 0, 0.0, pltpu.roll(u, 1, axis=0))
            u_next = jnp.where(pos == seg_mask, 0.0, pltpu.roll(u, rows - 1, axis=0))
            qk = _silu(taps[3:4, :] + u_prev * taps[0:1, :] + u * taps[1:2, :] + u_next * taps[2:3, :])
            q_out[0] = spread((qk[:, :HEADS * ML_DK] * (ML_DK ** -0.5)).astype(BF16))
            k_out[0] = spread(qk[:, HEADS * ML_DK:].astype(BF16))
        return finish

    def sigmoid_to(out):
        def finish(z):
            out[0] = _sigmoid(z).astype(BF16)
        return finish

    def gates_to(out):
        def finish(z):
            g = z + gb_ref[...]
            col = lax.broadcasted_iota(jnp.int32, g.shape, 1)
            is_forget = jnp.logical_and((col & 4) != 0, col < 16)
            out[0] = jnp.where(is_forget, _log_sigmoid(g), g)
        return finish

    finishers = [silu_to(hq_o), log_forget_to(lff_o, 0), log_forget_to(lfb_o, 1), cast_to(hv_o),
                 silu_to(hg_o), conv_silu_to(mq_o, mk_o), cast_to(mv_o), sigmoid_to(mo_o), gates_to(mg_o)]
    bounds = [(g * 512, (g + 1) * 512) for g in range(8)] + [(8 * 512, PROJ_COLS)]
    z = _dot(hx, w_ref[:, bounds[0][0]:bounds[0][1]])
    for n, finish in enumerate(finishers):
        z_next = None
        if n + 1 < len(bounds):
            z_next = _dot(hx, w_ref[:, bounds[n + 1][0]:bounds[n + 1][1]])
        finish(z)
        z = z_next


def _inproj(lat, ctx, ctx_block, modall, w_bf, lbp, convp, gb, n_lat_blocks):
    b, _, d = lat.shape
    nblk = n_lat_blocks + CTX_LEN // TOKEN_BLOCK
    t = nblk * TOKEN_BLOCK
    tok = lambda width: pl.BlockSpec((1, TOKEN_BLOCK, width), lambda i, j: (i, j, 0))
    shp = lambda width, dt: jax.ShapeDtypeStruct((b, t, width), dt)
    return pl.pallas_call(
        functools.partial(_inproj_kernel, n_lat_blocks),
        out_shape=(shp(512, BF16), shp(512, F32), shp(512, F32), shp(512, BF16), shp(512, BF16),
                   shp(512, BF16), shp(512, BF16), shp(512, BF16), shp(512, BF16), shp(128, F32)),
        grid=(b, nblk),
        in_specs=_stream_specs(lat, ctx, ctx_block, n_lat_blocks) + [
            pl.BlockSpec((1, 1, 6, d), lambda i, j: (i, jnp.where(j >= n_lat_blocks, 1, 0), 0, 0)),
            pl.BlockSpec((d, PROJ_COLS), lambda i, j: (0, 0), pipeline_mode=pl.Buffered(1)),
            pl.BlockSpec((2, 512), lambda i, j: (0, 0)),
            pl.BlockSpec((4, 512), lambda i, j: (0, 0)),
            pl.BlockSpec((1, 128), lambda i, j: (0, 0)),
        ],
        out_specs=(tok(512), tok(512), tok(512), tok(512), tok(512),
                   tok(512), tok(512), tok(512), tok(512), tok(128)),
        compiler_params=_compiler_params(2),
        name="inproj_features",
    )(lat, ctx, modall, w_bf, lbp, convp, gb)


def _hgrn2_constants():
    c = CHUNK
    amask = np.zeros((N_LEVELS + 1, c, c), np.float32)
    side = np.zeros((3, c, 4 * HEAD_DIM), np.float32)
    for l in range(N_LEVELS):
        s = c >> (l + 1)
        for i in range(c):
            m = (i // (2 * s)) * 2 * s + s
            if i >= m:
                amask[l, i, m - s:m] = 1.0
            if l >= 3:
                side[l - 3, i, :] = 1.0 if i >= m else -1.0
    amask[N_LEVELS] = np.eye(c)
    tri = np.tril(np.ones((c, c), np.float32))
    cum = np.stack([tri, tri.T])
    amask = np.stack([amask, amask[..., ::-1, ::-1]]).reshape(2, (N_LEVELS + 1) * c, c)
    side = np.stack([side, side[:, ::-1]]).reshape(2, 3 * c, 4 * HEAD_DIM)
    return (jnp.asarray(cum, BF16), jnp.asarray(np.ascontiguousarray(side), F32),
            jnp.asarray(np.ascontiguousarray(amask), F32))


def _hgrn2_level_exponents(g_ref, lf, side_ref, d):
    c = CHUNK
    g = g_ref[...]
    width = g.shape[1]

    def split_row(r, n):
        return jnp.broadcast_to(g_ref[pl.ds(r, 1), :], (n, width))

    out = []
    for l in range(3):
        s = c >> (l + 1)
        parts = []
        for r0 in range(0, c, 2 * s):
            lo, hi = g[r0:r0 + s], g[r0 + s:r0 + 2 * s]
            if d == 0:
                gm = split_row(r0 + s - 1, s)
                parts += [gm - lo, hi - gm]
            else:
                gm = split_row(r0 + s, s)
                parts += [lo - gm, gm - hi]
        out.append(jnp.concatenate(parts, axis=0))
    sub = lax.broadcasted_iota(jnp.int32, (8, width), 0)
    for l, s in ((3, 4), (4, 2)):
        tiles = []
        for r0 in range(0, c, 8):
            if s == 4:
                gm = split_row(r0 + (3 if d == 0 else 4), 8)
            else:
                a, b = (1, 5) if d == 0 else (2, 6)
                gm = jnp.where(sub < 4, split_row(r0 + a, 8), split_row(r0 + b, 8))
            tiles.append(gm)
        gm = jnp.concatenate(tiles, axis=0)
        out.append((g - gm) * side_ref[d, (l - 3) * c:(l - 2) * c, :])
    out.append(lf * jnp.maximum(side_ref[d, 2 * c:3 * c, :], 0.0))
    return out


def _hgrn2_phases(step, qf_ref, qb_ref, vf_ref, vb_ref, lff_ref, lfb_ref, cum_ref, side_ref, amask_ref,
                  of_ref, ob_ref, st_ref, g_ref):
    c = CHUNK
    rows_of = lambda d: slice(c, 2 * c) if step != d else slice(0, c)
    dirs = ((qf_ref, vf_ref, lff_ref, of_ref, c - 1), (qb_ref, vb_ref, lfb_ref, ob_ref, 0))
    streams = [(bi, d) + dirs[d] for bi in range(qf_ref.shape[0]) for d in range(2)]
    w = HG_WIDTH
    heads = [slice(h * HEAD_DIM, (h + 1) * HEAD_DIM) for h in range(HEADS)]

    for bi, d, _, _, lf_ref, _, _ in streams:
        sums = _dot(cum_ref[d], jnp.concatenate(_split3(lf_ref[bi, rows_of(d), :]), axis=1))
        g_ref[bi, d] = sums[:, :w] + sums[:, w:2 * w] + sums[:, 2 * w:]
    yield

    feats = []
    for bi, d, q_ref, _, lf_ref, _, end_row in streams:
        lf = lf_ref[bi, rows_of(d), :]
        gd_ref = g_ref.at[bi, d]
        g = gd_ref[...]
        g_end = jnp.broadcast_to(gd_ref[pl.ds(end_row, 1), :], (c, w))
        q = q_ref[bi, rows_of(d), :].astype(F32)
        k = 1.0 - jnp.exp2(lf)
        exps = _hgrn2_level_exponents(gd_ref, lf, side_ref, d)
        ts = []
        for l in range(N_LEVELS):
            s = c >> (l + 1)
            if l < 3:
                parts = []
                for r0 in range(0, c, 2 * s):
                    first, second = (k, q) if d == 0 else (q, k)
                    parts += [first[r0:r0 + s], second[r0 + s:r0 + 2 * s]]
                qk = jnp.concatenate(parts, axis=0)
            else:
                qk = jnp.where(side_ref[d, (l - 3) * c:(l - 2) * c, :] > 0.0, q, k)
            ts.append((qk * jnp.exp2(exps[l])).astype(BF16))
        feats.append(dict(
            ts=ts, q_bf=q.astype(BF16), k_bf=k.astype(BF16),
            q_in=(q * jnp.exp2(g)).astype(BF16),
            k_out=(k * jnp.exp2(g_end - g)).astype(BF16),
            total=jnp.exp2(gd_ref[pl.ds(end_row, 1), :])))
    yield

    diag_row = N_LEVELS * c
    prods = []
    for (bi, d, *_), f in zip(streams, feats):
        for sl in heads:
            p = [_dot_nt(t[:, sl], t[:, sl]) for t in f["ts"]]
            p.append(_dot_nt(f["q_bf"][:, sl], f["k_bf"][:, sl]))
            prods.append(p)
    yield

    intra = []
    for n, (bi, d, *_) in enumerate(streams):
        for h in range(HEADS):
            p = prods[n * HEADS + h]
            a = p[N_LEVELS] * amask_ref[d, diag_row:diag_row + c, :]
            for l in range(N_LEVELS):
                a = a + p[l] * amask_ref[d, l * c:(l + 1) * c, :]
            intra.append(a.astype(BF16))
    yield

    for n, ((bi, d, _, v_ref, _, o_ref, _), f) in enumerate(zip(streams, feats)):
        for h, sl in enumerate(heads):
            v = v_ref[bi, rows_of(d), sl]
            state = st_ref[bi, d, h]
            o = _dot(intra[n * HEADS + h], v) + _dot_nt(f["q_in"][:, sl], state.astype(BF16))
            o_ref[bi, rows_of(d), sl] = o.astype(BF16)
            st_ref[bi, d, h] = state * f["total"][:, sl] + _dot_tn(v, f["k_out"][:, sl])
    yield


def _scan_chunk_maps(n_lat, n_ctx):
    n = n_lat + n_ctx

    def fwd(i, c):
        return (i, jnp.where(c < n_ctx, n_lat + c, c - n_ctx), 0)

    def bwd(i, c):
        return (i, jnp.where(c < n_ctx, n - 1 - c, n - 1 - c), 0)

    return fwd, bwd


def _mlstm_constants():
    c = ML_CHUNK
    tri = np.tril(np.ones((c, c), np.float32))
    cum = np.stack([tri, tri.T])
    negmask = np.where(cum > 0.5, 0.0, -1e30).astype(np.float32)
    sel = np.zeros((2, 3, 128, HEADS, HEAD_DIM), np.float32)
    for d in range(2):
        for h in range(HEADS):
            sel[d, :, 8 * d + h, h, :] = 1.0
    sel = sel.reshape(2, 3 * 128, HEADS * HEAD_DIM)
    return jnp.asarray(cum, BF16), jnp.asarray(negmask, F32), jnp.asarray(sel, BF16)


def _running_max(x, reverse):
    n, lanes = x.shape
    row = lax.broadcasted_iota(jnp.int32, (n, lanes), 0)
    shift = 1
    while shift < n:
        if shift % 8 == 0:
            pad = jnp.full((shift, lanes), -jnp.inf, x.dtype)
            shifted = (jnp.concatenate([x[shift:], pad], axis=0) if reverse
                       else jnp.concatenate([pad, x[:n - shift]], axis=0))
        elif reverse:
            shifted = jnp.where(row >= n - shift, -jnp.inf, pltpu.roll(x, n - shift, axis=0))
        else:
            shifted = jnp.where(row < shift, -jnp.inf, pltpu.roll(x, shift, axis=0))
        x = jnp.maximum(x, shifted)
        shift *= 2
    return x


def _mlstm_phases(qf_ref, qb_ref, kf_ref, kb_ref, vf_ref, vb_ref, gf_ref, gb_ref,
                  cum_ref, neg_ref, sel_ref, hf_ref, hb_ref, cn_ref, m_ref):
    c = ML_CHUNK
    dirs = ((qf_ref, kf_ref, vf_ref, gf_ref, hf_ref, c - 1), (qb_ref, kb_ref, vb_ref, gb_ref, hb_ref, 0))
    streams = [(bi, d) + dirs[d] for bi in range(qf_ref.shape[0]) for d in range(2)]
    heads = [slice(h * HEAD_DIM, (h + 1) * HEAD_DIM) for h in range(HEADS)]
    lane = lax.broadcasted_iota(jnp.int32, (c, 128), 1)
    cat3 = lambda a: jnp.concatenate(_split3(a), axis=1)

    cums = []
    for bi, d, _, _, _, g_ref, _, _ in streams:
        sums = _dot(cum_ref[d], cat3(g_ref[bi]))
        cums.append(sums[:, :128] + sums[:, 128:256] + sums[:, 256:])
    yield

    small = []
    for (bi, d, _, _, _, g_ref, _, end_row), cs in zip(streams, cums):
        valid = jnp.logical_and(lane >= 8 * d, lane < 8 * d + HEADS)
        b_sh = jnp.where(valid, pltpu.roll(cs, 128 - HEADS, axis=1), 0.0)
        u = jnp.where(valid, g_ref[bi], 0.0) - b_sh
        mu = m_ref[bi, d][0:1, :]
        m_run = jnp.maximum(_running_max(u, d == 1), mu)
        m_end = m_run[end_row:end_row + 1, :]
        w_inter = jnp.exp(mu - m_run)
        m_ref[bi, d] = jnp.broadcast_to(b_sh[end_row:end_row + 1, :] + m_end, (8, 128))
        small.append(dict(
            m_run=m_run, w_inter=w_inter.astype(BF16), w_k=jnp.exp(u - m_end).astype(BF16),
            floor=jnp.exp(-(b_sh + m_run)).astype(BF16),
            carry=jnp.broadcast_to(w_inter[end_row:end_row + 1, :], (8, 128)), u_t=u.T))
    yield

    wide = []
    for (bi, d, *_), sm in zip(streams, small):
        sel3 = sel_ref[d]
        sel1 = sel_ref[d, 0:128, :]
        wide.append(dict(
            m_run=_dot(cat3(sm["m_run"]), sel3), w_inter=_dot(sm["w_inter"], sel1),
            w_k=_dot(sm["w_k"], sel1), floor=_dot(sm["floor"], sel1),
            carry=_dot(cat3(sm["carry"]), sel3)))

    qk = []
    for bi, d, q_ref, k_ref, *_ in streams:
        qk.append([_dot_nt(q_ref[bi, :, hb], k_ref[bi, :, hb]) for hb in heads])
    yield

    ones = jnp.ones((c, HEAD_DIM), BF16)
    zeros = jnp.zeros((HEAD_DIM - ML_DK, 2 * HEAD_DIM), BF16)
    operands = []
    for n, (bi, d, q_ref, _, v_ref, _, _, _) in enumerate(streams):
        for h, hb in enumerate(heads):
            u_row = jnp.broadcast_to(small[n]["u_t"][8 * d + h:8 * d + h + 1, :], (c, c))
            w = jnp.exp(u_row - wide[n]["m_run"][:, hb] + neg_ref[d])
            s = (qk[n][h] * w).astype(BF16)
            q_w = (q_ref[bi, :, hb].astype(F32) * wide[n]["w_inter"][:, hb]).astype(BF16)
            v1 = jnp.concatenate([v_ref[bi, :, hb], ones], axis=1)
            cn = cn_ref[bi, d, h]
            rhs = jnp.concatenate([v1, cn.astype(BF16), zeros], axis=0)
            operands.append((jnp.concatenate([s, q_w], axis=1), rhs, v1, cn))
    yield

    for n, (bi, d, _, k_ref, _, _, h_ref, _) in enumerate(streams):
        for h, hb in enumerate(heads):
            lhs, rhs, v1, cn = operands[n * HEADS + h]
            res = _dot(lhs, rhs)
            k_w = (k_ref[bi, :, hb].astype(F32) * wide[n]["w_k"][:, hb]).astype(BF16)
            upd = _dot_tn(k_w, v1)[:ML_DK]
            h_ref[bi, :, hb] = (res[:, :HEAD_DIM] / jnp.maximum(
                jnp.abs(res[:, HEAD_DIM:]), wide[n]["floor"][:, hb])).astype(BF16)
            carry = wide[n]["carry"][0:1, hb]
            cn_ref[bi, d, h] = cn * jnp.concatenate([carry, carry], axis=1) + upd
    yield


def _scans_kernel(hq_f, hq_b, hv_f, hv_b, lf_f, lf_b, mq_f, mq_b, mk_f, mk_b, mv_f, mv_b, mg_f, mg_b,
                  hg_cum, hg_side, hg_amask, ml_cum, ml_neg, ml_sel,
                  hgo_f, hgo_b, mlo_f, mlo_b, st_ref, g_ref, cn_ref, m_ref):
    @pl.when(pl.program_id(1) == 0)
    def _():
        st_ref[...] = jnp.zeros_like(st_ref)
        cn_ref[...] = jnp.zeros_like(cn_ref)
        m_ref[...] = jnp.zeros_like(m_ref)

    hg_args = (hq_f, hq_b, hv_f, hv_b, lf_f, lf_b, hg_cum, hg_side, hg_amask, hgo_f, hgo_b, st_ref, g_ref)
    hgrn2 = itertools.chain(_hgrn2_phases(0, *hg_args), _hgrn2_phases(1, *hg_args))
    mlstm = _mlstm_phases(mq_f, mq_b, mk_f, mk_b, mv_f, mv_b, mg_f, mg_b, ml_cum, ml_neg, ml_sel,
                          mlo_f, mlo_b, cn_ref, m_ref)
    exhausted = object()
    progressed = True
    while progressed:
        progressed = False
        for phases, count in ((hgrn2, 2), (mlstm, 1)):
            for _ in range(count):
                progressed = (next(phases, exhausted) is not exhausted) or progressed


def _scans(hq, hv, lff, lfb, mq, mk, mv, mg, hg_consts, ml_consts):
    b, t, w = hq.shape
    n_ctx = CTX_LEN // ML_CHUNK
    n_lat = t // ML_CHUNK - n_ctx
    fwd, bwd = _scan_chunk_maps(n_lat, n_ctx)
    nb = math.gcd(SCAN_ROWS, b)
    blk = lambda width, m: pl.BlockSpec((nb, ML_CHUNK, width), m)
    both = lambda width: [blk(width, fwd), blk(width, bwd)]
    const = lambda a: pl.BlockSpec(a.shape, lambda i, c: (0, 0, 0))
    consts = list(hg_consts) + list(ml_consts)
    out = jax.ShapeDtypeStruct((b, t, w), BF16)
    return pl.pallas_call(
        _scans_kernel,
        out_shape=(out, out, out, out),
        grid=(b // nb, t // ML_CHUNK),
        in_specs=(both(w) + both(w) + [blk(w, fwd), blk(w, bwd)] + both(w) + both(w) + both(w) + both(128)
                  + [const(a) for a in consts]),
        out_specs=tuple(both(w) + both(w)),
        scratch_shapes=[pltpu.VMEM((nb, 2, HEADS, HEAD_DIM, HEAD_DIM), F32),
                        pltpu.VMEM((nb, 2, CHUNK, w), F32),
                        pltpu.VMEM((nb, 2, HEADS, ML_DK, 2 * HEAD_DIM), F32),
                        pltpu.VMEM((nb, 2, 8, 128), F32)],
        compiler_params=_compiler_params(2),
        name="scans",
    )(hq, hq, hv, hv, lff, lfb, mq, mq, mk, mk, mv, mv, mg, mg, *consts)


def _route_block(hx, rwt_ref, rbt_ref, tri_ref, hxa_o, hxb_o, route_o, cnt_o, carry_ref):
    @pl.when(jnp.logical_and(pl.program_id(0) == 0, pl.program_id(1) == 0))
    def _():
        carry_ref[...] = jnp.zeros_like(carry_ref)

    hxa_o[...] = _pack_pairs(hx[:, 0:256], hx[:, 256:512])
    hxb_o[...] = _pack_pairs(hx[:, 512:768], hx[:, 768:1024])
    h_hi, h_mid, h_lo = _split3(hx)
    r_hi, r_mid, _ = _split3(rwt_ref[...])
    r_both = jnp.concatenate([r_hi.astype(F32), r_mid.astype(F32)], axis=0).astype(BF16)
    by_hi = _dot_nt(r_both, h_hi)
    by_mid = _dot_nt(r_both, h_mid)
    by_lo = _dot_nt(r_hi, h_lo)
    logits = ((by_hi[:N_EXPERTS] + by_hi[N_EXPERTS:]) + (by_mid[:N_EXPERTS] + by_mid[N_EXPERTS:])
              + by_lo + rbt_ref[...])
    sub = lax.broadcasted_iota(jnp.int32, logits.shape, 0).astype(F32)
    top1 = jnp.max(logits, axis=0, keepdims=True)
    idx1 = jnp.min(jnp.where(logits == top1, sub, float(N_EXPERTS)), axis=0, keepdims=True)
    rest = jnp.where(sub == idx1, -jnp.inf, logits)
    top2 = jnp.max(rest, axis=0, keepdims=True)
    idx2 = jnp.min(jnp.where(rest == top2, sub, float(N_EXPERTS)), axis=0, keepdims=True)
    p1 = 1.0 / (1.0 + jnp.exp(top2 - top1))
    p2 = 1.0 - p1
    oh1 = jnp.where(sub == idx1, 1.0, 0.0)
    oh2 = jnp.where(sub == idx2, 1.0, 0.0)
    both = oh1 + oh2
    before = _dot(both.astype(BF16), tri_ref[...]) + carry_ref[...]
    rank1 = jnp.sum(before * oh1, axis=0, keepdims=True)
    rank2 = jnp.sum(before * oh2, axis=0, keepdims=True)
    total = carry_ref[...] + jnp.broadcast_to(jnp.sum(both, axis=1, keepdims=True), both.shape)
    carry_ref[...] = total
    cnt_o[...] = total
    route = jnp.zeros_like(logits)
    for name, val in (("e1", idx1), ("e2", idx2), ("rank1", rank1), ("rank2", rank2), ("p1", p1), ("p2", p2)):
        route = jnp.where(sub == float(ROUTE_ROWS[name]), val, route)
    route_o[...] = route


def _mixout_kernel(n_lat_blocks, with_route, x_ref, c_ref, hf_ref, hb_ref, mf_ref, mb_ref, hg_ref, mo_ref,
                   nrm_ref, w_ref, mod_ref, ln_ref, *rest):
    x = x_ref[0] if with_route else jnp.where(pl.program_id(1) >= n_lat_blocks, c_ref[0], x_ref[0])
    parts = []
    streams = ((hf_ref, hb_ref, hg_ref, 0), (mf_ref, mb_ref, mo_ref, 1))
    for f_ref, b_ref, gate_ref, row in streams:
        o = f_ref[0].astype(F32) + b_ref[0].astype(F32)
        gate = gate_ref[0].astype(F32)
        for h in range(HEADS):
            sl = slice(h * HEAD_DIM, (h + 1) * HEAD_DIM)
            oh = o[:, sl]
            ms = jnp.mean(oh * oh, axis=-1, keepdims=True)
            parts.append((oh * lax.rsqrt(ms + RMS_EPS) * nrm_ref[row:row + 1, sl] * gate[:, sl]).astype(BF16))
    y = jnp.concatenate(parts, axis=1)
    mix = _dot(y, w_ref[...])
    r = DEEPNORM_ALPHA * x + mod_ref[0, 0, 2:3, :] * mix
    xn = _layer_norm(r, ln_ref[0:1, :], ln_ref[1:2, :])
    if not with_route:
        (o_ref,) = rest
        o_ref[0] = xn
        return
    rwt_ref, rbt_ref, tri_ref, o_ref, hxa_o, hxb_o, route_o, cnt_o, carry_ref = rest
    o_ref[0] = xn
    hx = xn * (1.0 + mod_ref[0, 0, 4:5, :]) + mod_ref[0, 0, 3:4, :]
    _route_block(hx, rwt_ref, rbt_ref, tri_ref, hxa_o, hxb_o, route_o, cnt_o, carry_ref)


def _mixout(lat, ctx, ctx_block, hg_f, hg_b, ml_f, ml_b, hg_gate, ml_gate, norms, w_bf, modall, ln,
            n_lat_blocks, n_blocks, router=None):
    b, _, d = lat.shape
    tm = TOKEN_BLOCK
    if router is not None:
        assert n_blocks == n_lat_blocks
        tm = ROUTE_BLOCK
        n_blocks = n_lat_blocks = n_blocks * TOKEN_BLOCK // tm
        ctx_block = 0
    tok = lambda width: pl.BlockSpec((1, tm, width), lambda i, j: (i, j, 0))
    const = lambda shape: pl.BlockSpec(shape, lambda i, j: (0,) * len(shape))
    in_specs = _stream_specs(lat, ctx, ctx_block, n_lat_blocks, tm) + [
        tok(512), tok(512), tok(512), tok(512), tok(512), tok(512),
        const((2, 512)),
        pl.BlockSpec((d, d), lambda i, j: (0, 0), pipeline_mode=pl.Buffered(1)),
        pl.BlockSpec((1, 1, 6, d), lambda i, j: (i, jnp.where(j >= n_lat_blocks, 1, 0), 0, 0)),
        const((2, d)),
    ]
    args = [lat, ctx, hg_f, hg_b, ml_f, ml_b, hg_gate, ml_gate, norms, w_bf, modall, ln]
    out_shape = [jax.ShapeDtypeStruct((b, n_blocks * tm, d), F32)]
    out_specs = [tok(d)]
    scratch = []
    if router is not None:
        rw, rb = router
        n = b * n_blocks * tm
        rows = lambda width: pl.BlockSpec((tm, width), lambda i, j: (i * n_blocks + j, 0))
        in_specs += [const((N_EXPERTS, d)), const((N_EXPERTS, tm)), const((tm, tm))]
        args += [rw.T, jnp.broadcast_to(rb.reshape(N_EXPERTS, 1), (N_EXPERTS, tm)),
                 jnp.asarray(np.triu(np.ones((tm, tm), np.float32), 1), BF16)]
        out_shape += [jax.ShapeDtypeStruct((n, PACK_W), jnp.uint32), jax.ShapeDtypeStruct((n, PACK_W), jnp.uint32),
                      jax.ShapeDtypeStruct((N_EXPERTS, n), F32), jax.ShapeDtypeStruct((N_EXPERTS, tm), F32)]
        out_specs += [rows(PACK_W), rows(PACK_W),
                      pl.BlockSpec((N_EXPERTS, tm), lambda i, j: (0, i * n_blocks + j)), const((N_EXPERTS, tm))]
        scratch = [pltpu.VMEM((N_EXPERTS, tm), F32)]
    return pl.pallas_call(
        functools.partial(_mixout_kernel, n_lat_blocks, router is not None),
        out_shape=tuple(out_shape),
        grid=(b, n_blocks),
        in_specs=in_specs,
        out_specs=tuple(out_specs),
        scratch_shapes=scratch,
        compiler_params=_compiler_params(2),
        name="mix_out_ln_route" if router is not None else "mix_out_ln",
    )(*args)


FF_CHUNK = 256


def _ffn_kernel(x_ref, mod_ref, wgu_ref, wd_ref, ln_ref, o_ref, h_ref):
    x = x_ref[0]
    hx = (x * (1.0 + mod_ref[0, 0, 4:5, :]) + mod_ref[0, 0, 3:4, :]).astype(BF16)
    for c0 in range(0, D_FF, FF_CHUNK):
        g = _dot(hx, wgu_ref[:, c0:c0 + FF_CHUNK])
        u = _dot(hx, wgu_ref[:, D_FF + c0:D_FF + c0 + FF_CHUNK])
        h_ref[:, c0:c0 + FF_CHUNK] = (_silu(g) * u).astype(BF16)
    f = _dot(h_ref[...], wd_ref[...])
    r = DEEPNORM_ALPHA * x + mod_ref[0, 0, 5:6, :] * f
    o_ref[0] = _layer_norm(r, ln_ref[0:1, :], ln_ref[1:2, :])


def _ffn(xt, modall, wgu_bf, wd_bf, ln, n_lat_blocks):
    b, t, d = xt.shape
    tok = pl.BlockSpec((1, TOKEN_BLOCK, d), lambda i, j: (i, j, 0))
    return pl.pallas_call(
        _ffn_kernel,
        out_shape=jax.ShapeDtypeStruct((b, t, d), F32),
        grid=(b, t // TOKEN_BLOCK),
        in_specs=[
            tok,
            pl.BlockSpec((1, 1, 6, d), lambda i, j: (i, jnp.where(j >= n_lat_blocks, 1, 0), 0, 0)),
            pl.BlockSpec((d, 2 * D_FF), lambda i, j: (0, 0), pipeline_mode=pl.Buffered(1)),
            pl.BlockSpec((D_FF, d), lambda i, j: (0, 0), pipeline_mode=pl.Buffered(1)),
            pl.BlockSpec((2, d), lambda i, j: (0, 0)),
        ],
        out_specs=tok,
        scratch_shapes=[pltpu.VMEM((TOKEN_BLOCK, D_FF), BF16)],
        compiler_params=_compiler_params(2),
        name="ffn_dense_ln",
    )(xt, modall, wgu_bf, wd_bf, ln)


EXPERT_CHUNKS = ((0, 256), (256, 256), (512, 256), (768, 256), (1024, 256), (1280, 128))
ROUTE_ROWS = dict(e1=0, e2=1, rank1=2, rank2=3, p1=4, p2=5)
PACK_W = 256
SC_WINDOW = 128


def _pack_pairs(a, b):
    ua = lax.bitcast_convert_type(a.astype(BF16).astype(F32), jnp.uint32)
    ub = lax.bitcast_convert_type(b.astype(BF16).astype(F32), jnp.uint32)
    return (ua >> 16) | ub


def _unpack_pairs(w):
    a = lax.bitcast_convert_type(w << 16, F32)
    b = lax.bitcast_convert_type(w & jnp.uint32(0xFFFF0000), F32)
    return a, b


def _sc_mesh():
    return plsc.VectorSubcoreMesh(core_axis_name="core", subcore_axis_name="subcore")


def _sc_scatter_rows(rows, idx, n_out):
    n_src, width = rows.shape
    m = idx.shape[0]
    src_blocks = n_src // SC_WINDOW

    @functools.partial(pl.kernel, out_type=jax.ShapeDtypeStruct((n_out, width), rows.dtype), mesh=_sc_mesh())
    def scatter(x_hbm, i_hbm, o_hbm):
        def body(x_vmem, i_vmem):
            pltpu.sync_copy(x_vmem, o_hbm.at[i_vmem.at[0]])

        pltpu.emit_pipeline(
            body, grid=(m // SC_WINDOW,),
            in_specs=[pl.BlockSpec((SC_WINDOW, width), lambda i: (i % src_blocks, 0)),
                      pl.BlockSpec((1, SC_WINDOW), lambda i: (0, i))],
            out_specs=[],
            core_axis_name=("core", "subcore"), dimension_semantics=(pltpu.PARALLEL,),
        )(x_hbm, i_hbm)

    return scatter(rows, idx.reshape(1, m))


def _sc_gather_rows(table, idx):
    m = idx.shape[0]
    width = table.shape[1]

    @functools.partial(pl.kernel, out_type=jax.ShapeDtypeStruct((m, width), table.dtype), mesh=_sc_mesh())
    def gather(t_hbm, i_hbm, o_hbm):
        def body(i_vmem, o_vmem):
            pltpu.sync_copy(t_hbm.at[i_vmem.at[0]], o_vmem)

        pltpu.emit_pipeline(
            body, grid=(m // SC_WINDOW,),
            in_specs=[pl.BlockSpec((1, SC_WINDOW), lambda i: (0, i))],
            out_specs=[pl.BlockSpec((SC_WINDOW, width), lambda i: (i, 0))],
            core_axis_name=("core", "subcore"), dimension_semantics=(pltpu.PARALLEL,),
        )(i_hbm, o_hbm)

    return gather(table, idx.reshape(1, m))


CAST_ROWS = 64


def _experts_kernel(te_ref, used_ref, xa_ref, xb_ref, wgu_ref, wd_ref, ya_o, yb_o, h_ref, wgu_bf, wd_bf):
    i = pl.program_id(0)
    changed = jnp.logical_or(i == 0, te_ref[i] != te_ref[jnp.maximum(i - 1, 0)])

    @pl.when(jnp.logical_and(i < used_ref[0], changed))
    def _():
        def cast_gu(r, carry):
            rows = pl.ds(pl.multiple_of(r * CAST_ROWS, CAST_ROWS), CAST_ROWS)
            wgu_bf[rows, :] = wgu_ref[0, rows, :].astype(BF16)
            return carry

        def cast_d(r, carry):
            rows = pl.ds(pl.multiple_of(r * CAST_ROWS, CAST_ROWS), CAST_ROWS)
            wd_bf[rows, :] = wd_ref[0, rows, :].astype(BF16)
            return carry

        lax.fori_loop(0, D_MODEL // CAST_ROWS, cast_gu, 0)
        lax.fori_loop(0, D_EXPERT // CAST_ROWS, cast_d, 0)

    @pl.when(i < used_ref[0])
    def _():
        hx = jnp.concatenate([p.astype(BF16) for w in (xa_ref[...], xb_ref[...]) for p in _unpack_pairs(w)],
                             axis=1)
        for c0, cw in EXPERT_CHUNKS:
            g = _dot(hx, wgu_bf[:, c0:c0 + cw])
            u = _dot(hx, wgu_bf[:, D_EXPERT + c0:D_EXPERT + c0 + cw])
            h_ref[:, c0:c0 + cw] = (_silu(g) * u).astype(BF16)
        f = _dot(h_ref[...], wd_bf[...])
        ya_o[...] = _pack_pairs(f[:, 0:256], f[:, 256:512])
        yb_o[...] = _pack_pairs(f[:, 512:768], f[:, 768:1024])


def _moe_experts(tile_expert, n_used, xa, xb, wgu, wd):
    r = xa.shape[0]
    tm = EXPERT_TILE
    rows = pl.BlockSpec((tm, PACK_W), lambda i, te, nu: (i, 0))
    out = jax.ShapeDtypeStruct((r, PACK_W), jnp.uint32)
    return pl.pallas_call(
        _experts_kernel,
        out_shape=(out, out),
        grid_spec=pltpu.PrefetchScalarGridSpec(
            num_scalar_prefetch=2,
            grid=(r // tm,),
            in_specs=[rows, rows,
                      pl.BlockSpec((1, D_MODEL, 2 * D_EXPERT), lambda i, te, nu: (te[i], 0, 0)),
                      pl.BlockSpec((1, D_EXPERT, D_MODEL), lambda i, te, nu: (te[i], 0, 0))],
            out_specs=(rows, rows),
            scratch_shapes=[pltpu.VMEM((tm, D_EXPERT), BF16),
                            pltpu.VMEM((D_MODEL, 2 * D_EXPERT), BF16),
                            pltpu.VMEM((D_EXPERT, D_MODEL), BF16)]),
        compiler_params=_compiler_params(1),
        name="moe_experts",
    )(tile_expert, n_used, xa, xb, wgu, wd)


def _combine_kernel(x_ref, mod_ref, route_ref, y1a_ref, y1b_ref, y2a_ref, y2b_ref, ln_ref, o_ref):
    route = route_ref[...].T
    f = None
    for ya_ref, yb_ref, name in ((y1a_ref, y1b_ref, "p1"), (y2a_ref, y2b_ref, "p2")):
        col = ROUTE_ROWS[name]
        y = jnp.concatenate(_unpack_pairs(ya_ref[...]) + _unpack_pairs(yb_ref[...]), axis=1)
        term = route[:, col:col + 1] * y
        f = term if f is None else f + term
    r = DEEPNORM_ALPHA * x_ref[0] + mod_ref[0, 0, 5:6, :] * f
    o_ref[0] = _layer_norm(r, ln_ref[0:1, :], ln_ref[1:2, :])


def _moe_combine(xt, modall, route, yga, ygb, ln, seq):
    b, _, d = xt.shape
    tm = ROUTE_BLOCK
    nj = seq // tm
    nblk = b * nj
    first = lambda width: pl.BlockSpec((tm, width), lambda i, j: (i * nj + j, 0))
    second = lambda width: pl.BlockSpec((tm, width), lambda i, j: (nblk + i * nj + j, 0))
    return pl.pallas_call(
        _combine_kernel,
        out_shape=jax.ShapeDtypeStruct((b, seq, d), F32),
        grid=(b, nj),
        in_specs=[
            pl.BlockSpec((1, tm, d), lambda i, j: (i, j, 0)),
            pl.BlockSpec((1, 1, 6, d), lambda i, j: (i, 0, 0, 0)),
            pl.BlockSpec((N_EXPERTS, tm), lambda i, j: (0, i * nj + j)),
            first(PACK_W), first(PACK_W), second(PACK_W), second(PACK_W),
            pl.BlockSpec((2, d), lambda i, j: (0, 0)),
        ],
        out_specs=pl.BlockSpec((1, tm, d), lambda i, j: (i, j, 0)),
        compiler_params=_compiler_params(2),
        name="moe_combine_ln",
    )(xt, modall, route, yga, ygb, yga, ygb, ln)


def _moe(xt, modall, hxa, hxb, route, counts, wgu, wd, ln, seq):
    b = xt.shape[0]
    n = b * seq
    tm = EXPERT_TILE
    counts = counts[:, 0].astype(jnp.int32)
    sizes = (counts + tm - 1) // tm * tm
    ends = jnp.cumsum(sizes)
    starts = ends - sizes
    col = lambda name: route[ROUTE_ROWS[name]].astype(jnp.int32)
    pos = jnp.concatenate([jnp.take(starts, col("e1")) + col("rank1"),
                           jnp.take(starts, col("e2")) + col("rank2")])
    n_rows = 2 * n + N_EXPERTS * tm
    tile_start = jnp.arange(n_rows // tm, dtype=jnp.int32) * tm
    tile_expert = jnp.minimum(jnp.sum(tile_start[:, None] >= ends[None, :], axis=1), N_EXPERTS - 1)
    n_used = (ends[-1:] // tm).astype(jnp.int32)
    xa = _sc_scatter_rows(hxa, pos, n_rows)
    xb = _sc_scatter_rows(hxb, pos, n_rows)
    ya, yb = _moe_experts(tile_expert.astype(jnp.int32), n_used, xa, xb, wgu, wd)
    return _moe_combine(xt, modall, route, _sc_gather_rows(ya, pos), _sc_gather_rows(yb, pos), ln, seq)


def _permute_w_in(w):
    gates = jnp.pad(w[:, 3584:3600], ((0, 0), (0, 112)))
    return jnp.concatenate([w[:, :3584], w[:, 3600:4112], gates], axis=1).astype(BF16)


def kernel(x, c, ctx, c_ctx, w_ada, b_ada, w_in, ml_conv_w, ml_conv_b, hg_lower_bound, ml_gate_bias,
           hg_norm, ml_norm, w_out, ln_g, ln_b, ffn_w_gate_up, ffn_w_down, router_w, router_b,
           moe_w_gate_up, moe_w_down):
    bsz, seq, d = x.shape
    depth = w_ada.shape[0]
    assert depth == DEPTH and d == D_MODEL and ctx.shape[1] == CTX_LEN and seq % ROUTE_BLOCK == 0
    n_lat_blocks = seq // TOKEN_BLOCK
    n_blocks = n_lat_blocks + CTX_LEN // TOKEN_BLOCK

    lb = jnp.cumsum(jax.nn.softmax(hg_lower_bound.astype(F32), axis=0), axis=0)
    lb = lb - lb[0]

    cond = jnp.concatenate([c, c_ctx[None, :], jnp.zeros((16 - bsz - 1, d), F32)], axis=0)
    mod = _modulation(cond, w_ada, b_ada)
    mod = mod.reshape(depth, 16, 6, d)

    hg_consts = _hgrn2_constants()
    ml_consts = _mlstm_constants()

    lat, cx, cx_block = x, ctx, 0
    out = None
    for l in range(depth):
        last = l == depth - 1
        mod_c = jnp.broadcast_to(mod[l, bsz][None], (bsz, 6, d))
        modall = jnp.stack([mod[l, :bsz], mod_c], axis=1)
        lbp = lb[l]
        convp = jnp.concatenate([ml_conv_w[l], ml_conv_b[l][None]], axis=0)
        gb = jnp.pad(ml_gate_bias[l].reshape(1, -1), ((0, 0), (0, 112)))
        feats = _inproj(lat, cx, cx_block, modall, _permute_w_in(w_in[l]), lbp, convp, gb, n_lat_blocks)
        hq, lff, lfb, hv, hg_gate, mq, mk, mv, ml_gate, mg = feats
        hg_f, hg_b, ml_f, ml_b = _scans(hq, hv, lff, lfb, mq, mk, mv, mg, hg_consts, ml_consts)
        norms = jnp.stack([hg_norm[l].reshape(-1), ml_norm[l].reshape(-1)])
        jj = l // 2
        dense = l % 2 == 0
        assert dense or last
        mixed = _mixout(lat, cx, cx_block, hg_f, hg_b, ml_f, ml_b, hg_gate, ml_gate, norms,
                        w_out[l].astype(BF16), modall, jnp.stack([ln_g[l, 0], ln_b[l, 0]]), n_lat_blocks,
                        n_lat_blocks if last else n_blocks,
                        router=None if dense else (router_w[jj], router_b[jj]))
        ln1 = jnp.stack([ln_g[l, 1], ln_b[l, 1]])
        if dense:
            xt = _ffn(mixed[0], modall, ffn_w_gate_up[jj].astype(BF16), ffn_w_down[jj].astype(BF16), ln1,
                      n_lat_blocks)
            lat, cx, cx_block = xt, xt, n_lat_blocks
            out = xt[:, :seq]
        else:
            out = _moe(*mixed[:1], modall, *mixed[1:], moe_w_gate_up[jj], moe_w_down[jj], ln1, seq)
    return out
```
